```python
import numpy as np
import jax
import jax.numpy as jnp
from jax import lax

D_MODEL = 1024
BATCH = 32
SEQ = 256
DEPTH = 2
DEC_BATCH = 2
DEC_SEQ = 2048
PAST_LEN = 512

GRID_W = 64
HEAD_DIM = 64
MIX_WIDTH = D_MODEL
A_HEADS = 4
A_DK = 64
A_DV = 64
A_WIDTH = A_HEADS * A_DV
B_WIDTH = D_MODEL // 4
CONV_K = 31
CONV_PAD = (CONV_K - 1) // 2
C_HEADS = 8
C_KV_HEADS = 2
C_WIDTH = C_HEADS * HEAD_DIM
D_FF = 2816
HGRN_CHUNK = 16
Q_BLOCK = 128
ROPE_THETA = 10000.0
ROPE_PAIRS = HEAD_DIM // 4
N_MOD = 9
ALPHA = (2 * DEPTH) ** 0.25
BETA = (8 * DEPTH) ** -0.25
F_MIN = 1e-6
IN_SIZES = (A_WIDTH, A_WIDTH, A_WIDTH, A_WIDTH, A_WIDTH, 2 * B_WIDTH, C_WIDTH,
            C_KV_HEADS * HEAD_DIM, C_KV_HEADS * HEAD_DIM)
IN_WIDTH = 5 * A_WIDTH + 2 * B_WIDTH + C_WIDTH + 2 * C_KV_HEADS * HEAD_DIM

kernel_name = "hybrid_hgrn2_conformer_gqa_diffusion_step"


def _layer_norm(x, g, b, eps=1e-5):
    xf = x.astype(jnp.float32)
    mu = jnp.mean(xf, axis=-1, keepdims=True)
    var = jnp.mean(jnp.square(xf - mu), axis=-1, keepdims=True)
    return ((xf - mu) * lax.rsqrt(var + eps) * g + b).astype(x.dtype)


def _rms_norm(x, g, eps=1e-6):
    xf = x.astype(jnp.float32)
    return (xf * lax.rsqrt(jnp.mean(jnp.square(xf), axis=-1, keepdims=True) + eps) * g).astype(x.dtype)


def _modulate(x, shift, scale):
    return x * (1.0 + scale) + shift


def _swiglu(h, w_in, w_out):
    gate, up = jnp.split(h @ w_in, 2, axis=-1)
    return (jax.nn.silu(gate) * up) @ w_out


def _axial_rope_tables(seq):
    rows = seq // GRID_W
    row_id = jnp.repeat(jnp.arange(rows), GRID_W).astype(jnp.float32)
    col_id = jnp.tile(jnp.arange(GRID_W), rows).astype(jnp.float32)
    inv = ROPE_THETA ** (-jnp.arange(ROPE_PAIRS, dtype=jnp.float32) / ROPE_PAIRS)
    ang = jnp.stack([row_id[:, None] * inv, col_id[:, None] * inv], axis=0)
    return jnp.cos(ang), jnp.sin(ang)


def _apply_rope(x, cos, sin):
    halves = jnp.split(x.astype(jnp.float32), 2, axis=-1)
    out = []
    for axis in range(2):
        x1, x2 = jnp.split(halves[axis], 2, axis=-1)
        cs = cos[axis][None, :, None, :]
        sn = sin[axis][None, :, None, :]
        out += [x1 * cs - x2 * sn, x2 * cs + x1 * sn]
    return jnp.concatenate(out, axis=-1).astype(x.dtype)


def _block_attention(q, k, v):
    bsz, seq_q, heads, hd = q.shape
    kvh = k.shape[2]
    groups = heads // kvh
    nblk = seq_q // Q_BLOCK
    qb = jnp.moveaxis(q.reshape(bsz, nblk, Q_BLOCK, kvh, groups, hd), 1, 0)
    scale = hd ** -0.5

    def one_block(q_blk):
        s = jnp.einsum("bqkgd,bskd->bkgqs", q_blk, k).astype(jnp.float32) * scale
        p = jax.nn.softmax(s, axis=-1).astype(v.dtype)
        return jnp.einsum("bkgqs,bskd->bqkgd", p, v)

    o = lax.map(one_block, qb)
    return jnp.moveaxis(o, 0, 1).reshape(bsz, seq_q, heads * hd)


def _hgrn_scan(q, k, v, logf, s0):
    bsz, seq, heads, _ = q.shape
    dv = v.shape[-1]
    n = seq // HGRN_CHUNK
    chunk = lambda t: t.astype(jnp.float32).reshape(bsz, n, HGRN_CHUNK, heads, t.shape[-1])
    q, k, v, logf = chunk(q), chunk(k), chunk(v), chunk(logf)
    a = jnp.cumsum(logf, axis=2)
    a_last = a[:, :, -1]
    tri = jnp.tril(jnp.ones((HGRN_CHUNK, HGRN_CHUNK), dtype=bool))[None, None, :, :, None, None]
    diff = a[:, :, :, None] - a[:, :, None, :]
    decay = jnp.where(tri, jnp.exp(jnp.where(tri, diff, 0.0)), 0.0)
    scores = jnp.einsum("bntshd,bnshd->bnhts", q[:, :, :, None] * decay, k)
    o_intra = jnp.einsum("bnhts,bnshe->bnthe", scores, v)
    kv = jnp.einsum("bnshd,bnshe->bnhde", k * jnp.exp(a_last[:, :, None] - a), v)

    def step(state, inp):
        dec, kv_n = inp
        return dec[..., None] * state + kv_n, state

    s_fin, s_prev = lax.scan(step, s0.astype(jnp.float32),
                             (jnp.moveaxis(jnp.exp(a_last), 1, 0), jnp.moveaxis(kv, 1, 0)))
    s_prev = jnp.moveaxis(s_prev, 0, 1)
    o_inter = jnp.einsum("bnthd,bnhde->bnthe", q * jnp.exp(a), s_prev)
    return (o_intra + o_inter).reshape(bsz, seq, heads, dv), s_fin


def _depthwise_conv(u, w, b):
    out = lax.conv_general_dilated(
        u, w[:, None, :].astype(u.dtype), window_strides=(1,),
        padding=[(CONV_PAD, CONV_PAD)], dimension_numbers=("NWC", "WIO", "NWC"),
        feature_group_count=u.shape[-1])
    return out + b


def _mixer(h, p, lb, ctx):
    bsz, seq, _ = h.shape
    z = h @ p["w_in"]
    cuts = [int(i) for i in np.cumsum(IN_SIZES)[:-1]]
    zq, zi, zff, zfb, zg, zglu, cq, ck, cv = jnp.split(z, cuts, axis=-1)

    heads_a = lambda t: t.reshape(bsz, seq, A_HEADS, -1)
    lb_h = lb.reshape(A_HEADS, A_DK)

    def forget(zf):
        f = lb_h + (1.0 - lb_h) * jax.nn.sigmoid(heads_a(zf).astype(jnp.float32))
        return 1.0 - f, jnp.log(jnp.maximum(f, F_MIN))

    qa, ia = heads_a(zq), heads_a(zi)
    k_fw, logf_fw = forget(zff)
    k_bw, logf_bw = forget(zfb)
    s0 = jnp.zeros((bsz, 2, A_HEADS, A_DK, A_DV), jnp.float32) if ctx is None else ctx[2]
    rev = lambda t: t[:, ::-1]
    o_fw, s_fw = _hgrn_scan(qa, k_fw, ia, logf_fw, s0[:, 0])
    o_bw, s_bw = _hgrn_scan(rev(qa), rev(k_bw), rev(ia), rev(logf_bw), s0[:, 1])
    o_a = _rms_norm(o_fw + rev(o_bw), p["hgrn_norm_g"].reshape(A_HEADS, A_DV))
    o_a = (o_a * jax.nn.silu(heads_a(zg).astype(jnp.float32))).astype(h.dtype).reshape(bsz, seq, A_WIDTH)

    glu_a, glu_b = jnp.split(zglu, 2, axis=-1)
    u = _depthwise_conv(glu_a * jax.nn.sigmoid(glu_b), p["conv_w"], p["conv_b"])
    o_conv = jax.nn.silu(_layer_norm(u, p["conv_ln_g"], p["conv_ln_b"]))

    qc = _rms_norm(cq.reshape(bsz, seq, C_HEADS, HEAD_DIM), p["q_norm_g"])
    kc = _rms_norm(ck.reshape(bsz, seq, C_KV_HEADS, HEAD_DIM), p["k_norm_g"])
    vc = cv.reshape(bsz, seq, C_KV_HEADS, HEAD_DIM)
    if ctx is None:
        o_c = _block_attention(qc, kc, vc)
        new = (kc, vc, jnp.stack([s_fw, s_bw], axis=1))
    else:
        cos, sin = _axial_rope_tables(seq)
        k_all = jnp.concatenate([ctx[0].astype(kc.dtype), _apply_rope(kc, cos, sin)], axis=1)
        v_all = jnp.concatenate([ctx[1].astype(vc.dtype), vc], axis=1)
        o_c = _block_attention(_apply_rope(qc, cos, sin), k_all, v_all)
        new = None

    y = jnp.concatenate([o_a, o_conv, o_c], axis=-1) @ p["w_out"]
    return y, new


def _layer(x, cond, p, lb, ctx):
    mod = (jax.nn.silu(cond) @ p["w_mod"] + p["b_mod"])[:, None, :]
    sh1, sc1, g1, sh2, sc2, g2, sh3, sc3, g3 = jnp.split(mod, N_MOD, axis=-1)
    f1 = _swiglu(_modulate(x, sh1, sc1), p["ffn1_w_in"], p["ffn1_w_out"])
    x = _layer_norm(ALPHA * x + 0.5 * g1 * f1, p["ln_g"][0], p["ln_b"][0])
    y, new = _mixer(_modulate(x, sh2, sc2), p, lb, ctx)
    x = _layer_norm(ALPHA * x + g2 * y, p["ln_g"][1], p["ln_b"][1])
    f2 = _swiglu(_modulate(x, sh3, sc3), p["ffn2_w_in"], p["ffn2_w_out"])
    x = _layer_norm(ALPHA * x + 0.5 * g3 * f2, p["ln_g"][2], p["ln_b"][2])
    return x, new


def setup_inputs(seed: int = 0) -> dict:
    key = jax.random.key(seed)
    ks = jax.random.split(key, 26)

    def nrm(i, shape, scale):
        return jax.random.normal(ks[i], shape, jnp.float32) * scale

    return {
        "x_prompt": nrm(0, (BATCH, SEQ, D_MODEL), 1.0),
        "x_sample": nrm(1, (DEC_BATCH, DEC_SEQ, D_MODEL), 1.0),
        "cache_k": nrm(2, (DEC_BATCH, DEPTH, PAST_LEN, C_KV_HEADS, HEAD_DIM), 1.0),
        "cache_v": nrm(3, (DEC_BATCH, DEPTH, PAST_LEN, C_KV_HEADS, HEAD_DIM), 1.0),
        "state_hgrn": nrm(4, (DEC_BATCH, DEPTH, 2, A_HEADS, A_DK, A_DV), 0.5),
        "c": nrm(5, (DEC_BATCH, D_MODEL), 1.0),
        "c_ctx": nrm(6, (D_MODEL,), 1.0),
        "w_mod": nrm(7, (DEPTH, D_MODEL, N_MOD * D_MODEL), 0.5 * D_MODEL ** -0.5),
        "b_mod": nrm(8, (DEPTH, N_MOD * D_MODEL), 0.02),
        "ln_g": jnp.ones((DEPTH, 3, D_MODEL), jnp.float32) + nrm(9, (DEPTH, 3, D_MODEL), 0.02),
        "ln_b": nrm(10, (DEPTH, 3, D_MODEL), 0.02),
        "ffn1_w_in": nrm(11, (DEPTH, D_MODEL, 2 * D_FF), D_MODEL ** -0.5),
        "ffn1_w_out": nrm(12, (DEPTH, D_FF, D_MODEL), BETA * D_FF ** -0.5),
        "ffn2_w_in": nrm(13, (DEPTH, D_MODEL, 2 * D_FF), D_MODEL ** -0.5),
        "ffn2_w_out": nrm(14, (DEPTH, D_FF, D_MODEL), BETA * D_FF ** -0.5),
        "w_in": nrm(15, (DEPTH, D_MODEL, IN_WIDTH), D_MODEL ** -0.5),
        "w_out": nrm(16, (DEPTH, MIX_WIDTH, D_MODEL), BETA * MIX_WIDTH ** -0.5),
        "hgrn_lb_logits": nrm(17, (DEPTH, A_HEADS * A_DK), 1.0),
        "hgrn_norm_g": jnp.ones((DEPTH, A_WIDTH), jnp.float32) + nrm(18, (DEPTH, A_WIDTH), 0.02),
        "conv_w": nrm(19, (DEPTH, CONV_K, B_WIDTH), CONV_K ** -0.5),
        "conv_b": nrm(20, (DEPTH, B_WIDTH), 0.02),
        "conv_ln_g": jnp.ones((DEPTH, B_WIDTH), jnp.float32) + nrm(21, (DEPTH, B_WIDTH), 0.02),
        "conv_ln_b": nrm(22, (DEPTH, B_WIDTH), 0.02),
        "q_norm_g": jnp.ones((DEPTH, HEAD_DIM), jnp.float32) + nrm(23, (DEPTH, HEAD_DIM), 0.02),
        "k_norm_g": jnp.ones((DEPTH, HEAD_DIM), jnp.float32) + nrm(24, (DEPTH, HEAD_DIM), 0.02),
    }


def reference(x_prompt, x_sample, cache_k, cache_v, state_hgrn, c, c_ctx, w_mod, b_mod, ln_g, ln_b,
              ffn1_w_in, ffn1_w_out, ffn2_w_in, ffn2_w_out, w_in, w_out, hgrn_lb_logits,
              hgrn_norm_g, conv_w, conv_b, conv_ln_g, conv_ln_b, q_norm_g, k_norm_g):
    lb_soft = jax.nn.softmax(hgrn_lb_logits.astype(jnp.float32), axis=0)
    lbs = jnp.cumsum(lb_soft, axis=0) - lb_soft[0:1]
    xp, xs = x_prompt, x_sample
    ks, vs, ss = [], [], []
    for l in range(DEPTH):
        p = {
            "w_mod": w_mod[l], "b_mod": b_mod[l], "ln_g": ln_g[l], "ln_b": ln_b[l],
            "ffn1_w_in": ffn1_w_in[l], "ffn1_w_out": ffn1_w_out[l],
            "ffn2_w_in": ffn2_w_in[l], "ffn2_w_out": ffn2_w_out[l],
            "w_in": w_in[l], "w_out": w_out[l], "hgrn_norm_g": hgrn_norm_g[l],
            "conv_w": conv_w[l], "conv_b": conv_b[l], "conv_ln_g": conv_ln_g[l],
            "conv_ln_b": conv_ln_b[l], "q_norm_g": q_norm_g[l], "k_norm_g": k_norm_g[l],
        }
        xp, (k_l, v_l, s_l) = _layer(xp, c_ctx[None, :], p, lbs[l], None)
        ks.append(k_l)
        vs.append(v_l)
        ss.append(s_l)
        xs, _ = _layer(xs, c, p, lbs[l], (cache_k[:, l], cache_v[:, l], state_hgrn[:, l]))
    new_cache_k = jnp.stack(ks, axis=1)
    new_cache_v = jnp.stack(vs, axis=1)
    new_state_hgrn = jnp.stack(ss, axis=1)
    return (xp, xs, new_cache_k, new_cache_v, new_state_hgrn)
```

```python
import functools

import jax
import jax.numpy as jnp
from jax import lax
from jax.experimental import pallas as pl
from jax.experimental.pallas import tpu as pltpu

F32 = jnp.float32
BF16 = jnp.bfloat16

D_MODEL = 1024
DEPTH = 2
GRID_W = 64
HEAD_DIM = 64
A_HEADS = 4
A_DK = 64
A_WIDTH = 256
B_WIDTH = 256
CONV_K = 31
CONV_PAD = 15
C_HEADS = 8
C_KV_HEADS = 2
C_WIDTH = 512
KV_WIDTH = C_KV_HEADS * HEAD_DIM
D_FF = 2816
ROPE_THETA = 10000.0
ROPE_PAIRS = 16
N_MOD = 9
ALPHA = (2 * DEPTH) ** 0.25
F_MIN = 1e-6
IN_WIDTH = 5 * A_WIDTH + 2 * B_WIDTH + C_WIDTH + 2 * KV_WIDTH
HG_WIDTH = 5 * A_WIDTH

LANES = 128
SUBLANES = 8
VMEM_BYTES_V7X = 64 * 1024 * 1024

ROW_TILE = 512
FFN_CHUNK = 256
HGRN_CHUNK = 128
HGRN_BASE = 8
CONV_TILE = 256
CONV_HALO = 16
ATTN_Q_TILE_LONG = 128


def _params(semantics, vmem_mb):
    return pltpu.CompilerParams(dimension_semantics=semantics,
                                vmem_limit_bytes=min(vmem_mb * 1024 * 1024, VMEM_BYTES_V7X - (4 << 20)))


def _resident(block_shape, index_map):
    return pl.BlockSpec(block_shape, index_map, pipeline_mode=pl.Buffered(1))


def _sigmoid(x):
    return 1.0 / (1.0 + jnp.exp(-x))


def _layer_norm(y, g, b, eps=1e-5):
    mu = jnp.mean(y, axis=-1, keepdims=True)
    d = y - mu
    var = jnp.mean(d * d, axis=-1, keepdims=True)
    return d * lax.rsqrt(var + eps) * g + b


def _dot(a, b):
    return jnp.dot(a, b, preferred_element_type=F32)


def _dot_nt(a, b):
    return lax.dot_general(a, b, (((1,), (1,)), ((), ())), preferred_element_type=F32)


def _dot_tn(a, b):
    return lax.dot_general(a, b, (((0,), (0,)), ((), ())), preferred_element_type=F32)


def _segment_ones(width, seg):
    r = lax.broadcasted_iota(jnp.int32, (width, width), 0) // seg
    c = lax.broadcasted_iota(jnp.int32, (width, width), 1) // seg
    return (r == c).astype(BF16)


def _segment_sum(x, ones_bd):
    hi = x.astype(BF16)
    lo = (x - hi.astype(F32)).astype(BF16)
    return _dot(hi, ones_bd) + _dot(lo, ones_bd)


def _head_rms_norm(x, gain, eps=1e-6):
    ones_bd = _segment_ones(LANES, HEAD_DIM)
    cols = []
    for c in range(x.shape[1] // LANES):
        xc = x[:, c * LANES:(c + 1) * LANES]
        ms = _segment_sum(xc * xc, ones_bd) * (1.0 / HEAD_DIM)
        cols.append(xc * lax.rsqrt(ms + eps) * gain[:, c * LANES:(c + 1) * LANES])
    return cols[0] if len(cols) == 1 else jnp.concatenate(cols, axis=1)


def _mod_kernel(c_ref, w_ref, b_ref, o_ref):
    c = c_ref[...]
    s = (c * _sigmoid(c)).astype(BF16)
    o_ref[...] = _dot(s, w_ref[...].astype(BF16)) + b_ref[...]


def _modulation(conds, w_mod, b_mod):
    tn = D_MODEL
    return pl.pallas_call(
        _mod_kernel,
        grid=(DEPTH, N_MOD * D_MODEL // tn),
        in_specs=[
            pl.BlockSpec((SUBLANES, D_MODEL), lambda l, j: (0, 0)),
            pl.BlockSpec((None, D_MODEL, tn), lambda l, j: (l, 0, j)),
            pl.BlockSpec((None, 1, tn), lambda l, j: (l, 0, j)),
        ],
        out_specs=pl.BlockSpec((None, SUBLANES, tn), lambda l, j: (l, 0, j)),
        out_shape=jax.ShapeDtypeStruct((DEPTH, SUBLANES, N_MOD * D_MODEL), F32),
        compiler_params=_params(("arbitrary", "arbitrary"), 32),
        name="modulation",
    )(conds, w_mod, b_mod.reshape(DEPTH, 1, N_MOD * D_MODEL))


def _mod_spec(cond_of_block):
    return pl.BlockSpec((None, N_MOD, D_MODEL), lambda i: (cond_of_block(i), 0, 0))


def _ffn_kernel(x_ref, mod_ref, wg_ref, wu_ref, wo_ref, g_ref, b_ref, o_ref, *, sub):
    x = x_ref[...]
    shift = mod_ref[3 * sub:3 * sub + 1, :]
    scale = mod_ref[3 * sub + 1:3 * sub + 2, :]
    gate = mod_ref[3 * sub + 2:3 * sub + 3, :]
    h = (x * (1.0 + scale) + shift).astype(BF16)
    acc = jnp.zeros(x.shape, F32)
    for j in range(D_FF // FFN_CHUNK):
        cols = slice(j * FFN_CHUNK, (j + 1) * FFN_CHUNK)
        gt = _dot(h, wg_ref[:, cols])
        up = _dot(h, wu_ref[:, cols])
        act = (gt * _sigmoid(gt) * up).astype(BF16)
        acc = acc + _dot(act, wo_ref[cols, :])
    y = ALPHA * x + 0.5 * gate * acc
    o_ref[...] = _layer_norm(y, g_ref[...], b_ref[...])


def _ffn(x, mod_l, cond_of_block, w_in, w_out, ln_g, ln_b, layer, sub):
    rows = x.shape[0]
    return pl.pallas_call(
        functools.partial(_ffn_kernel, sub=sub),
        grid=(rows // ROW_TILE,),
        in_specs=[
            pl.BlockSpec((ROW_TILE, D_MODEL), lambda i: (i, 0)),
            _mod_spec(cond_of_block),
            _resident((None, D_MODEL, D_FF), lambda i: (layer, 0, 0)),
            _resident((None, D_MODEL, D_FF), lambda i: (layer, 0, 1)),
            _resident((None, D_FF, D_MODEL), lambda i: (layer, 0, 0)),
            _resident((None, 1, D_MODEL), lambda i: (layer * 3 + sub, 0, 0)),
            _resident((None, 1, D_MODEL), lambda i: (layer * 3 + sub, 0, 0)),
        ],
        out_specs=pl.BlockSpec((ROW_TILE, D_MODEL), lambda i: (i, 0)),
        out_shape=jax.ShapeDtypeStruct((rows, D_MODEL), F32),
        compiler_params=_params(("parallel",), 48),
        name=f"ffn{sub // 2 + 1}",
    )(x, mod_l, w_in, w_in, w_out, ln_g, ln_b)


def _proj_kernel(*refs, rope):
    if rope:
        (x_ref, mod_ref, w_ref, qg_ref, kg_ref, cos_ref, s1_ref, s2_ref,
         zh_ref, u_ref, q_ref, k_ref, v_ref) = refs
    else:
        x_ref, mod_ref, w_ref, qg_ref, kg_ref, zh_ref, u_ref, q_ref, k_ref, v_ref = refs
    x = x_ref[...]
    h = (x * (1.0 + mod_ref[4:5, :]) + mod_ref[3:4, :]).astype(BF16)

    zh_ref[...] = _dot(h, w_ref[:, 0:HG_WIDTH])
    o = HG_WIDTH
    glu_a = _dot(h, w_ref[:, o:o + B_WIDTH])
    glu_b = _dot(h, w_ref[:, o + B_WIDTH:o + 2 * B_WIDTH])
    u_ref[...] = glu_a * _sigmoid(glu_b)
    o += 2 * B_WIDTH
    cq = _dot(h, w_ref[:, o:o + C_WIDTH])
    ck = _dot(h, w_ref[:, o + C_WIDTH:o + C_WIDTH + KV_WIDTH])
    v_ref[...] = _dot(h, w_ref[:, o + C_WIDTH + KV_WIDTH:o + C_WIDTH + 2 * KV_WIDTH])

    qn = _head_rms_norm(cq, qg_ref[...])
    kn = _head_rms_norm(ck, kg_ref[...])
    if rope:
        cos, s1, s2 = cos_ref[...], s1_ref[...], s2_ref[...]

        def rot(t):
            cols = []
            for c in range(t.shape[1] // LANES):
                tc = t[:, c * LANES:(c + 1) * LANES]
                cols.append(tc * cos + pltpu.roll(tc, LANES - ROPE_PAIRS, 1) * s1
                            + pltpu.roll(tc, ROPE_PAIRS, 1) * s2)
            return cols[0] if len(cols) == 1 else jnp.concatenate(cols, axis=1)

        qn, kn = rot(qn), rot(kn)
    q_ref[...] = (qn * (HEAD_DIM ** -0.5)).astype(BF16)
    k_ref[...] = kn


def _proj(x, mod_l, cond_of_block, w_in, qg, kg, layer, rope_tables, seq):
    rows = x.shape[0]
    rope = rope_tables is not None
    in_specs = [
        pl.BlockSpec((ROW_TILE, D_MODEL), lambda i: (i, 0)),
        _mod_spec(cond_of_block),
        _resident((None, D_MODEL, IN_WIDTH), lambda i: (layer, 0, 0)),
        _resident((None, 1, C_WIDTH), lambda i: (layer, 0, 0)),
        _resident((None, 1, KV_WIDTH), lambda i: (layer, 0, 0)),
    ]
    args = [x, mod_l, w_in, qg, kg]
    if rope:
        per_seq = seq // ROW_TILE
        in_specs += [pl.BlockSpec((ROW_TILE, LANES), lambda i: (i % per_seq, 0))] * 3
        args += list(rope_tables)
    return pl.pallas_call(
        functools.partial(_proj_kernel, rope=rope),
        grid=(rows // ROW_TILE,),
        in_specs=in_specs,
        out_specs=[
            pl.BlockSpec((ROW_TILE, HG_WIDTH), lambda i: (i, 0)),
            pl.BlockSpec((ROW_TILE, B_WIDTH), lambda i: (i, 0)),
            pl.BlockSpec((ROW_TILE, C_WIDTH), lambda i: (i, 0)),
            pl.BlockSpec((ROW_TILE, KV_WIDTH), lambda i: (i, 0)),
            pl.BlockSpec((ROW_TILE, KV_WIDTH), lambda i: (i, 0)),
        ],
        out_shape=[
            jax.ShapeDtypeStruct((rows, HG_WIDTH), F32),
            jax.ShapeDtypeStruct((rows, B_WIDTH), F32),
            jax.ShapeDtypeStruct((rows, C_WIDTH), BF16),
            jax.ShapeDtypeStruct((rows, KV_WIDTH), F32),
            jax.ShapeDtypeStruct((rows, KV_WIDTH), F32),
        ],
        compiler_params=_params(("parallel",), 48),
        name="mixer_in_proj",
    )(*args)


def _cumsum_rows(x, reverse):
    n = x.shape[0]
    row = lax.broadcasted_iota(jnp.int32, x.shape, 0)
    s = 1
    while s < n:
        if reverse:
            x = x + jnp.where(row < n - s, pltpu.roll(x, n - s, 0), 0.0)
        else:
            x = x + jnp.where(row >= s, pltpu.roll(x, s, 0), 0.0)
        s *= 2
    return x


def _hgrn_kernel(zh_ref, s0_ref, lbl_ref, ng_ref, o_ref, sfin_ref, acc_ref, *, layer, seq):
    C, W = HGRN_CHUNK, A_WIDTH
    n_chunks = seq // C

    lg = lbl_ref[...]
    e = jnp.exp(lg - jnp.max(lg, axis=0, keepdims=True))
    soft = e / jnp.sum(e, axis=0, keepdims=True)
    lb = jnp.zeros((1, W), F32)
    for j in range(1, layer + 1):
        lb = lb + soft[j:j + 1, :]

    lane_head = lax.broadcasted_iota(jnp.int32, (1, W), 1) // A_DK
    head_masks = [lane_head == h for h in range(A_HEADS)]
    t_idx = lax.broadcasted_iota(jnp.int32, (C, A_HEADS * C), 0)
    s_idx = lax.broadcasted_iota(jnp.int32, (C, A_HEADS * C), 1) % C
    bd_mask = (lax.broadcasted_iota(jnp.int32, (W, W), 0) // A_DK
               == lax.broadcasted_iota(jnp.int32, (W, W), 1) // A_DK)

    def stack_heads(t_bf):
        return jnp.concatenate([jnp.where(m, t_bf, jnp.zeros_like(t_bf)) for m in head_masks], axis=0)

    def chunk(n, st, reverse):
        r0 = pl.multiple_of(n * C, C)
        rows = pl.ds(r0, C)
        q = zh_ref[rows, 0:W]
        v = zh_ref[rows, W:2 * W]
        zf = zh_ref[rows, (3 if reverse else 2) * W:(4 if reverse else 3) * W]
        f = lb + (1.0 - lb) * _sigmoid(zf)
        kk = 1.0 - f
        a = _cumsum_rows(jnp.log(jnp.maximum(f, F_MIN)), reverse)

        v_bf = v.astype(BF16)
        scores = jnp.zeros((C, A_HEADS * C), F32)
        h = C // 2
        while h >= HGRN_BASE:
            a3 = a.reshape(C // h, h, W)
            zero = jnp.zeros((1, 1, W), F32)
            if reverse:
                own = a3[:, 0:1, :]
                other = jnp.concatenate([own[1:], zero], axis=0)
            else:
                own = a3[:, h - 1:h, :]
                other = jnp.concatenate([zero, own[:-1]], axis=0)
            qt = (q.reshape(C // h, h, W) * jnp.exp(a3 - other)).reshape(C, W).astype(BF16)
            kt = (kk.reshape(C // h, h, W) * jnp.exp(own - a3)).reshape(C, W).astype(BF16)
            s_l = _dot_nt(qt, stack_heads(kt))
            tb, sb = t_idx // h, s_idx // h
            if reverse:
                m = (sb == tb + 1) & (tb % 2 == 0)
            else:
                m = (tb == sb + 1) & (tb % 2 == 1)
            scores = scores + jnp.where(m, s_l, 0.0)
            h //= 2
        hb = HGRN_BASE
        a3 = a.reshape(C // hb, hb, W)
        ref_row = a3[:, hb // 2:hb // 2 + 1, :] if reverse else a3[:, hb // 2 - 1:hb // 2, :]
        qt = (q.reshape(C // hb, hb, W) * jnp.exp(a3 - ref_row)).reshape(C, W).astype(BF16)
        kt = (kk.reshape(C // hb, hb, W) * jnp.exp(ref_row - a3)).reshape(C, W).astype(BF16)
        s_l = _dot_nt(qt, stack_heads(kt))
        same = (t_idx // hb) == (s_idx // hb)
        m = same & ((s_idx >= t_idx) if reverse else (s_idx <= t_idx))
        scores = scores + jnp.where(m, s_l, 0.0)

        o = _dot(scores.astype(BF16), stack_heads(v_bf))
        o = o + _dot_nt((q * jnp.exp(a)).astype(BF16), st.astype(BF16))
        a_last = a[0:1, :] if reverse else a[C - 1:C, :]
        kh = (kk * jnp.exp(a_last - a)).astype(BF16)
        st = st * jnp.exp(a_last) + jnp.where(bd_mask, _dot_tn(v_bf, kh), 0.0)
        return rows, o, st

    def fwd_body(n, st):
        rows, o, st = chunk(n, st, False)
        acc_ref[rows, :] = o
        return st

    ones_bd = _segment_ones(LANES, A_DK)

    def bwd_body(i, st):
        n = n_chunks - 1 - i
        rows, o, st = chunk(n, st, True)
        tot = acc_ref[rows, :] + o
        zg = zh_ref[rows, 4 * W:5 * W]
        cols = []
        for c in range(W // LANES):
            lanes = slice(c * LANES, (c + 1) * LANES)
            tc = tot[:, lanes]
            ms = _segment_sum(tc * tc, ones_bd) * (1.0 / A_DK)
            cols.append(tc * lax.rsqrt(ms + 1e-6) * ng_ref[:, lanes])
        on = jnp.concatenate(cols, axis=1)
        o_ref[rows, :] = (on * (zg * _sigmoid(zg))).astype(o_ref.dtype)
        return st

    sfin_ref[0] = lax.fori_loop(0, n_chunks, fwd_body, s0_ref[0])
    sfin_ref[1] = lax.fori_loop(0, n_chunks, bwd_body, s0_ref[1])


def _hgrn(zh, s0, lb_logits, norm_g, layer, seq):
    rows = zh.shape[0]
    n_seq = rows // seq
    per_seq_state = s0.shape[0] != 1
    W = A_WIDTH
    return pl.pallas_call(
        functools.partial(_hgrn_kernel, layer=layer, seq=seq),
        grid=(n_seq,),
        in_specs=[
            pl.BlockSpec((seq, HG_WIDTH), lambda b: (b, 0)),
            pl.BlockSpec((None, 2, W, W), (lambda b: (b, 0, 0, 0)) if per_seq_state else (lambda b: (0, 0, 0, 0))),
            _resident((DEPTH, W), lambda b: (0, 0)),
            _resident((None, 1, W), lambda b: (layer, 0, 0)),
        ],
        out_specs=[
            pl.BlockSpec((seq, W), lambda b: (b, 0)),
            pl.BlockSpec((None, 2, W, W), lambda b: (b, 0, 0, 0)),
        ],
        out_shape=[
            jax.ShapeDtypeStruct((rows, W), BF16),
            jax.ShapeDtypeStruct((n_seq, 2, W, W), F32),
        ],
        scratch_shapes=[pltpu.VMEM((seq, W), F32)],
        compiler_params=_params(("parallel",), 48),
        name="hgrn2",
    )(zh, s0, lb_logits, norm_g)


def _conv_kernel(up_ref, uc_ref, un_ref, w_ref, b_ref, g_ref, beta_ref, o_ref, pad_ref, *, tiles_per_seq):
    i = pl.program_id(0)
    T, H = CONV_TILE, CONV_HALO
    first = (i % tiles_per_seq) == 0
    last = (i % tiles_per_seq) == tiles_per_seq - 1
    pad_ref[0:H, :] = jnp.where(first, 0.0, up_ref[...])
    pad_ref[H:H + T, :] = uc_ref[...]
    pad_ref[H + T:H + T + H, :] = jnp.where(last, 0.0, un_ref[...])
    acc = jnp.zeros((T, B_WIDTH), F32)
    for j in range(CONV_K):
        start = H - CONV_PAD + j
        acc = acc + w_ref[j:j + 1, :] * pad_ref[start:start + T, :]
    y = _layer_norm(acc + b_ref[...], g_ref[...], beta_ref[...])
    o_ref[...] = (y * _sigmoid(y)).astype(o_ref.dtype)


def _conv(u, conv_w, conv_b, ln_g, ln_b, layer, seq):
    rows = u.shape[0]
    T, H = CONV_TILE, CONV_HALO
    tiles_per_seq = seq // T
    ratio = T // H
    n_halo_blocks = rows // H
    return pl.pallas_call(
        functools.partial(_conv_kernel, tiles_per_seq=tiles_per_seq),
        grid=(rows // T,),
        in_specs=[
            pl.BlockSpec((H, B_WIDTH), lambda i: (jnp.maximum(i * ratio - 1, 0), 0)),
            pl.BlockSpec((T, B_WIDTH), lambda i: (i, 0)),
            pl.BlockSpec((H, B_WIDTH), lambda i: (jnp.minimum((i + 1) * ratio, n_halo_blocks - 1), 0)),
            _resident((None, CONV_K, B_WIDTH), lambda i: (layer, 0, 0)),
            _resident((None, 1, B_WIDTH), lambda i: (layer, 0, 0)),
            _resident((None, 1, B_WIDTH), lambda i: (layer, 0, 0)),
            _resident((None, 1, B_WIDTH), lambda i: (layer, 0, 0)),
        ],
        out_specs=pl.BlockSpec((T, B_WIDTH), lambda i: (i, 0)),
        out_shape=jax.ShapeDtypeStruct((rows, B_WIDTH), BF16),
        scratch_shapes=[pltpu.VMEM((T + 2 * H, B_WIDTH), F32)],
        compiler_params=_params(("parallel",), 16),
        name="conv_module",
    )(u, u, u, conv_w, conv_b, ln_g, ln_b)


def _attn_kernel(q_ref, k_ref, v_ref, o_ref):
    tq = q_ref.shape[0]
    group = C_HEADS // C_KV_HEADS
    for g in range(C_KV_HEADS):
        kv_lanes = slice(g * HEAD_DIM, (g + 1) * HEAD_DIM)
        kg = k_ref[:, kv_lanes].astype(BF16)
        vg = v_ref[:, kv_lanes].astype(BF16)
        qs = jnp.concatenate(
            [q_ref[:, (g * group + j) * HEAD_DIM:(g * group + j + 1) * HEAD_DIM] for j in range(group)], axis=0)
        s = _dot_nt(qs, kg)
        p = jnp.exp(s - jnp.max(s, axis=-1, keepdims=True))
        l = jnp.sum(p, axis=-1, keepdims=True)
        o = _dot(p.astype(BF16), vg) / l
        for j in range(group):
            hq = g * group + j
            o_ref[:, hq * HEAD_DIM:(hq + 1) * HEAD_DIM] = o[j * tq:(j + 1) * tq, :].astype(o_ref.dtype)


def _attention(q, k, v, seq_q, seq_k, tq):
    n_seq = q.shape[0] // seq_q
    q_tiles = seq_q // tq
    return pl.pallas_call(
        _attn_kernel,
        grid=(n_seq, q_tiles),
        in_specs=[
            pl.BlockSpec((tq, C_WIDTH), lambda b, i: (b * q_tiles + i, 0)),
            pl.BlockSpec((seq_k, KV_WIDTH), lambda b, i: (b, 0)),
            pl.BlockSpec((seq_k, KV_WIDTH), lambda b, i: (b, 0)),
        ],
        out_specs=pl.BlockSpec((tq, C_WIDTH), lambda b, i: (b * q_tiles + i, 0)),
        out_shape=jax.ShapeDtypeStruct(q.shape, BF16),
        compiler_params=_params(("parallel", "arbitrary"), 48),
        name="attention",
    )(q, k, v)


def _out_kernel(x_ref, oa_ref, ob_ref, oc_ref, mod_ref, w_ref, g_ref, b_ref, o_ref):
    x = x_ref[...]
    y = _dot(oa_ref[...], w_ref[0:A_WIDTH, :])
    y = y + _dot(ob_ref[...], w_ref[A_WIDTH:A_WIDTH + B_WIDTH, :])
    y = y + _dot(oc_ref[...], w_ref[A_WIDTH + B_WIDTH:, :])
    o_ref[...] = _layer_norm(ALPHA * x + mod_ref[5:6, :] * y, g_ref[...], b_ref[...])


def _out_proj(x, o_a, o_b, o_c, mod_l, cond_of_block, w_out, ln_g, ln_b, layer):
    rows = x.shape[0]
    return pl.pallas_call(
        _out_kernel,
        grid=(rows // ROW_TILE,),
        in_specs=[
            pl.BlockSpec((ROW_TILE, D_MODEL), lambda i: (i, 0)),
            pl.BlockSpec((ROW_TILE, A_WIDTH), lambda i: (i, 0)),
            pl.BlockSpec((ROW_TILE, B_WIDTH), lambda i: (i, 0)),
            pl.BlockSpec((ROW_TILE, C_WIDTH), lambda i: (i, 0)),
            _mod_spec(cond_of_block),
            _resident((None, D_MODEL, D_MODEL), lambda i: (layer, 0, 0)),
            _resident((None, 1, D_MODEL), lambda i: (layer * 3 + 1, 0, 0)),
            _resident((None, 1, D_MODEL), lambda i: (layer * 3 + 1, 0, 0)),
        ],
        out_specs=pl.BlockSpec((ROW_TILE, D_MODEL), lambda i: (i, 0)),
        out_shape=jax.ShapeDtypeStruct((rows, D_MODEL), F32),
        compiler_params=_params(("parallel",), 32),
        name="mixer_out_proj",
    )(x, o_a, o_b, o_c, mod_l, w_out, ln_g, ln_b)


def _rope_tables(seq):
    pos = jnp.arange(seq)
    row_id = (pos // GRID_W).astype(F32)
    col_id = (pos % GRID_W).astype(F32)
    inv = ROPE_THETA ** (-jnp.arange(ROPE_PAIRS, dtype=F32) / ROPE_PAIRS)
    lane = jnp.arange(LANES) % HEAD_DIM
    use_col = (lane // (2 * ROPE_PAIRS)) == 1
    first = (lane % (2 * ROPE_PAIRS)) < ROPE_PAIRS
    freq = inv[lane % ROPE_PAIRS]
    ang = jnp.where(use_col[None, :], col_id[:, None], row_id[:, None]) * freq[None, :]
    cos, sin = jnp.cos(ang), jnp.sin(ang)
    return cos, jnp.where(first[None, :], -sin, 0.0), jnp.where(first[None, :], 0.0, sin)


def _state_to_block_diag(s):
    eye = jnp.eye(A_HEADS, dtype=s.dtype)
    st = jnp.swapaxes(s, -1, -2)
    bd = st[..., :, :, None, :] * eye[:, None, :, None]
    return bd.reshape(s.shape[:-3] + (A_WIDTH, A_WIDTH))


def _block_diag_to_state(bd):
    b5 = bd.reshape(bd.shape[:-2] + (A_HEADS, A_DK, A_HEADS, A_DK))
    blocks = jnp.stack([b5[..., h, :, h, :] for h in range(A_HEADS)], axis=-3)
    return jnp.swapaxes(blocks, -1, -2)


def kernel(x_prompt, x_sample, cache_k, cache_v, state_hgrn, c, c_ctx, w_mod, b_mod, ln_g, ln_b, ffn1_w_in, ffn1_w_out, ffn2_w_in, ffn2_w_out, w_in, w_out, hgrn_lb_logits, hgrn_norm_g, conv_w, conv_b, conv_ln_g, conv_ln_b, q_norm_g, k_norm_g):
    batch, seq, _ = x_prompt.shape
    dec_batch, dec_seq, _ = x_sample.shape
    past = cache_k.shape[2]
    assert seq % ROW_TILE == 0 or ROW_TILE % seq == 0
    assert dec_seq % ROW_TILE == 0 and dec_batch + 1 <= SUBLANES

    bf = lambda w: w.astype(BF16)
    ffn1_w_in, ffn1_w_out, ffn2_w_in, ffn2_w_out, w_in, w_out = map(
        bf, (ffn1_w_in, ffn1_w_out, ffn2_w_in, ffn2_w_out, w_in, w_out))

    conds = jnp.zeros((SUBLANES, D_MODEL), F32).at[0].set(c_ctx).at[1:1 + dec_batch].set(c)
    mod = _modulation(conds, w_mod, b_mod).reshape(DEPTH, SUBLANES, N_MOD, D_MODEL)

    ln_g3 = ln_g.reshape(DEPTH * 3, 1, D_MODEL)
    ln_b3 = ln_b.reshape(DEPTH * 3, 1, D_MODEL)
    qg = jnp.tile(q_norm_g, (1, C_HEADS)).reshape(DEPTH, 1, C_WIDTH)
    kg = jnp.tile(k_norm_g, (1, C_KV_HEADS)).reshape(DEPTH, 1, KV_WIDTH)
    norm_g = hgrn_norm_g.reshape(DEPTH, 1, A_WIDTH)
    conv_b3 = conv_b.reshape(DEPTH, 1, B_WIDTH)
    conv_g3 = conv_ln_g.reshape(DEPTH, 1, B_WIDTH)
    conv_beta3 = conv_ln_b.reshape(DEPTH, 1, B_WIDTH)
    tables = _rope_tables(dec_seq)
    zero_state = jnp.zeros((1, 2, A_WIDTH, A_WIDTH), F32)
    lat_state = _state_to_block_diag(state_hgrn)

    blocks_per_lat = dec_seq // ROW_TILE
    ctx_cond = lambda i: 0
    lat_cond = lambda i: 1 + i // blocks_per_lat

    xp = x_prompt.reshape(batch * seq, D_MODEL)
    xs = x_sample.reshape(dec_batch * dec_seq, D_MODEL)
    ks, vs, ss = [], [], []
    for l in range(DEPTH):
        mod_l = mod[l]

        def layer_fn(x, cond, sq, rope_tables, s0, k_past, v_past):
            x = _ffn(x, mod_l, cond, ffn1_w_in, ffn1_w_out, ln_g3, ln_b3, l, 0)
            zh, u, q, k, v = _proj(x, mod_l, cond, w_in, qg, kg, l, rope_tables, sq)
            o_a, s_fin = _hgrn(zh, s0, hgrn_lb_logits, norm_g, l, sq)
            o_b = _conv(u, conv_w, conv_b3, conv_g3, conv_beta3, l, sq)
            if k_past is None:
                o_c = _attention(q, k, v, sq, sq, sq)
            else:
                n = x.shape[0] // sq
                k_all = jnp.concatenate([k_past, k.reshape(n, sq, KV_WIDTH)], axis=1)
                v_all = jnp.concatenate([v_past, v.reshape(n, sq, KV_WIDTH)], axis=1)
                sk = k_all.shape[1]
                o_c = _attention(q, k_all.reshape(n * sk, KV_WIDTH), v_all.reshape(n * sk, KV_WIDTH),
                                 sq, sk, ATTN_Q_TILE_LONG)
            x = _out_proj(x, o_a, o_b, o_c, mod_l, cond, w_out, ln_g3, ln_b3, l)
            x = _ffn(x, mod_l, cond, ffn2_w_in, ffn2_w_out, ln_g3, ln_b3, l, 2)
            return x, k, v, s_fin

        xp, k_l, v_l, s_l = layer_fn(xp, ctx_cond, seq, None, zero_state, None, None)
        ks.append(k_l.reshape(batch, seq, C_KV_HEADS, HEAD_DIM))
        vs.append(v_l.reshape(batch, seq, C_KV_HEADS, HEAD_DIM))
        ss.append(_block_diag_to_state(s_l))
        xs, _, _, _ = layer_fn(xs, lat_cond, dec_seq, tables, lat_state[:, l],
                               cache_k[:, l].reshape(dec_batch, past, KV_WIDTH),
                               cache_v[:, l].reshape(dec_batch, past, KV_WIDTH))

    return (xp.reshape(batch, seq, D_MODEL), xs.reshape(dec_batch, dec_seq, D_MODEL),
            jnp.stack(ks, axis=1), jnp.stack(vs, axis=1), jnp.stack(ss, axis=1))
```

```python
import functools

import jax
import jax.numpy as jnp
from jax import lax
from jax.experimental import pallas as pl
from jax.experimental.pallas import tpu as pltpu

F32 = jnp.float32
BF16 = jnp.bfloat16

D_MODEL = 1024
DEPTH = 2
GRID_W = 64
HEAD_DIM = 64
A_HEADS = 4
A_DK = 64
A_WIDTH = 256
B_WIDTH = 256
CONV_K = 31
CONV_PAD = 15
C_HEADS = 8
C_KV_HEADS = 2
C_WIDTH = 512
KV_WIDTH = C_KV_HEADS * HEAD_DIM
D_FF = 2816
ROPE_THETA = 10000.0
ROPE_PAIRS = 16
N_MOD = 9
ALPHA = (2 * DEPTH) ** 0.25
F_MIN = 1e-6
LOG2_E = 1.4426950408889634
IN_WIDTH = 5 * A_WIDTH + 2 * B_WIDTH + C_WIDTH + 2 * KV_WIDTH
HG_WIDTH = 5 * A_WIDTH

LANES = 128
SUBLANES = 8
VMEM_BYTES_V7X = 64 * 1024 * 1024

ROW_TILE = 512
FFN_CHUNK = 256
HGRN_CHUNK = 128
HGRN_BASE = 8
CONV_TILE = 256
CONV_HALO = 16
ATTN_Q_TILE_LONG = 128
ATTN_KEY_BLOCK = 512


def _params(semantics, vmem_mb):
    return pltpu.CompilerParams(dimension_semantics=semantics,
                                vmem_limit_bytes=min(vmem_mb * 1024 * 1024, VMEM_BYTES_V7X - (4 << 20)))


def _resident(block_shape, index_map):
    return pl.BlockSpec(block_shape, index_map, pipeline_mode=pl.Buffered(1))


def _sigmoid(x):
    return 1.0 / (1.0 + jnp.exp(-x))


def _layer_norm(y, g, b, eps=1e-5):
    mu = jnp.mean(y, axis=-1, keepdims=True)
    d = y - mu
    var = jnp.mean(d * d, axis=-1, keepdims=True)
    return d * lax.rsqrt(var + eps) * g + b


def _dot(a, b):
    return jnp.dot(a, b, preferred_element_type=F32)


def _dot_nt(a, b):
    return lax.dot_general(a, b, (((1,), (1,)), ((), ())), preferred_element_type=F32)


def _dot_tn(a, b):
    return lax.dot_general(a, b, (((0,), (0,)), ((), ())), preferred_element_type=F32)


def _segment_ones(width, seg):
    r = lax.broadcasted_iota(jnp.int32, (width, width), 0) // seg
    c = lax.broadcasted_iota(jnp.int32, (width, width), 1) // seg
    return (r == c).astype(BF16)


def _segment_sum(x, ones_bd):
    hi = x.astype(BF16)
    lo = (x - hi.astype(F32)).astype(BF16)
    return _dot(hi, ones_bd) + _dot(lo, ones_bd)


def _head_rms_norm(x, gain, eps=1e-6):
    ones_bd = _segment_ones(LANES, HEAD_DIM)
    cols = []
    for c in range(x.shape[1] // LANES):
        xc = x[:, c * LANES:(c + 1) * LANES]
        ms = _segment_sum(xc * xc, ones_bd) * (1.0 / HEAD_DIM)
        cols.append(xc * lax.rsqrt(ms + eps) * gain[:, c * LANES:(c + 1) * LANES])
    return cols[0] if len(cols) == 1 else jnp.concatenate(cols, axis=1)


def _mod_kernel(c_ref, w_ref, b_ref, o_ref):
    c = c_ref[...]
    s = (c * _sigmoid(c)).astype(BF16)
    o_ref[...] = _dot(s, w_ref[...].astype(BF16)) + b_ref[...]


def _modulation(conds, w_mod, b_mod):
    tn = D_MODEL
    return pl.pallas_call(
        _mod_kernel,
        grid=(DEPTH, N_MOD * D_MODEL // tn),
        in_specs=[
            pl.BlockSpec((SUBLANES, D_MODEL), lambda l, j: (0, 0)),
            pl.BlockSpec((None, D_MODEL, tn), lambda l, j: (l, 0, j)),
            pl.BlockSpec((None, 1, tn), lambda l, j: (l, 0, j)),
        ],
        out_specs=pl.BlockSpec((None, SUBLANES, tn), lambda l, j: (l, 0, j)),
        out_shape=jax.ShapeDtypeStruct((DEPTH, SUBLANES, N_MOD * D_MODEL), F32),
        compiler_params=_params(("arbitrary", "arbitrary"), 32),
        name="modulation",
    )(conds, w_mod, b_mod.reshape(DEPTH, 1, N_MOD * D_MODEL))


def _mod_spec(cond_of_block):
    return pl.BlockSpec((None, N_MOD, D_MODEL), lambda i: (cond_of_block(i), 0, 0))


def _ffn_kernel(*refs, sub, with_mixer_out):
    if with_mixer_out:
        (x_ref, oa_ref, ob_ref, oc_ref, mod_ref, wmix_ref, gmix_ref, bmix_ref,
         wg_ref, wu_ref, wo_ref, g_ref, b_ref, o_ref) = refs
        y = _dot(oa_ref[...], wmix_ref[0:A_WIDTH, :])
        y = y + _dot(ob_ref[...], wmix_ref[A_WIDTH:A_WIDTH + B_WIDTH, :])
        y = y + _dot(oc_ref[...], wmix_ref[A_WIDTH + B_WIDTH:, :])
        x = _layer_norm(ALPHA * x_ref[...] + mod_ref[5:6, :] * y, gmix_ref[...], bmix_ref[...])
    else:
        x_ref, mod_ref, wg_ref, wu_ref, wo_ref, g_ref, b_ref, o_ref = refs
        x = x_ref[...]
    shift = mod_ref[3 * sub:3 * sub + 1, :]
    scale = mod_ref[3 * sub + 1:3 * sub + 2, :]
    gate = mod_ref[3 * sub + 2:3 * sub + 3, :]
    h = (x * (1.0 + scale) + shift).astype(BF16)
    acc = jnp.zeros(x.shape, F32)
    for j in range(D_FF // FFN_CHUNK):
        cols = slice(j * FFN_CHUNK, (j + 1) * FFN_CHUNK)
        gt = _dot(h, wg_ref[:, cols])
        up = _dot(h, wu_ref[:, cols])
        act = (gt * _sigmoid(gt) * up).astype(BF16)
        acc = acc + _dot(act, wo_ref[cols, :])
    y = ALPHA * x + 0.5 * gate * acc
    o_ref[...] = _layer_norm(y, g_ref[...], b_ref[...])


def _ffn(x, mod_l, cond_of_block, w_in, w_out, ln_g, ln_b, layer, sub, mixer=None):
    rows = x.shape[0]
    row_spec = lambda width: pl.BlockSpec((ROW_TILE, width), lambda i: (i, 0))
    ln_spec = lambda idx: _resident((None, 1, D_MODEL), lambda i: (layer * 3 + idx, 0, 0))
    in_specs, args = [row_spec(D_MODEL)], [x]
    if mixer is not None:
        o_a, o_b, o_c, w_mix = mixer
        in_specs += [row_spec(A_WIDTH), row_spec(B_WIDTH), row_spec(C_WIDTH)]
        args += [o_a, o_b, o_c]
    in_specs.append(_mod_spec(cond_of_block))
    args.append(mod_l)
    if mixer is not None:
        in_specs += [_resident((None, D_MODEL, D_MODEL), lambda i: (layer, 0, 0)), ln_spec(1), ln_spec(1)]
        args += [w_mix, ln_g, ln_b]
    in_specs += [
        _resident((None, D_MODEL, D_FF), lambda i: (layer, 0, 0)),
        _resident((None, D_MODEL, D_FF), lambda i: (layer, 0, 1)),
        _resident((None, D_FF, D_MODEL), lambda i: (layer, 0, 0)),
        ln_spec(sub), ln_spec(sub),
    ]
    args += [w_in, w_in, w_out, ln_g, ln_b]
    return pl.pallas_call(
        functools.partial(_ffn_kernel, sub=sub, with_mixer_out=mixer is not None),
        grid=(rows // ROW_TILE,),
        in_specs=in_specs,
        out_specs=row_spec(D_MODEL),
        out_shape=jax.ShapeDtypeStruct((rows, D_MODEL), F32),
        compiler_params=_params(("parallel",), 52),
        name=f"ffn{sub // 2 + 1}",
    )(*args)


def _proj_kernel(*refs, rope):
    if rope:
        (x_ref, mod_ref, w_ref, qg_ref, kg_ref, cos_ref, s1_ref, s2_ref,
         zh_ref, u_ref, q_ref, k_ref, v_ref) = refs
    else:
        x_ref, mod_ref, w_ref, qg_ref, kg_ref, zh_ref, u_ref, q_ref, k_ref, v_ref = refs
    x = x_ref[...]
    h = (x * (1.0 + mod_ref[4:5, :]) + mod_ref[3:4, :]).astype(BF16)

    zh_ref[...] = _dot(h, w_ref[:, 0:HG_WIDTH])
    o = HG_WIDTH
    glu_a = _dot(h, w_ref[:, o:o + B_WIDTH])
    glu_b = _dot(h, w_ref[:, o + B_WIDTH:o + 2 * B_WIDTH])
    u_ref[...] = glu_a * _sigmoid(glu_b)
    o += 2 * B_WIDTH
    cq = _dot(h, w_ref[:, o:o + C_WIDTH])
    ck = _dot(h, w_ref[:, o + C_WIDTH:o + C_WIDTH + KV_WIDTH])
    v_ref[...] = _dot(h, w_ref[:, o + C_WIDTH + KV_WIDTH:o + C_WIDTH + 2 * KV_WIDTH])

    qn = _head_rms_norm(cq, qg_ref[...])
    kn = _head_rms_norm(ck, kg_ref[...])
    if rope:
        cos, s1, s2 = cos_ref[...], s1_ref[...], s2_ref[...]

        def rot(t):
            cols = []
            for c in range(t.shape[1] // LANES):
                tc = t[:, c * LANES:(c + 1) * LANES]
                cols.append(tc * cos + pltpu.roll(tc, LANES - ROPE_PAIRS, 1) * s1
                            + pltpu.roll(tc, ROPE_PAIRS, 1) * s2)
            return cols[0] if len(cols) == 1 else jnp.concatenate(cols, axis=1)

        qn, kn = rot(qn), rot(kn)
    q_ref[...] = (qn * (HEAD_DIM ** -0.5)).astype(BF16)
    k_ref[...] = kn


def _proj(x, mod_l, cond_of_block, w_in, qg, kg, layer, rope_tables, seq):
    rows = x.shape[0]
    rope = rope_tables is not None
    in_specs = [
        pl.BlockSpec((ROW_TILE, D_MODEL), lambda i: (i, 0)),
        _mod_spec(cond_of_block),
        _resident((None, D_MODEL, IN_WIDTH), lambda i: (layer, 0, 0)),
        _resident((None, 1, C_WIDTH), lambda i: (layer, 0, 0)),
        _resident((None, 1, KV_WIDTH), lambda i: (layer, 0, 0)),
    ]
    args = [x, mod_l, w_in, qg, kg]
    if rope:
        per_seq = seq // ROW_TILE
        in_specs += [pl.BlockSpec((ROW_TILE, LANES), lambda i: (i % per_seq, 0))] * 3
        args += list(rope_tables)
    return pl.pallas_call(
        functools.partial(_proj_kernel, rope=rope),
        grid=(rows // ROW_TILE,),
        in_specs=in_specs,
        out_specs=[
            pl.BlockSpec((ROW_TILE, HG_WIDTH), lambda i: (i, 0)),
            pl.BlockSpec((ROW_TILE, B_WIDTH), lambda i: (i, 0)),
            pl.BlockSpec((ROW_TILE, C_WIDTH), lambda i: (i, 0)),
            pl.BlockSpec((ROW_TILE, KV_WIDTH), lambda i: (i, 0)),
            pl.BlockSpec((ROW_TILE, KV_WIDTH), lambda i: (i, 0)),
        ],
        out_shape=[
            jax.ShapeDtypeStruct((rows, HG_WIDTH), F32),
            jax.ShapeDtypeStruct((rows, B_WIDTH), F32),
            jax.ShapeDtypeStruct((rows, C_WIDTH), BF16),
            jax.ShapeDtypeStruct((rows, KV_WIDTH), F32),
            jax.ShapeDtypeStruct((rows, KV_WIDTH), F32),
        ],
        compiler_params=_params(("parallel",), 48),
        name="mixer_in_proj",
    )(*args)


def _cumsum_rows(x, reverse):
    n = x.shape[0]
    row = lax.broadcasted_iota(jnp.int32, x.shape, 0)
    s = 1
    while s < n:
        if reverse:
            x = x + jnp.where(row < n - s, pltpu.roll(x, n - s, 0), 0.0)
        else:
            x = x + jnp.where(row >= s, pltpu.roll(x, s, 0), 0.0)
        s *= 2
    return x


def _hgrn_kernel(zh_ref, s0_ref, lbl_ref, ng_ref, o_ref, sfin_ref, fwd_ref, bwd_ref, *, layer, seq):
    C, W = HGRN_CHUNK, A_WIDTH
    n_chunks = seq // C

    lg = lbl_ref[...]
    e = jnp.exp(lg - jnp.max(lg, axis=0, keepdims=True))
    soft = e / jnp.sum(e, axis=0, keepdims=True)
    lb = jnp.zeros((1, W), F32)
    for j in range(1, layer + 1):
        lb = lb + soft[j:j + 1, :]

    lane_head = lax.broadcasted_iota(jnp.int32, (1, W), 1) // A_DK
    head_masks = [lane_head == h for h in range(A_HEADS)]
    t_idx = lax.broadcasted_iota(jnp.int32, (C, A_HEADS * C), 0)
    s_idx = lax.broadcasted_iota(jnp.int32, (C, A_HEADS * C), 1) % C
    pair_fwd = jnp.where(s_idx <= t_idx, t_idx ^ s_idx, 2 * C)
    pair_bwd = jnp.where(s_idx >= t_idx, t_idx ^ s_idx, 2 * C)
    bd_mask = (lax.broadcasted_iota(jnp.int32, (W, W), 0) // A_DK
               == lax.broadcasted_iota(jnp.int32, (W, W), 1) // A_DK)

    def stack_heads(t_bf):
        return jnp.concatenate([jnp.where(m, t_bf, jnp.zeros_like(t_bf)) for m in head_masks], axis=0)

    def chunk(n, st, reverse):
        r0 = pl.multiple_of(n * C, C)
        rows = pl.ds(r0, C)
        q = zh_ref[rows, 0:W]
        v = zh_ref[rows, W:2 * W]
        zf = zh_ref[rows, (3 if reverse else 2) * W:(4 if reverse else 3) * W]
        f = lb + (1.0 - lb) * _sigmoid(zf)
        kk = 1.0 - f
        a = _cumsum_rows(jnp.log(jnp.maximum(f, F_MIN)), reverse) * LOG2_E
        pair = pair_bwd if reverse else pair_fwd

        v_bf = v.astype(BF16)
        scores = jnp.zeros((C, A_HEADS * C), F32)
        h = C // 2
        while h >= HGRN_BASE:
            a3 = a.reshape(C // h, h, W)
            zero = jnp.zeros((1, 1, W), F32)
            if reverse:
                own = a3[:, 0:1, :]
                other = jnp.concatenate([own[1:], zero], axis=0)
            else:
                own = a3[:, h - 1:h, :]
                other = jnp.concatenate([zero, own[:-1]], axis=0)
            qt = (q.reshape(C // h, h, W) * jnp.exp2(a3 - other)).reshape(C, W).astype(BF16)
            kt = (kk.reshape(C // h, h, W) * jnp.exp2(own - a3)).reshape(C, W).astype(BF16)
            scores = jnp.where(pair < 2 * h, _dot_nt(qt, stack_heads(kt)), scores)
            h //= 2
        hb = HGRN_BASE
        a3 = a.reshape(C // hb, hb, W)
        ref_row = a3[:, hb // 2:hb // 2 + 1, :] if reverse else a3[:, hb // 2 - 1:hb // 2, :]
        qt = (q.reshape(C // hb, hb, W) * jnp.exp2(a3 - ref_row)).reshape(C, W).astype(BF16)
        kt = (kk.reshape(C // hb, hb, W) * jnp.exp2(ref_row - a3)).reshape(C, W).astype(BF16)
        scores = jnp.where(pair < hb, _dot_nt(qt, stack_heads(kt)), scores)

        o = _dot(scores.astype(BF16), stack_heads(v_bf))
        o = o + _dot_nt((q * jnp.exp2(a)).astype(BF16), st.astype(BF16))
        a_last = a[0:1, :] if reverse else a[C - 1:C, :]
        kh = (kk * jnp.exp2(a_last - a)).astype(BF16)
        st = st * jnp.exp2(a_last) + jnp.where(bd_mask, _dot_tn(v_bf, kh), 0.0)
        return rows, o, st

    def scan_body(i, carry):
        st_f, st_b = carry
        rows_f, o_f, st_f = chunk(i, st_f, False)
        fwd_ref[rows_f, :] = o_f
        rows_b, o_b, st_b = chunk(n_chunks - 1 - i, st_b, True)
        bwd_ref[rows_b, :] = o_b
        return st_f, st_b

    finals = lax.fori_loop(0, n_chunks, scan_body, (s0_ref[0], s0_ref[1]), unroll=2)

    ones_bd = _segment_ones(LANES, A_DK)

    def norm_body(n, carry):
        rows = pl.ds(pl.multiple_of(n * C, C), C)
        tot = fwd_ref[rows, :] + bwd_ref[rows, :]
        zg = zh_ref[rows, 4 * W:5 * W]
        cols = []
        for c in range(W // LANES):
            lanes = slice(c * LANES, (c + 1) * LANES)
            tc = tot[:, lanes]
            ms = _segment_sum(tc * tc, ones_bd) * (1.0 / A_DK)
            cols.append(tc * lax.rsqrt(ms + 1e-6) * ng_ref[:, lanes])
        on = jnp.concatenate(cols, axis=1)
        o_ref[rows, :] = (on * (zg * _sigmoid(zg))).astype(o_ref.dtype)
        return carry

    lax.fori_loop(0, n_chunks, norm_body, 0)

    for d, st in enumerate(finals):
        s_kv = st.T
        for h in range(A_HEADS):
            blk = slice(h * A_DK, (h + 1) * A_DK)
            sfin_ref[d, h] = s_kv[blk, blk]


def _hgrn(zh, s0, lb_logits, norm_g, layer, seq):
    rows = zh.shape[0]
    n_seq = rows // seq
    per_seq_state = s0.shape[0] != 1
    W = A_WIDTH
    return pl.pallas_call(
        functools.partial(_hgrn_kernel, layer=layer, seq=seq),
        grid=(n_seq,),
        in_specs=[
            pl.BlockSpec((seq, HG_WIDTH), lambda b: (b, 0)),
            pl.BlockSpec((None, 2, W, W), (lambda b: (b, 0, 0, 0)) if per_seq_state else (lambda b: (0, 0, 0, 0))),
            _resident((DEPTH, W), lambda b: (0, 0)),
            _resident((None, 1, W), lambda b: (layer, 0, 0)),
        ],
        out_specs=[
            pl.BlockSpec((seq, W), lambda b: (b, 0)),
            pl.BlockSpec((None, 2, A_HEADS, A_DK, A_DK), lambda b: (b, 0, 0, 0, 0)),
        ],
        out_shape=[
            jax.ShapeDtypeStruct((rows, W), BF16),
            jax.ShapeDtypeStruct((n_seq, 2, A_HEADS, A_DK, A_DK), F32),
        ],
        scratch_shapes=[pltpu.VMEM((seq, W), F32), pltpu.VMEM((seq, W), F32)],
        compiler_params=_params(("parallel",), 48),
        name="hgrn2",
    )(zh, s0, lb_logits, norm_g)


def _conv_kernel(up_ref, uc_ref, un_ref, w_ref, b_ref, g_ref, beta_ref, o_ref, sh_ref, *, tiles_per_seq):
    i = pl.program_id(0)
    T, H, S = CONV_TILE, CONV_HALO, SUBLANES
    P = T + 2 * H
    first = (i % tiles_per_seq) == 0
    last = (i % tiles_per_seq) == tiles_per_seq - 1
    sh_ref[0, 0:H, :] = jnp.where(first, 0.0, up_ref[...])
    sh_ref[0, H:H + T, :] = uc_ref[...]
    sh_ref[0, H + T:P, :] = jnp.where(last, 0.0, un_ref[...])
    for b in range(1, S):
        sh_ref[b, 0:P - S, :] = sh_ref[0, b:b + P - S, :]
    acc = jnp.zeros((T, B_WIDTH), F32)
    for j in range(CONV_K):
        start = H - CONV_PAD + j
        a, b = start // S, start % S
        acc = acc + w_ref[j:j + 1, :] * sh_ref[b, a * S:a * S + T, :]
    y = _layer_norm(acc + b_ref[...], g_ref[...], beta_ref[...])
    o_ref[...] = (y * _sigmoid(y)).astype(o_ref.dtype)


def _conv(u, conv_w, conv_b, ln_g, ln_b, layer, seq):
    rows = u.shape[0]
    T, H = CONV_TILE, CONV_HALO
    tiles_per_seq = seq // T
    ratio = T // H
    n_halo_blocks = rows // H
    return pl.pallas_call(
        functools.partial(_conv_kernel, tiles_per_seq=tiles_per_seq),
        grid=(rows // T,),
        in_specs=[
            pl.BlockSpec((H, B_WIDTH), lambda i: (jnp.maximum(i * ratio - 1, 0), 0)),
            pl.BlockSpec((T, B_WIDTH), lambda i: (i, 0)),
            pl.BlockSpec((H, B_WIDTH), lambda i: (jnp.minimum((i + 1) * ratio, n_halo_blocks - 1), 0)),
            _resident((None, CONV_K, B_WIDTH), lambda i: (layer, 0, 0)),
            _resident((None, 1, B_WIDTH), lambda i: (layer, 0, 0)),
            _resident((None, 1, B_WIDTH), lambda i: (layer, 0, 0)),
            _resident((None, 1, B_WIDTH), lambda i: (layer, 0, 0)),
        ],
        out_specs=pl.BlockSpec((T, B_WIDTH), lambda i: (i, 0)),
        out_shape=jax.ShapeDtypeStruct((rows, B_WIDTH), BF16),
        scratch_shapes=[pltpu.VMEM((SUBLANES, T + 2 * H, B_WIDTH), F32)],
        compiler_params=_params(("parallel",), 16),
        name="conv_module",
    )(u, u, u, conv_w, conv_b, ln_g, ln_b)


def _attn_kernel(q_ref, k_ref, v_ref, o_ref):
    tq = q_ref.shape[0]
    seq_k = k_ref.shape[0]
    kb = min(seq_k, ATTN_KEY_BLOCK)
    group = C_HEADS // C_KV_HEADS
    for g in range(C_KV_HEADS):
        kv_lanes = slice(g * HEAD_DIM, (g + 1) * HEAD_DIM)
        qs = jnp.concatenate(
            [q_ref[:, (g * group + j) * HEAD_DIM:(g * group + j + 1) * HEAD_DIM] for j in range(group)], axis=0)
        m = l = acc = None
        for b in range(seq_k // kb):
            keys = slice(b * kb, (b + 1) * kb)
            s = _dot_nt(qs, k_ref[keys, kv_lanes].astype(BF16))
            m_blk = jnp.max(s, axis=-1, keepdims=True)
            m_new = m_blk if m is None else jnp.maximum(m, m_blk)
            p = jnp.exp(s - m_new)
            pv = _dot(p.astype(BF16), v_ref[keys, kv_lanes].astype(BF16))
            if m is None:
                l, acc = jnp.sum(p, axis=-1, keepdims=True), pv
            else:
                alpha = jnp.exp(m - m_new)
                l = alpha * l + jnp.sum(p, axis=-1, keepdims=True)
                acc = alpha * acc + pv
            m = m_new
        o = acc / l
        for j in range(group):
            hq = g * group + j
            o_ref[:, hq * HEAD_DIM:(hq + 1) * HEAD_DIM] = o[j * tq:(j + 1) * tq, :].astype(o_ref.dtype)


def _attention(q, k, v, seq_q, seq_k, tq):
    n_seq = q.shape[0] // seq_q
    q_tiles = seq_q // tq
    return pl.pallas_call(
        _attn_kernel,
        grid=(n_seq, q_tiles),
        in_specs=[
            pl.BlockSpec((tq, C_WIDTH), lambda b, i: (b * q_tiles + i, 0)),
            pl.BlockSpec((seq_k, KV_WIDTH), lambda b, i: (b, 0)),
            pl.BlockSpec((seq_k, KV_WIDTH), lambda b, i: (b, 0)),
        ],
        out_specs=pl.BlockSpec((tq, C_WIDTH), lambda b, i: (b * q_tiles + i, 0)),
        out_shape=jax.ShapeDtypeStruct(q.shape, BF16),
        compiler_params=_params(("parallel", "arbitrary"), 48),
        name="attention",
    )(q, k, v)


def _rope_tables(seq):
    pos = jnp.arange(seq)
    row_id = (pos // GRID_W).astype(F32)
    col_id = (pos % GRID_W).astype(F32)
    inv = ROPE_THETA ** (-jnp.arange(ROPE_PAIRS, dtype=F32) / ROPE_PAIRS)
    lane = jnp.arange(LANES) % HEAD_DIM
    use_col = (lane // (2 * ROPE_PAIRS)) == 1
    first = (lane % (2 * ROPE_PAIRS)) < ROPE_PAIRS
    freq = inv[lane % ROPE_PAIRS]
    ang = jnp.where(use_col[None, :], col_id[:, None], row_id[:, None]) * freq[None, :]
    cos, sin = jnp.cos(ang), jnp.sin(ang)
    return cos, jnp.where(first[None, :], -sin, 0.0), jnp.where(first[None, :], 0.0, sin)


def _state_to_block_diag(s):
    eye = jnp.eye(A_HEADS, dtype=s.dtype)
    st = jnp.swapaxes(s, -1, -2)
    bd = st[..., :, :, None, :] * eye[:, None, :, None]
    return bd.reshape(s.shape[:-3] + (A_WIDTH, A_WIDTH))


def kernel(x_prompt, x_sample, cache_k, cache_v, state_hgrn, c, c_ctx, w_mod, b_mod, ln_g, ln_b, ffn1_w_in, ffn1_w_out, ffn2_w_in, ffn2_w_out, w_in, w_out, hgrn_lb_logits, hgrn_norm_g, conv_w, conv_b, conv_ln_g, conv_ln_b, q_norm_g, k_norm_g):
    batch, seq, _ = x_prompt.shape
    dec_batch, dec_seq, _ = x_sample.shape
    past = cache_k.shape[2]
    assert seq % ROW_TILE == 0 or ROW_TILE % seq == 0
    assert dec_seq % ROW_TILE == 0 and dec_batch + 1 <= SUBLANES

    bf = lambda w: w.astype(BF16)
    ffn1_w_in, ffn1_w_out, ffn2_w_in, ffn2_w_out, w_in, w_out = map(
        bf, (ffn1_w_in, ffn1_w_out, ffn2_w_in, ffn2_w_out, w_in, w_out))

    conds = jnp.zeros((SUBLANES, D_MODEL), F32).at[0].set(c_ctx).at[1:1 + dec_batch].set(c)
    mod = _modulation(conds, w_mod, b_mod).reshape(DEPTH, SUBLANES, N_MOD, D_MODEL)

    ln_g3 = ln_g.reshape(DEPTH * 3, 1, D_MODEL)
    ln_b3 = ln_b.reshape(DEPTH * 3, 1, D_MODEL)
    qg = jnp.tile(q_norm_g, (1, C_HEADS)).reshape(DEPTH, 1, C_WIDTH)
    kg = jnp.tile(k_norm_g, (1, C_KV_HEADS)).reshape(DEPTH, 1, KV_WIDTH)
    norm_g = hgrn_norm_g.reshape(DEPTH, 1, A_WIDTH)
    conv_b3 = conv_b.reshape(DEPTH, 1, B_WIDTH)
    conv_g3 = conv_ln_g.reshape(DEPTH, 1, B_WIDTH)
    conv_beta3 = conv_ln_b.reshape(DEPTH, 1, B_WIDTH)
    tables = _rope_tables(dec_seq)
    zero_state = jnp.zeros((1, 2, A_WIDTH, A_WIDTH), F32)
    lat_state = _state_to_block_diag(state_hgrn)

    blocks_per_lat = dec_seq // ROW_TILE
    ctx_cond = lambda i: 0
    lat_cond = lambda i: 1 + i // blocks_per_lat

    xp = x_prompt.reshape(batch * seq, D_MODEL)
    xs = x_sample.reshape(dec_batch * dec_seq, D_MODEL)
    ks, vs, ss = [], [], []
    for l in range(DEPTH):
        mod_l = mod[l]

        def layer_fn(x, cond, sq, rope_tables, s0, k_past, v_past):
            x = _ffn(x, mod_l, cond, ffn1_w_in, ffn1_w_out, ln_g3, ln_b3, l, 0)
            zh, u, q, k, v = _proj(x, mod_l, cond, w_in, qg, kg, l, rope_tables, sq)
            o_a, s_fin = _hgrn(zh, s0, hgrn_lb_logits, norm_g, l, sq)
            o_b = _conv(u, conv_w, conv_b3, conv_g3, conv_beta3, l, sq)
            if k_past is None:
                o_c = _attention(q, k, v, sq, sq, sq)
            else:
                n = x.shape[0] // sq
                k_all = jnp.concatenate([k_past, k.reshape(n, sq, KV_WIDTH)], axis=1)
                v_all = jnp.concatenate([v_past, v.reshape(n, sq, KV_WIDTH)], axis=1)
                sk = k_all.shape[1]
                o_c = _attention(q, k_all.reshape(n * sk, KV_WIDTH), v_all.reshape(n * sk, KV_WIDTH),
                                 sq, sk, ATTN_Q_TILE_LONG)
            x = _ffn(x, mod_l, cond, ffn2_w_in, ffn2_w_out, ln_g3, ln_b3, l, 2, mixer=(o_a, o_b, o_c, w_out))
            return x, k, v, s_fin

        xp, k_l, v_l, s_l = layer_fn(xp, ctx_cond, seq, None, zero_state, None, None)
        ks.append(k_l.reshape(batch, seq, C_KV_HEADS, HEAD_DIM))
        vs.append(v_l.reshape(batch, seq, C_KV_HEADS, HEAD_DIM))
        ss.append(s_l)
        xs, _, _, _ = layer_fn(xs, lat_cond, dec_seq, tables, lat_state[:, l],
                               cache_k[:, l].reshape(dec_batch, past, KV_WIDTH),
                               cache_v[:, l].reshape(dec_batch, past, KV_WIDTH))

    return (xp.reshape(batch, seq, D_MODEL), xs.reshape(dec_batch, dec_seq, D_MODEL),
            jnp.stack(ks, axis=1), jnp.stack(vs, axis=1), jnp.stack(ss, axis=1))
```

```python
import functools

import jax
import jax.numpy as jnp
from jax import lax
from jax.experimental import pallas as pl
from jax.experimental.pallas import tpu as pltpu

F32 = jnp.float32
BF16 = jnp.bfloat16

D_MODEL = 1024
DEPTH = 2
GRID_W = 64
HEAD_DIM = 64
A_HEADS = 4
A_DK = 64
A_WIDTH = 256
B_WIDTH = 256
CONV_K = 31
CONV_PAD = 15
C_HEADS = 8
C_KV_HEADS = 2
C_WIDTH = 512
KV_WIDTH = C_KV_HEADS * HEAD_DIM
D_FF = 2816
ROPE_THETA = 10000.0
ROPE_PAIRS = 16
N_MOD = 9
ALPHA = (2 * DEPTH) ** 0.25
F_MIN = 1e-6
LOG2_E = 1.4426950408889634
IN_WIDTH = 5 * A_WIDTH + 2 * B_WIDTH + C_WIDTH + 2 * KV_WIDTH
HG_WIDTH = 5 * A_WIDTH

LANES = 128
SUBLANES = 8
VMEM_BYTES_V7X = 64 * 1024 * 1024

ROW_TILE = 512
FFN_CHUNK = 256
HGRN_CHUNK = 128
HGRN_BASE = 8
CONV_TILE = 256
CONV_HALO = 16
ATTN_Q_TILE_LONG = 256
ATTN_KEY_BLOCK = 512
ATTN_SHORT_SEQS_PER_STEP = 4


def _params(semantics, vmem_mb):
    return pltpu.CompilerParams(dimension_semantics=semantics,
                                vmem_limit_bytes=min(vmem_mb * 1024 * 1024, VMEM_BYTES_V7X - (4 << 20)))


def _resident(block_shape, index_map):
    return pl.BlockSpec(block_shape, index_map, pipeline_mode=pl.Buffered(1))


def _sigmoid(x):
    return 1.0 / (1.0 + jnp.exp(-x))


def _layer_norm(y, g, b, eps=1e-5):
    mu = jnp.mean(y, axis=-1, keepdims=True)
    d = y - mu
    var = jnp.mean(d * d, axis=-1, keepdims=True)
    return d * lax.rsqrt(var + eps) * g + b


def _dot(a, b):
    return jnp.dot(a, b, preferred_element_type=F32)


def _dot_nt(a, b):
    return lax.dot_general(a, b, (((1,), (1,)), ((), ())), preferred_element_type=F32)


def _dot_tn(a, b):
    return lax.dot_general(a, b, (((0,), (0,)), ((), ())), preferred_element_type=F32)


def _segment_ones(width, seg):
    r = lax.broadcasted_iota(jnp.int32, (width, width), 0) // seg
    c = lax.broadcasted_iota(jnp.int32, (width, width), 1) // seg
    return (r == c).astype(BF16)


def _segment_sum(x, ones_bd):
    hi = x.astype(BF16)
    lo = (x - hi.astype(F32)).astype(BF16)
    return _dot(hi, ones_bd) + _dot(lo, ones_bd)


def _head_rms_norm(x, gain, eps=1e-6):
    ones_bd = _segment_ones(LANES, HEAD_DIM)
    cols = []
    for c in range(x.shape[1] // LANES):
        xc = x[:, c * LANES:(c + 1) * LANES]
        ms = _segment_sum(xc * xc, ones_bd) * (1.0 / HEAD_DIM)
        cols.append(xc * lax.rsqrt(ms + eps) * gain[:, c * LANES:(c + 1) * LANES])
    return cols[0] if len(cols) == 1 else jnp.concatenate(cols, axis=1)


def _mod_kernel(c_ref, w_ref, b_ref, o_ref):
    c = c_ref[...]
    s = (c * _sigmoid(c)).astype(BF16)
    o_ref[...] = _dot(s, w_ref[...].astype(BF16)) + b_ref[...]


def _modulation(conds, w_mod, b_mod):
    tn = D_MODEL
    return pl.pallas_call(
        _mod_kernel,
        grid=(DEPTH, N_MOD * D_MODEL // tn),
        in_specs=[
            pl.BlockSpec((SUBLANES, D_MODEL), lambda l, j: (0, 0)),
            pl.BlockSpec((None, D_MODEL, tn), lambda l, j: (l, 0, j)),
            pl.BlockSpec((None, 1, tn), lambda l, j: (l, 0, j)),
        ],
        out_specs=pl.BlockSpec((None, SUBLANES, tn), lambda l, j: (l, 0, j)),
        out_shape=jax.ShapeDtypeStruct((DEPTH, SUBLANES, N_MOD * D_MODEL), F32),
        compiler_params=_params(("arbitrary", "arbitrary"), 32),
        name="modulation",
    )(conds, w_mod, b_mod.reshape(DEPTH, 1, N_MOD * D_MODEL))


def _mod_spec(cond_of_block):
    return pl.BlockSpec((None, N_MOD, D_MODEL), lambda i: (cond_of_block(i), 0, 0))


def _ffn_kernel(*refs, sub, with_mixer_out):
    if with_mixer_out:
        (x_ref, oa_ref, ob_ref, oc_ref, mod_ref, wmix_ref, gmix_ref, bmix_ref,
         wg_ref, wu_ref, wo_ref, g_ref, b_ref, o_ref) = refs
        y = _dot(oa_ref[...], wmix_ref[0:A_WIDTH, :])
        y = y + _dot(ob_ref[...], wmix_ref[A_WIDTH:A_WIDTH + B_WIDTH, :])
        y = y + _dot(oc_ref[...], wmix_ref[A_WIDTH + B_WIDTH:, :])
        x = _layer_norm(ALPHA * x_ref[...] + mod_ref[5:6, :] * y, gmix_ref[...], bmix_ref[...])
    else:
        x_ref, mod_ref, wg_ref, wu_ref, wo_ref, g_ref, b_ref, o_ref = refs
        x = x_ref[...]
    shift = mod_ref[3 * sub:3 * sub + 1, :]
    scale = mod_ref[3 * sub + 1:3 * sub + 2, :]
    gate = mod_ref[3 * sub + 2:3 * sub + 3, :]
    h = (x * (1.0 + scale) + shift).astype(BF16)
    acc = jnp.zeros(x.shape, F32)
    for j in range(D_FF // FFN_CHUNK):
        cols = slice(j * FFN_CHUNK, (j + 1) * FFN_CHUNK)
        gt = _dot(h, wg_ref[:, cols])
        up = _dot(h, wu_ref[:, cols])
        act = (gt * _sigmoid(gt) * up).astype(BF16)
        acc = acc + _dot(act, wo_ref[cols, :])
    y = ALPHA * x + 0.5 * gate * acc
    o_ref[...] = _layer_norm(y, g_ref[...], b_ref[...])


def _ffn(x, mod_l, cond_of_block, w_in, w_out, ln_g, ln_b, layer, sub, mixer=None):
    rows = x.shape[0]
    row_spec = lambda width: pl.BlockSpec((ROW_TILE, width), lambda i: (i, 0))
    ln_spec = lambda idx: _resident((None, 1, D_MODEL), lambda i: (layer * 3 + idx, 0, 0))
    in_specs, args = [row_spec(D_MODEL)], [x]
    if mixer is not None:
        o_a, o_b, o_c, w_mix = mixer
        in_specs += [row_spec(A_WIDTH), row_spec(B_WIDTH), row_spec(C_WIDTH)]
        args += [o_a, o_b, o_c]
    in_specs.append(_mod_spec(cond_of_block))
    args.append(mod_l)
    if mixer is not None:
        in_specs += [_resident((None, D_MODEL, D_MODEL), lambda i: (layer, 0, 0)), ln_spec(1), ln_spec(1)]
        args += [w_mix, ln_g, ln_b]
    in_specs += [
        _resident((None, D_MODEL, D_FF), lambda i: (layer, 0, 0)),
        _resident((None, D_MODEL, D_FF), lambda i: (layer, 0, 1)),
        _resident((None, D_FF, D_MODEL), lambda i: (layer, 0, 0)),
        ln_spec(sub), ln_spec(sub),
    ]
    args += [w_in, w_in, w_out, ln_g, ln_b]
    return pl.pallas_call(
        functools.partial(_ffn_kernel, sub=sub, with_mixer_out=mixer is not None),
        grid=(rows // ROW_TILE,),
        in_specs=in_specs,
        out_specs=row_spec(D_MODEL),
        out_shape=jax.ShapeDtypeStruct((rows, D_MODEL), F32),
        compiler_params=_params(("parallel",), 52),
        name=f"ffn{sub // 2 + 1}",
    )(*args)


def _proj_kernel(*refs, rope):
    if rope:
        (x_ref, mod_ref, w_ref, qg_ref, kg_ref, cos_ref, s1_ref, s2_ref,
         zh_ref, u_ref, q_ref, k_ref, v_ref) = refs
    else:
        x_ref, mod_ref, w_ref, qg_ref, kg_ref, zh_ref, u_ref, q_ref, k_ref, v_ref = refs
    x = x_ref[...]
    h = (x * (1.0 + mod_ref[4:5, :]) + mod_ref[3:4, :]).astype(BF16)

    zh_ref[...] = _dot(h, w_ref[:, 0:HG_WIDTH])
    o = HG_WIDTH
    glu_a = _dot(h, w_ref[:, o:o + B_WIDTH])
    glu_b = _dot(h, w_ref[:, o + B_WIDTH:o + 2 * B_WIDTH])
    u_ref[...] = glu_a * _sigmoid(glu_b)
    o += 2 * B_WIDTH
    cq = _dot(h, w_ref[:, o:o + C_WIDTH])
    ck = _dot(h, w_ref[:, o + C_WIDTH:o + C_WIDTH + KV_WIDTH])
    v_ref[...] = _dot(h, w_ref[:, o + C_WIDTH + KV_WIDTH:o + C_WIDTH + 2 * KV_WIDTH])

    qn = _head_rms_norm(cq, qg_ref[...])
    kn = _head_rms_norm(ck, kg_ref[...])
    if rope:
        cos, s1, s2 = cos_ref[...], s1_ref[...], s2_ref[...]

        def rot(t):
            cols = []
            for c in range(t.shape[1] // LANES):
                tc = t[:, c * LANES:(c + 1) * LANES]
                cols.append(tc * cos + pltpu.roll(tc, LANES - ROPE_PAIRS, 1) * s1
                            + pltpu.roll(tc, ROPE_PAIRS, 1) * s2)
            return cols[0] if len(cols) == 1 else jnp.concatenate(cols, axis=1)

        qn, kn = rot(qn), rot(kn)
    q_ref[...] = (qn * (HEAD_DIM ** -0.5 * LOG2_E)).astype(BF16)
    k_ref[...] = kn


def _proj(x, mod_l, cond_of_block, w_in, qg, kg, layer, rope_tables, seq):
    rows = x.shape[0]
    rope = rope_tables is not None
    in_specs = [
        pl.BlockSpec((ROW_TILE, D_MODEL), lambda i: (i, 0)),
        _mod_spec(cond_of_block),
        _resident((None, D_MODEL, IN_WIDTH), lambda i: (layer, 0, 0)),
        _resident((None, 1, C_WIDTH), lambda i: (layer, 0, 0)),
        _resident((None, 1, KV_WIDTH), lambda i: (layer, 0, 0)),
    ]
    args = [x, mod_l, w_in, qg, kg]
    if rope:
        per_seq = seq // ROW_TILE
        in_specs += [pl.BlockSpec((ROW_TILE, LANES), lambda i: (i % per_seq, 0))] * 3
        args += list(rope_tables)
    return pl.pallas_call(
        functools.partial(_proj_kernel, rope=rope),
        grid=(rows // ROW_TILE,),
        in_specs=in_specs,
        out_specs=[
            pl.BlockSpec((ROW_TILE, HG_WIDTH), lambda i: (i, 0)),
            pl.BlockSpec((ROW_TILE, B_WIDTH), lambda i: (i, 0)),
            pl.BlockSpec((ROW_TILE, C_WIDTH), lambda i: (i, 0)),
            pl.BlockSpec((ROW_TILE, KV_WIDTH), lambda i: (i, 0)),
            pl.BlockSpec((ROW_TILE, KV_WIDTH), lambda i: (i, 0)),
        ],
        out_shape=[
            jax.ShapeDtypeStruct((rows, HG_WIDTH), F32),
            jax.ShapeDtypeStruct((rows, B_WIDTH), F32),
            jax.ShapeDtypeStruct((rows, C_WIDTH), BF16),
            jax.ShapeDtypeStruct((rows, KV_WIDTH), F32),
            jax.ShapeDtypeStruct((rows, KV_WIDTH), F32),
        ],
        compiler_params=_params(("parallel",), 48),
        name="mixer_in_proj",
    )(*args)


def _cumsum_rows(x, reverse):
    n = x.shape[0]
    row = lax.broadcasted_iota(jnp.int32, x.shape, 0)
    s = 1
    while s < n:
        if reverse:
            x = x + jnp.where(row < n - s, pltpu.roll(x, n - s, 0), 0.0)
        else:
            x = x + jnp.where(row >= s, pltpu.roll(x, s, 0), 0.0)
        s *= 2
    return x


def _hgrn_kernel(zh_ref, s0_ref, lbl_ref, ng_ref, o_ref, sfin_ref, fwd_ref, bwd_ref, *, layer, seq):
    C, W = HGRN_CHUNK, A_WIDTH
    n_chunks = seq // C

    lg = lbl_ref[...]
    e = jnp.exp(lg - jnp.max(lg, axis=0, keepdims=True))
    soft = e / jnp.sum(e, axis=0, keepdims=True)
    lb = jnp.zeros((1, W), F32)
    for j in range(1, layer + 1):
        lb = lb + soft[j:j + 1, :]

    lane_head = lax.broadcasted_iota(jnp.int32, (1, W), 1) // A_DK
    head_masks = [lane_head == h for h in range(A_HEADS)]
    t_idx = lax.broadcasted_iota(jnp.int32, (C, A_HEADS * C), 0)
    s_idx = lax.broadcasted_iota(jnp.int32, (C, A_HEADS * C), 1) % C
    pair_fwd = jnp.where(s_idx <= t_idx, t_idx ^ s_idx, 2 * C)
    pair_bwd = jnp.where(s_idx >= t_idx, t_idx ^ s_idx, 2 * C)
    bd_mask = (lax.broadcasted_iota(jnp.int32, (W, W), 0) // A_DK
               == lax.broadcasted_iota(jnp.int32, (W, W), 1) // A_DK)

    def stack_heads(t_bf):
        return jnp.concatenate([jnp.where(m, t_bf, jnp.zeros_like(t_bf)) for m in head_masks], axis=0)

    def chunk(n, st, reverse):
        r0 = pl.multiple_of(n * C, C)
        rows = pl.ds(r0, C)
        q = zh_ref[rows, 0:W]
        v = zh_ref[rows, W:2 * W]
        zf = zh_ref[rows, (3 if reverse else 2) * W:(4 if reverse else 3) * W]
        f = lb + (1.0 - lb) * _sigmoid(zf)
        kk = 1.0 - f
        a = _cumsum_rows(jnp.log(jnp.maximum(f, F_MIN)), reverse) * LOG2_E
        pair = pair_bwd if reverse else pair_fwd

        v_bf = v.astype(BF16)
        scores = jnp.zeros((C, A_HEADS * C), F32)
        h = C // 2
        while h >= HGRN_BASE:
            a3 = a.reshape(C // h, h, W)
            zero = jnp.zeros((1, 1, W), F32)
            if reverse:
                own = a3[:, 0:1, :]
                other = jnp.concatenate([own[1:], zero], axis=0)
            else:
                own = a3[:, h - 1:h, :]
                other = jnp.concatenate([zero, own[:-1]], axis=0)
            qt = (q.reshape(C // h, h, W) * jnp.exp2(a3 - other)).reshape(C, W).astype(BF16)
            kt = (kk.reshape(C // h, h, W) * jnp.exp2(own - a3)).reshape(C, W).astype(BF16)
            scores = jnp.where(pair < 2 * h, _dot_nt(qt, stack_heads(kt)), scores)
            h //= 2
        hb = HGRN_BASE
        a3 = a.reshape(C // hb, hb, W)
        ref_row = a3[:, hb // 2:hb // 2 + 1, :] if reverse else a3[:, hb // 2 - 1:hb // 2, :]
        qt = (q.reshape(C // hb, hb, W) * jnp.exp2(a3 - ref_row)).reshape(C, W).astype(BF16)
        kt = (kk.reshape(C // hb, hb, W) * jnp.exp2(ref_row - a3)).reshape(C, W).astype(BF16)
        scores = jnp.where(pair < hb, _dot_nt(qt, stack_heads(kt)), scores)

        o = _dot(scores.astype(BF16), stack_heads(v_bf))
        o = o + _dot_nt((q * jnp.exp2(a)).astype(BF16), st.astype(BF16))
        a_last = a[0:1, :] if reverse else a[C - 1:C, :]
        kh = (kk * jnp.exp2(a_last - a)).astype(BF16)
        st = st * jnp.exp2(a_last) + jnp.where(bd_mask, _dot_tn(v_bf, kh), 0.0)
        return rows, o, st

    def scan_body(i, carry):
        st_f, st_b = carry
        rows_f, o_f, st_f = chunk(i, st_f, False)
        fwd_ref[rows_f, :] = o_f
        rows_b, o_b, st_b = chunk(n_chunks - 1 - i, st_b, True)
        bwd_ref[rows_b, :] = o_b
        return st_f, st_b

    finals = lax.fori_loop(0, n_chunks, scan_body, (s0_ref[0], s0_ref[1]), unroll=2)

    ones_bd = _segment_ones(LANES, A_DK)

    def norm_body(n, carry):
        rows = pl.ds(pl.multiple_of(n * C, C), C)
        tot = fwd_ref[rows, :] + bwd_ref[rows, :]
        zg = zh_ref[rows, 4 * W:5 * W]
        cols = []
        for c in range(W // LANES):
            lanes = slice(c * LANES, (c + 1) * LANES)
            tc = tot[:, lanes]
            ms = _segment_sum(tc * tc, ones_bd) * (1.0 / A_DK)
            cols.append(tc * lax.rsqrt(ms + 1e-6) * ng_ref[:, lanes])
        on = jnp.concatenate(cols, axis=1)
        o_ref[rows, :] = (on * (zg * _sigmoid(zg))).astype(o_ref.dtype)
        return carry

    lax.fori_loop(0, n_chunks, norm_body, 0)

    for d, st in enumerate(finals):
        s_kv = st.T
        for h in range(A_HEADS):
            blk = slice(h * A_DK, (h + 1) * A_DK)
            sfin_ref[d, h] = s_kv[blk, blk]


def _hgrn(zh, s0, lb_logits, norm_g, layer, seq):
    rows = zh.shape[0]
    n_seq = rows // seq
    per_seq_state = s0.shape[0] != 1
    W = A_WIDTH
    return pl.pallas_call(
        functools.partial(_hgrn_kernel, layer=layer, seq=seq),
        grid=(n_seq,),
        in_specs=[
            pl.BlockSpec((seq, HG_WIDTH), lambda b: (b, 0)),
            pl.BlockSpec((None, 2, W, W), (lambda b: (b, 0, 0, 0)) if per_seq_state else (lambda b: (0, 0, 0, 0))),
            _resident((DEPTH, W), lambda b: (0, 0)),
            _resident((None, 1, W), lambda b: (layer, 0, 0)),
        ],
        out_specs=[
            pl.BlockSpec((seq, W), lambda b: (b, 0)),
            pl.BlockSpec((None, 2, A_HEADS, A_DK, A_DK), lambda b: (b, 0, 0, 0, 0)),
        ],
        out_shape=[
            jax.ShapeDtypeStruct((rows, W), BF16),
            jax.ShapeDtypeStruct((n_seq, 2, A_HEADS, A_DK, A_DK), F32),
        ],
        scratch_shapes=[pltpu.VMEM((seq, W), F32), pltpu.VMEM((seq, W), F32)],
        compiler_params=_params(("parallel",), 48),
        name="hgrn2",
    )(zh, s0, lb_logits, norm_g)


def _conv_kernel(up_ref, uc_ref, un_ref, w_ref, b_ref, g_ref, beta_ref, o_ref, sh_ref, *, tiles_per_seq):
    i = pl.program_id(0)
    T, H, S = CONV_TILE, CONV_HALO, SUBLANES
    P = T + 2 * H
    first = (i % tiles_per_seq) == 0
    last = (i % tiles_per_seq) == tiles_per_seq - 1
    sh_ref[0, 0:H, :] = jnp.where(first, 0.0, up_ref[...])
    sh_ref[0, H:H + T, :] = uc_ref[...]
    sh_ref[0, H + T:P, :] = jnp.where(last, 0.0, un_ref[...])
    for b in range(1, S):
        sh_ref[b, 0:P - S, :] = sh_ref[0, b:b + P - S, :]
    acc = jnp.zeros((T, B_WIDTH), F32)
    for j in range(CONV_K):
        start = H - CONV_PAD + j
        a, b = start // S, start % S
        acc = acc + w_ref[j:j + 1, :] * sh_ref[b, a * S:a * S + T, :]
    y = _layer_norm(acc + b_ref[...], g_ref[...], beta_ref[...])
    o_ref[...] = (y * _sigmoid(y)).astype(o_ref.dtype)


def _conv(u, conv_w, conv_b, ln_g, ln_b, layer, seq):
    rows = u.shape[0]
    T, H = CONV_TILE, CONV_HALO
    tiles_per_seq = seq // T
    ratio = T // H
    n_halo_blocks = rows // H
    return pl.pallas_call(
        functools.partial(_conv_kernel, tiles_per_seq=tiles_per_seq),
        grid=(rows // T,),
        in_specs=[
            pl.BlockSpec((H, B_WIDTH), lambda i: (jnp.maximum(i * ratio - 1, 0), 0)),
            pl.BlockSpec((T, B_WIDTH), lambda i: (i, 0)),
            pl.BlockSpec((H, B_WIDTH), lambda i: (jnp.minimum((i + 1) * ratio, n_halo_blocks - 1), 0)),
            _resident((None, CONV_K, B_WIDTH), lambda i: (layer, 0, 0)),
            _resident((None, 1, B_WIDTH), lambda i: (layer, 0, 0)),
            _resident((None, 1, B_WIDTH), lambda i: (layer, 0, 0)),
            _resident((None, 1, B_WIDTH), lambda i: (layer, 0, 0)),
        ],
        out_specs=pl.BlockSpec((T, B_WIDTH), lambda i: (i, 0)),
        out_shape=jax.ShapeDtypeStruct((rows, B_WIDTH), BF16),
        scratch_shapes=[pltpu.VMEM((SUBLANES, T + 2 * H, B_WIDTH), F32)],
        compiler_params=_params(("parallel",), 16),
        name="conv_module",
    )(u, u, u, conv_w, conv_b, ln_g, ln_b)


def _attn_kernel(q_ref, k_ref, v_ref, o_ref, kd_ref, va_ref, *, n_seq):
    tq = q_ref.shape[0] // n_seq
    seq_k = k_ref.shape[0] // n_seq
    kb = min(seq_k, ATTN_KEY_BLOCK)
    group = C_HEADS // C_KV_HEADS
    half = lax.broadcasted_iota(jnp.int32, (1, LANES), 1) // HEAD_DIM

    @pl.when(pl.program_id(1) == 0)
    def _():
        k = k_ref[...]
        k_swapped = pltpu.roll(k, HEAD_DIM, 1)
        v = v_ref[...]
        for g in range(C_KV_HEADS):
            kd_ref[g] = jnp.where(half == g, k, k_swapped).astype(BF16)
            va_ref[g] = jnp.where(half == g, v, 1.0).astype(BF16)

    n_blocks = seq_k // kb
    chains = [(sq, g) for sq in range(n_seq) for g in range(C_KV_HEADS)]
    units = [(c, b) for b in range(n_blocks) for c in range(len(chains))]
    q_stacked, run_max, acc = {}, {}, {}
    scores, probs, rescale = {}, {}, {}

    def stage_scores(u):
        c, b = units[u]
        sq, g = chains[c]
        if c not in q_stacked:
            heads = []
            for j in range(group):
                hq = g * group + j
                col = q_ref[sq * tq:(sq + 1) * tq, (hq // 2) * LANES:(hq // 2 + 1) * LANES]
                heads.append(jnp.where(half == hq % 2, col, jnp.zeros_like(col)))
            q_stacked[c] = jnp.concatenate(heads, axis=0)
        keys = slice(sq * seq_k + b * kb, sq * seq_k + (b + 1) * kb)
        scores[u] = _dot_nt(q_stacked[c], kd_ref[g, keys, :])

    def stage_exp(u):
        c, b = units[u]
        s = scores.pop(u)
        m_blk = jnp.max(s, axis=-1, keepdims=True)
        m_new = m_blk if b == 0 else jnp.maximum(run_max[c], m_blk)
        probs[u] = jnp.exp2(s - m_new).astype(BF16)
        if b > 0:
            rescale[u] = jnp.exp2(run_max[c] - m_new)
        run_max[c] = m_new

    def stage_values(u):
        c, b = units[u]
        sq, g = chains[c]
        keys = slice(sq * seq_k + b * kb, sq * seq_k + (b + 1) * kb)
        pv = _dot(probs.pop(u), va_ref[g, keys, :])
        acc[c] = pv if b == 0 else rescale.pop(u) * acc[c] + pv
        if b == n_blocks - 1:
            total = acc.pop(c)
            o = total / pltpu.roll(total, HEAD_DIM, 1)
            for j in range(group):
                hq = g * group + j
                o_ref[sq * tq:(sq + 1) * tq, hq * HEAD_DIM:(hq + 1) * HEAD_DIM] = (
                    o[j * tq:(j + 1) * tq, g * HEAD_DIM:(g + 1) * HEAD_DIM].astype(o_ref.dtype))

    for t in range(len(units) + 2):
        if t < len(units):
            stage_scores(t)
        if 0 <= t - 1 < len(units):
            stage_exp(t - 1)
        if 0 <= t - 2 < len(units):
            stage_values(t - 2)


def _attention(q, k, v, seq_q, seq_k, tq, seqs_per_step=1):
    n_seq = q.shape[0] // seq_q
    q_tiles = seq_q // tq
    assert seqs_per_step == 1 or q_tiles == 1
    q_rows, k_rows = seqs_per_step * tq, seqs_per_step * seq_k
    return pl.pallas_call(
        functools.partial(_attn_kernel, n_seq=seqs_per_step),
        grid=(n_seq // seqs_per_step, q_tiles),
        in_specs=[
            pl.BlockSpec((q_rows, C_WIDTH), lambda b, i: (b * q_tiles + i, 0)),
            pl.BlockSpec((k_rows, KV_WIDTH), lambda b, i: (b, 0)),
            pl.BlockSpec((k_rows, KV_WIDTH), lambda b, i: (b, 0)),
        ],
        out_specs=pl.BlockSpec((q_rows, C_WIDTH), lambda b, i: (b * q_tiles + i, 0)),
        out_shape=jax.ShapeDtypeStruct(q.shape, BF16),
        scratch_shapes=[pltpu.VMEM((C_KV_HEADS, k_rows, LANES), BF16),
                        pltpu.VMEM((C_KV_HEADS, k_rows, LANES), BF16)],
        compiler_params=_params(("parallel", "arbitrary"), 48),
        name="attention",
    )(q, k, v)


def _rope_tables(seq):
    pos = jnp.arange(seq)
    row_id = (pos // GRID_W).astype(F32)
    col_id = (pos % GRID_W).astype(F32)
    inv = ROPE_THETA ** (-jnp.arange(ROPE_PAIRS, dtype=F32) / ROPE_PAIRS)
    lane = jnp.arange(LANES) % HEAD_DIM
    use_col = (lane // (2 * ROPE_PAIRS)) == 1
    first = (lane % (2 * ROPE_PAIRS)) < ROPE_PAIRS
    freq = inv[lane % ROPE_PAIRS]
    ang = jnp.where(use_col[None, :], col_id[:, None], row_id[:, None]) * freq[None, :]
    cos, sin = jnp.cos(ang), jnp.sin(ang)
    return cos, jnp.where(first[None, :], -sin, 0.0), jnp.where(first[None, :], 0.0, sin)


def _state_to_block_diag(s):
    eye = jnp.eye(A_HEADS, dtype=s.dtype)
    st = jnp.swapaxes(s, -1, -2)
    bd = st[..., :, :, None, :] * eye[:, None, :, None]
    return bd.reshape(s.shape[:-3] + (A_WIDTH, A_WIDTH))


def kernel(x_prompt, x_sample, cache_k, cache_v, state_hgrn, c, c_ctx, w_mod, b_mod, ln_g, ln_b, ffn1_w_in, ffn1_w_out, ffn2_w_in, ffn2_w_out, w_in, w_out, hgrn_lb_logits, hgrn_norm_g, conv_w, conv_b, conv_ln_g, conv_ln_b, q_norm_g, k_norm_g):
    batch, seq, _ = x_prompt.shape
    dec_batch, dec_seq, _ = x_sample.shape
    past = cache_k.shape[2]
    assert seq % ROW_TILE == 0 or ROW_TILE % seq == 0
    assert dec_seq % ROW_TILE == 0 and dec_batch + 1 <= SUBLANES
    assert KV_WIDTH == LANES and C_HEADS % (2 * C_KV_HEADS) == 0

    bf = lambda w: w.astype(BF16)
    ffn1_w_in, ffn1_w_out, ffn2_w_in, ffn2_w_out, w_in, w_out = map(
        bf, (ffn1_w_in, ffn1_w_out, ffn2_w_in, ffn2_w_out, w_in, w_out))

    conds = jnp.zeros((SUBLANES, D_MODEL), F32).at[0].set(c_ctx).at[1:1 + dec_batch].set(c)
    mod = _modulation(conds, w_mod, b_mod).reshape(DEPTH, SUBLANES, N_MOD, D_MODEL)

    ln_g3 = ln_g.reshape(DEPTH * 3, 1, D_MODEL)
    ln_b3 = ln_b.reshape(DEPTH * 3, 1, D_MODEL)
    qg = jnp.tile(q_norm_g, (1, C_HEADS)).reshape(DEPTH, 1, C_WIDTH)
    kg = jnp.tile(k_norm_g, (1, C_KV_HEADS)).reshape(DEPTH, 1, KV_WIDTH)
    norm_g = hgrn_norm_g.reshape(DEPTH, 1, A_WIDTH)
    conv_b3 = conv_b.reshape(DEPTH, 1, B_WIDTH)
    conv_g3 = conv_ln_g.reshape(DEPTH, 1, B_WIDTH)
    conv_beta3 = conv_ln_b.reshape(DEPTH, 1, B_WIDTH)
    tables = _rope_tables(dec_seq)
    zero_state = jnp.zeros((1, 2, A_WIDTH, A_WIDTH), F32)
    lat_state = _state_to_block_diag(state_hgrn)

    blocks_per_lat = dec_seq // ROW_TILE
    ctx_cond = lambda i: 0
    lat_cond = lambda i: 1 + i // blocks_per_lat

    xp = x_prompt.reshape(batch * seq, D_MODEL)
    xs = x_sample.reshape(dec_batch * dec_seq, D_MODEL)
    ks, vs, ss = [], [], []
    for l in range(DEPTH):
        mod_l = mod[l]

        def layer_fn(x, cond, sq, rope_tables, s0, k_past, v_past):
            x = _ffn(x, mod_l, cond, ffn1_w_in, ffn1_w_out, ln_g3, ln_b3, l, 0)
            zh, u, q, k, v = _proj(x, mod_l, cond, w_in, qg, kg, l, rope_tables, sq)
            o_a, s_fin = _hgrn(zh, s0, hgrn_lb_logits, norm_g, l, sq)
            o_b = _conv(u, conv_w, conv_b3, conv_g3, conv_beta3, l, sq)
            if k_past is None:
                o_c = _attention(q, k, v, sq, sq, sq, seqs_per_step=ATTN_SHORT_SEQS_PER_STEP)
            else:
                n = x.shape[0] // sq
                k_all = jnp.concatenate([k_past, k.reshape(n, sq, KV_WIDTH)], axis=1)
                v_all = jnp.concatenate([v_past, v.reshape(n, sq, KV_WIDTH)], axis=1)
                sk = k_all.shape[1]
                o_c = _attention(q, k_all.reshape(n * sk, KV_WIDTH), v_all.reshape(n * sk, KV_WIDTH),
                                 sq, sk, ATTN_Q_TILE_LONG)
            x = _ffn(x, mod_l, cond, ffn2_w_in, ffn2_w_out, ln_g3, ln_b3, l, 2, mixer=(o_a, o_b, o_c, w_out))
            return x, k, v, s_fin

        xp, k_l, v_l, s_l = layer_fn(xp, ctx_cond, seq, None, zero_state, None, None)
        ks.append(k_l.reshape(batch, seq, C_KV_HEADS, HEAD_DIM))
        vs.append(v_l.reshape(batch, seq, C_KV_HEADS, HEAD_DIM))
        ss.append(s_l)
        xs, _, _, _ = layer_fn(xs, lat_cond, dec_seq, tables, lat_state[:, l],
                               cache_k[:, l].reshape(dec_batch, past, KV_WIDTH),
                               cache_v[:, l].reshape(dec_batch, past, KV_WIDTH))

    return (xp.reshape(batch, seq, D_MODEL), xs.reshape(dec_batch, dec_seq, D_MODEL),
            jnp.stack(ks, axis=1), jnp.stack(vs, axis=1), jnp.stack(ss, axis=1))
```

```python
import functools

import jax
import jax.numpy as jnp
from jax import lax
from jax.experimental import pallas as pl
from jax.experimental.pallas import tpu as pltpu

F32 = jnp.float32
BF16 = jnp.bfloat16

D_MODEL = 1024
DEPTH = 2
GRID_W = 64
HEAD_DIM = 64
A_HEADS = 4
A_DK = 64
A_WIDTH = 256
B_WIDTH = 256
CONV_K = 31
CONV_PAD = 15
C_HEADS = 8
C_KV_HEADS = 2
C_WIDTH = 512
KV_WIDTH = C_KV_HEADS * HEAD_DIM
D_FF = 2816
ROPE_THETA = 10000.0
ROPE_PAIRS = 16
N_MOD = 9
ALPHA = (2 * DEPTH) ** 0.25
F_MIN = 1e-6
LOG2_E = 1.4426950408889634
IN_WIDTH = 5 * A_WIDTH + 2 * B_WIDTH + C_WIDTH + 2 * KV_WIDTH
HG_WIDTH = 5 * A_WIDTH

LANES = 128
SUBLANES = 8
VMEM_BYTES_V7X = 64 * 1024 * 1024

ROW_TILE = 512
FFN_CHUNK = 256
HGRN_CHUNK = 128
HGRN_BASE = 8
CONV_TILE = 256
CONV_HALO = 16
ATTN_Q_TILE_LONG = 256
ATTN_KEY_BLOCK = 512
ATTN_SHORT_SEQS_PER_STEP = 4


def _params(semantics, vmem_mb):
    return pltpu.CompilerParams(dimension_semantics=semantics,
                                vmem_limit_bytes=min(vmem_mb * 1024 * 1024, VMEM_BYTES_V7X - (4 << 20)))


def _resident(block_shape, index_map):
    return pl.BlockSpec(block_shape, index_map, pipeline_mode=pl.Buffered(1))


def _sigmoid(x):
    return 1.0 / (1.0 + jnp.exp(-x))


def _layer_norm(y, g, b, eps=1e-5):
    mu = jnp.mean(y, axis=-1, keepdims=True)
    d = y - mu
    var = jnp.mean(d * d, axis=-1, keepdims=True)
    return d * lax.rsqrt(var + eps) * g + b


def _dot(a, b):
    return jnp.dot(a, b, preferred_element_type=F32)


def _dot_nt(a, b):
    return lax.dot_general(a, b, (((1,), (1,)), ((), ())), preferred_element_type=F32)


def _dot_tn(a, b):
    return lax.dot_general(a, b, (((0,), (0,)), ((), ())), preferred_element_type=F32)


def _segment_ones(width, seg):
    r = lax.broadcasted_iota(jnp.int32, (width, width), 0) // seg
    c = lax.broadcasted_iota(jnp.int32, (width, width), 1) // seg
    return (r == c).astype(BF16)


def _segment_sum(x, ones_bd):
    hi = x.astype(BF16)
    lo = (x - hi.astype(F32)).astype(BF16)
    return _dot(hi, ones_bd) + _dot(lo, ones_bd)


def _head_rms_norm(x, gain, eps=1e-6):
    ones_bd = _segment_ones(LANES, HEAD_DIM)
    cols = []
    for c in range(x.shape[1] // LANES):
        xc = x[:, c * LANES:(c + 1) * LANES]
        ms = _segment_sum(xc * xc, ones_bd) * (1.0 / HEAD_DIM)
        cols.append(xc * lax.rsqrt(ms + eps) * gain[:, c * LANES:(c + 1) * LANES])
    return cols[0] if len(cols) == 1 else jnp.concatenate(cols, axis=1)


def _mod_kernel(c_ref, w_ref, b_ref, o_ref):
    c = c_ref[...]
    s = (c * _sigmoid(c)).astype(BF16)
    o_ref[...] = _dot(s, w_ref[...].astype(BF16)) + b_ref[...]


def _modulation(conds, w_mod, b_mod):
    tn = D_MODEL
    return pl.pallas_call(
        _mod_kernel,
        grid=(DEPTH, N_MOD * D_MODEL // tn),
        in_specs=[
            pl.BlockSpec((SUBLANES, D_MODEL), lambda l, j: (0, 0)),
            pl.BlockSpec((None, D_MODEL, tn), lambda l, j: (l, 0, j)),
            pl.BlockSpec((None, 1, tn), lambda l, j: (l, 0, j)),
        ],
        out_specs=pl.BlockSpec((None, SUBLANES, tn), lambda l, j: (l, 0, j)),
        out_shape=jax.ShapeDtypeStruct((DEPTH, SUBLANES, N_MOD * D_MODEL), F32),
        compiler_params=_params(("arbitrary", "arbitrary"), 32),
        name="modulation",
    )(conds, w_mod, b_mod.reshape(DEPTH, 1, N_MOD * D_MODEL))


def _mod_spec(cond_of_block):
    return pl.BlockSpec((None, N_MOD, D_MODEL), lambda i: (cond_of_block(i), 0, 0))


def _ffn_kernel(*refs, sub, with_mixer_out, mixer_in_rope):
    refs = list(refs)
    x_ref = refs.pop(0)
    if with_mixer_out:
        oa_ref, ob_ref, oc_ref = refs[:3]
        del refs[:3]
    mod_ref = refs.pop(0)
    if with_mixer_out:
        wmix_ref, gmix_ref, bmix_ref = refs[:3]
        del refs[:3]
        y = _dot(oa_ref[...], wmix_ref[0:A_WIDTH, :])
        y = y + _dot(ob_ref[...], wmix_ref[A_WIDTH:A_WIDTH + B_WIDTH, :])
        y = y + _dot(oc_ref[...], wmix_ref[A_WIDTH + B_WIDTH:, :])
        x = _layer_norm(ALPHA * x_ref[...] + mod_ref[5:6, :] * y, gmix_ref[...], bmix_ref[...])
    else:
        x = x_ref[...]
    wg_ref, wu_ref, wo_ref, g_ref, b_ref = refs[:5]
    del refs[:5]
    if mixer_in_rope is not None:
        n_in = 6 if mixer_in_rope else 3
        mixer_in_refs, refs = refs[:n_in], refs[n_in:]
    o_ref = refs.pop(0)
    shift = mod_ref[3 * sub:3 * sub + 1, :]
    scale = mod_ref[3 * sub + 1:3 * sub + 2, :]
    gate = mod_ref[3 * sub + 2:3 * sub + 3, :]
    h = (x * (1.0 + scale) + shift).astype(BF16)
    acc = jnp.zeros(x.shape, F32)
    for j in range(D_FF // FFN_CHUNK):
        cols = slice(j * FFN_CHUNK, (j + 1) * FFN_CHUNK)
        gt = _dot(h, wg_ref[:, cols])
        up = _dot(h, wu_ref[:, cols])
        act = (gt * _sigmoid(gt) * up).astype(BF16)
        acc = acc + _dot(act, wo_ref[cols, :])
    y = ALPHA * x + 0.5 * gate * acc
    x_new = _layer_norm(y, g_ref[...], b_ref[...])
    o_ref[...] = x_new
    if mixer_in_rope is not None:
        _mixer_in(x_new, mod_ref, *mixer_in_refs, *refs)


def _ffn(x, mod_l, cond_of_block, w_in, w_out, ln_g, ln_b, layer, sub, mixer_out=None, mixer_in=None):
    rows = x.shape[0]
    row_spec = lambda width: pl.BlockSpec((ROW_TILE, width), lambda i: (i, 0))
    ln_spec = lambda idx: _resident((None, 1, D_MODEL), lambda i: (layer * 3 + idx, 0, 0))
    in_specs, args = [row_spec(D_MODEL)], [x]
    if mixer_out is not None:
        o_a, o_b, o_c, w_mix = mixer_out
        in_specs += [row_spec(A_WIDTH), row_spec(B_WIDTH), row_spec(C_WIDTH)]
        args += [o_a, o_b, o_c]
    in_specs.append(_mod_spec(cond_of_block))
    args.append(mod_l)
    if mixer_out is not None:
        in_specs += [_resident((None, D_MODEL, D_MODEL), lambda i: (layer, 0, 0)), ln_spec(1), ln_spec(1)]
        args += [w_mix, ln_g, ln_b]
    in_specs += [
        _resident((None, D_MODEL, D_FF), lambda i: (layer, 0, 0)),
        _resident((None, D_MODEL, D_FF), lambda i: (layer, 0, 1)),
        _resident((None, D_FF, D_MODEL), lambda i: (layer, 0, 0)),
        ln_spec(sub), ln_spec(sub),
    ]
    args += [w_in, w_in, w_out, ln_g, ln_b]
    out_specs = [row_spec(D_MODEL)]
    out_shape = [jax.ShapeDtypeStruct((rows, D_MODEL), F32)]
    rope = None
    if mixer_in is not None:
        w_mix_in, qg, kg, rope_tables, seq = mixer_in
        rope = rope_tables is not None
        in_specs += [
            _resident((None, D_MODEL, IN_WIDTH), lambda i: (layer, 0, 0)),
            _resident((None, 1, C_WIDTH), lambda i: (layer, 0, 0)),
            _resident((None, 1, KV_WIDTH), lambda i: (layer, 0, 0)),
        ]
        args += [w_mix_in, qg, kg]
        if rope:
            per_seq = seq // ROW_TILE
            in_specs += [pl.BlockSpec((ROW_TILE, LANES), lambda i: (i % per_seq, 0))] * 3
            args += list(rope_tables)
        widths = ((HG_WIDTH, F32), (B_WIDTH, F32), (C_WIDTH, BF16), (KV_WIDTH, F32), (KV_WIDTH, F32))
        out_specs += [row_spec(w) for w, _ in widths]
        out_shape += [jax.ShapeDtypeStruct((rows, w), dt) for w, dt in widths]
    return pl.pallas_call(
        functools.partial(_ffn_kernel, sub=sub, with_mixer_out=mixer_out is not None, mixer_in_rope=rope),
        grid=(rows // ROW_TILE,),
        in_specs=in_specs,
        out_specs=out_specs,
        out_shape=out_shape,
        compiler_params=_params(("parallel",), 56),
        name=f"ffn{sub // 2 + 1}",
    )(*args)


def _mixer_in(x, mod_ref, w_ref, qg_ref, kg_ref, *refs):
    rope = len(refs) == 8
    if rope:
        cos_ref, s1_ref, s2_ref = refs[:3]
    zh_ref, u_ref, q_ref, k_ref, v_ref = refs[-5:]
    h = (x * (1.0 + mod_ref[4:5, :]) + mod_ref[3:4, :]).astype(BF16)

    zh_ref[...] = _dot(h, w_ref[:, 0:HG_WIDTH])
    o = HG_WIDTH
    glu_a = _dot(h, w_ref[:, o:o + B_WIDTH])
    glu_b = _dot(h, w_ref[:, o + B_WIDTH:o + 2 * B_WIDTH])
    u_ref[...] = glu_a * _sigmoid(glu_b)
    o += 2 * B_WIDTH
    cq = _dot(h, w_ref[:, o:o + C_WIDTH])
    ck = _dot(h, w_ref[:, o + C_WIDTH:o + C_WIDTH + KV_WIDTH])
    v_ref[...] = _dot(h, w_ref[:, o + C_WIDTH + KV_WIDTH:o + C_WIDTH + 2 * KV_WIDTH])

    qn = _head_rms_norm(cq, qg_ref[...])
    kn = _head_rms_norm(ck, kg_ref[...])
    if rope:
        cos, s1, s2 = cos_ref[...], s1_ref[...], s2_ref[...]

        def rot(t):
            cols = []
            for c in range(t.shape[1] // LANES):
                tc = t[:, c * LANES:(c + 1) * LANES]
                cols.append(tc * cos + pltpu.roll(tc, LANES - ROPE_PAIRS, 1) * s1
                            + pltpu.roll(tc, ROPE_PAIRS, 1) * s2)
            return cols[0] if len(cols) == 1 else jnp.concatenate(cols, axis=1)

        qn, kn = rot(qn), rot(kn)
    q_ref[...] = (qn * (HEAD_DIM ** -0.5 * LOG2_E)).astype(BF16)
    k_ref[...] = kn


def _cumsum_rows(x, reverse):
    n = x.shape[0]
    row = lax.broadcasted_iota(jnp.int32, x.shape, 0)
    s = 1
    while s < n:
        if reverse:
            x = x + jnp.where(row < n - s, pltpu.roll(x, n - s, 0), 0.0)
        else:
            x = x + jnp.where(row >= s, pltpu.roll(x, s, 0), 0.0)
        s *= 2
    return x


def _hgrn_kernel(zh_ref, s0_ref, lbl_ref, ng_ref, o_ref, sfin_ref, fwd_ref, bwd_ref, *, layer, seq):
    C, W = HGRN_CHUNK, A_WIDTH
    n_chunks = seq // C

    lg = lbl_ref[...]
    e = jnp.exp(lg - jnp.max(lg, axis=0, keepdims=True))
    soft = e / jnp.sum(e, axis=0, keepdims=True)
    lb = jnp.zeros((1, W), F32)
    for j in range(1, layer + 1):
        lb = lb + soft[j:j + 1, :]

    lane_head = lax.broadcasted_iota(jnp.int32, (1, W), 1) // A_DK
    head_masks = [lane_head == h for h in range(A_HEADS)]
    t_idx = lax.broadcasted_iota(jnp.int32, (C, A_HEADS * C), 0)
    s_idx = lax.broadcasted_iota(jnp.int32, (C, A_HEADS * C), 1) % C
    pair_fwd = jnp.where(s_idx <= t_idx, t_idx ^ s_idx, 2 * C)
    pair_bwd = jnp.where(s_idx >= t_idx, t_idx ^ s_idx, 2 * C)
    bd_mask = (lax.broadcasted_iota(jnp.int32, (W, W), 0) // A_DK
               == lax.broadcasted_iota(jnp.int32, (W, W), 1) // A_DK)
    def stack_heads(t_bf):
        return jnp.concatenate([jnp.where(m, t_bf, jnp.zeros_like(t_bf)) for m in head_masks], axis=0)

    class Unit:
        def __init__(self, n, reverse):
            self.reverse = reverse
            self.rows = pl.ds(pl.multiple_of(n * C, C), C)
            self.pair = pair_bwd if reverse else pair_fwd
            self.scores = jnp.zeros((C, A_HEADS * C), F32)

        def gates(self):
            rows, reverse = self.rows, self.reverse
            self.q = zh_ref[rows, 0:W]
            self.v_bf = zh_ref[rows, W:2 * W].astype(BF16)
            zf = zh_ref[rows, (3 if reverse else 2) * W:(4 if reverse else 3) * W]
            f = lb + (1.0 - lb) * _sigmoid(zf)
            self.kk = 1.0 - f
            self.a = _cumsum_rows(jnp.log(jnp.maximum(f, F_MIN)), reverse) * LOG2_E

        def level(self, h):
            a3 = self.a.reshape(C // h, h, W)
            zero = jnp.zeros((1, 1, W), F32)
            if self.reverse:
                own = a3[:, 0:1, :]
                other = jnp.concatenate([own[1:], zero], axis=0)
            else:
                own = a3[:, h - 1:h, :]
                other = jnp.concatenate([zero, own[:-1]], axis=0)
            qt = (self.q.reshape(C // h, h, W) * jnp.exp2(a3 - other)).reshape(C, W).astype(BF16)
            kt = (self.kk.reshape(C // h, h, W) * jnp.exp2(own - a3)).reshape(C, W).astype(BF16)
            self.scores = jnp.where(self.pair < 2 * h, _dot_nt(qt, stack_heads(kt)), self.scores)

        def base(self):
            hb = HGRN_BASE
            a3 = self.a.reshape(C // hb, hb, W)
            ref_row = a3[:, hb // 2:hb // 2 + 1, :] if self.reverse else a3[:, hb // 2 - 1:hb // 2, :]
            qt = (self.q.reshape(C // hb, hb, W) * jnp.exp2(a3 - ref_row)).reshape(C, W).astype(BF16)
            kt = (self.kk.reshape(C // hb, hb, W) * jnp.exp2(ref_row - a3)).reshape(C, W).astype(BF16)
            self.scores = jnp.where(self.pair < hb, _dot_nt(qt, stack_heads(kt)), self.scores)

        def intra(self):
            self.o = _dot(self.scores.astype(BF16), stack_heads(self.v_bf))
            self.qe = (self.q * jnp.exp2(self.a)).astype(BF16)
            self.a_last = self.a[0:1, :] if self.reverse else self.a[C - 1:C, :]
            self.kv = jnp.where(bd_mask, _dot_tn(self.v_bf, (self.kk * jnp.exp2(self.a_last - self.a)).astype(BF16)),
                                0.0)

        def carry(self, st):
            out_ref = bwd_ref if self.reverse else fwd_ref
            out_ref[self.rows, :] = self.o + _dot_nt(self.qe, st.astype(BF16))
            return st * jnp.exp2(self.a_last) + self.kv

    def scan_body(i, carry):
        st_f, st_b = carry
        units = [Unit(2 * i, False), Unit(n_chunks - 1 - 2 * i, True),
                 Unit(2 * i + 1, False), Unit(n_chunks - 2 - 2 * i, True)]
        for u in units:
            u.gates()
        h = C // 2
        while h >= HGRN_BASE:
            for u in units:
                u.level(h)
            h //= 2
        for u in units:
            u.base()
        for u in units:
            u.intra()
        st_f = units[0].carry(st_f)
        st_b = units[1].carry(st_b)
        st_f = units[2].carry(st_f)
        st_b = units[3].carry(st_b)
        return st_f, st_b

    assert n_chunks % 2 == 0
    finals = lax.fori_loop(0, n_chunks // 2, scan_body, (s0_ref[0], s0_ref[1]))

    ones_bd = _segment_ones(LANES, A_DK)

    def norm_body(n, carry):
        rows = pl.ds(pl.multiple_of(n * C, C), C)
        tot = fwd_ref[rows, :] + bwd_ref[rows, :]
        zg = zh_ref[rows, 4 * W:5 * W]
        cols = []
        for c in range(W // LANES):
            lanes = slice(c * LANES, (c + 1) * LANES)
            tc = tot[:, lanes]
            ms = _segment_sum(tc * tc, ones_bd) * (1.0 / A_DK)
            cols.append(tc * lax.rsqrt(ms + 1e-6) * ng_ref[:, lanes])
        on = jnp.concatenate(cols, axis=1)
        o_ref[rows, :] = (on * (zg * _sigmoid(zg))).astype(o_ref.dtype)
        return carry

    lax.fori_loop(0, n_chunks, norm_body, 0)

    for d, st in enumerate(finals):
        s_kv = st.T
        for h in range(A_HEADS):
            blk = slice(h * A_DK, (h + 1) * A_DK)
            sfin_ref[d, h] = s_kv[blk, blk]


def _hgrn(zh, s0, lb_logits, norm_g, layer, seq):
    rows = zh.shape[0]
    n_seq = rows // seq
    per_seq_state = s0.shape[0] != 1
    W = A_WIDTH
    return pl.pallas_call(
        functools.partial(_hgrn_kernel, layer=layer, seq=seq),
        grid=(n_seq,),
        in_specs=[
            pl.BlockSpec((seq, HG_WIDTH), lambda b: (b, 0)),
            pl.BlockSpec((None, 2, W, W), (lambda b: (b, 0, 0, 0)) if per_seq_state else (lambda b: (0, 0, 0, 0))),
            _resident((DEPTH, W), lambda b: (0, 0)),
            _resident((None, 1, W), lambda b: (layer, 0, 0)),
        ],
        out_specs=[
            pl.BlockSpec((seq, W), lambda b: (b, 0)),
            pl.BlockSpec((None, 2, A_HEADS, A_DK, A_DK), lambda b: (b, 0, 0, 0, 0)),
        ],
        out_shape=[
            jax.ShapeDtypeStruct((rows, W), BF16),
            jax.ShapeDtypeStruct((n_seq, 2, A_HEADS, A_DK, A_DK), F32),
        ],
        scratch_shapes=[pltpu.VMEM((seq, W), F32), pltpu.VMEM((seq, W), F32)],
        compiler_params=_params(("parallel",), 48),
        name="hgrn2",
    )(zh, s0, lb_logits, norm_g)


def _conv_kernel(up_ref, uc_ref, un_ref, w_ref, b_ref, g_ref, beta_ref, o_ref, sh_ref, *, tiles_per_seq):
    i = pl.program_id(0)
    T, H, S = CONV_TILE, CONV_HALO, SUBLANES
    P = T + 2 * H
    first = (i % tiles_per_seq) == 0
    last = (i % tiles_per_seq) == tiles_per_seq - 1
    sh_ref[0, 0:H, :] = jnp.where(first, 0.0, up_ref[...])
    sh_ref[0, H:H + T, :] = uc_ref[...]
    sh_ref[0, H + T:P, :] = jnp.where(last, 0.0, un_ref[...])
    for b in range(1, S):
        sh_ref[b, 0:P - S, :] = sh_ref[0, b:b + P - S, :]
    acc = jnp.zeros((T, B_WIDTH), F32)
    for j in range(CONV_K):
        start = H - CONV_PAD + j
        a, b = start // S, start % S
        acc = acc + w_ref[j:j + 1, :] * sh_ref[b, a * S:a * S + T, :]
    y = _layer_norm(acc + b_ref[...], g_ref[...], beta_ref[...])
    o_ref[...] = (y * _sigmoid(y)).astype(o_ref.dtype)


def _conv(u, conv_w, conv_b, ln_g, ln_b, layer, seq):
    rows = u.shape[0]
    T, H = CONV_TILE, CONV_HALO
    tiles_per_seq = seq // T
    ratio = T // H
    n_halo_blocks = rows // H
    return pl.pallas_call(
        functools.partial(_conv_kernel, tiles_per_seq=tiles_per_seq),
        grid=(rows // T,),
        in_specs=[
            pl.BlockSpec((H, B_WIDTH), lambda i: (jnp.maximum(i * ratio - 1, 0), 0)),
            pl.BlockSpec((T, B_WIDTH), lambda i: (i, 0)),
            pl.BlockSpec((H, B_WIDTH), lambda i: (jnp.minimum((i + 1) * ratio, n_halo_blocks - 1), 0)),
            _resident((None, CONV_K, B_WIDTH), lambda i: (layer, 0, 0)),
            _resident((None, 1, B_WIDTH), lambda i: (layer, 0, 0)),
            _resident((None, 1, B_WIDTH), lambda i: (layer, 0, 0)),
            _resident((None, 1, B_WIDTH), lambda i: (layer, 0, 0)),
        ],
        out_specs=pl.BlockSpec((T, B_WIDTH), lambda i: (i, 0)),
        out_shape=jax.ShapeDtypeStruct((rows, B_WIDTH), BF16),
        scratch_shapes=[pltpu.VMEM((SUBLANES, T + 2 * H, B_WIDTH), F32)],
        compiler_params=_params(("parallel",), 16),
        name="conv_module",
    )(u, u, u, conv_w, conv_b, ln_g, ln_b)


def _attn_kernel(q_ref, k_ref, v_ref, o_ref, kd_ref, va_ref, *, n_seq):
    tq = q_ref.shape[0] // n_seq
    seq_k = k_ref.shape[0] // n_seq
    kb = min(seq_k, ATTN_KEY_BLOCK)
    group = C_HEADS // C_KV_HEADS
    half = lax.broadcasted_iota(jnp.int32, (1, LANES), 1) // HEAD_DIM

    @pl.when(pl.program_id(1) == 0)
    def _():
        k = k_ref[...]
        k_swapped = pltpu.roll(k, HEAD_DIM, 1)
        v = v_ref[...]
        for g in range(C_KV_HEADS):
            kd_ref[g] = jnp.where(half == g, k, k_swapped).astype(BF16)
            va_ref[g] = jnp.where(half == g, v, 1.0).astype(BF16)

    n_blocks = seq_k // kb
    chains = [(sq, g) for sq in range(n_seq) for g in range(C_KV_HEADS)]
    units = [(c, b) for b in range(n_blocks) for c in range(len(chains))]
    q_stacked, run_max, acc = {}, {}, {}
    scores, probs, rescale = {}, {}, {}

    def stage_scores(u):
        c, b = units[u]
        sq, g = chains[c]
        if c not in q_stacked:
            heads = []
            for j in range(group):
                hq = g * group + j
                col = q_ref[sq * tq:(sq + 1) * tq, (hq // 2) * LANES:(hq // 2 + 1) * LANES]
                heads.append(jnp.where(half == hq % 2, col, jnp.zeros_like(col)))
            q_stacked[c] = jnp.concatenate(heads, axis=0)
        keys = slice(sq * seq_k + b * kb, sq * seq_k + (b + 1) * kb)
        scores[u] = _dot_nt(q_stacked[c], kd_ref[g, keys, :])

    def stage_exp(u):
        c, b = units[u]
        s = scores.pop(u)
        m_blk = jnp.max(s, axis=-1, keepdims=True)
        m_new = m_blk if b == 0 else jnp.maximum(run_max[c], m_blk)
        probs[u] = jnp.exp2(s - m_new).astype(BF16)
        if b > 0:
            rescale[u] = jnp.exp2(run_max[c] - m_new)
        run_max[c] = m_new

    def stage_values(u):
        c, b = units[u]
        sq, g = chains[c]
        keys = slice(sq * seq_k + b * kb, sq * seq_k + (b + 1) * kb)
        pv = _dot(probs.pop(u), va_ref[g, keys, :])
        acc[c] = pv if b == 0 else rescale.pop(u) * acc[c] + pv
        if b == n_blocks - 1:
            total = acc.pop(c)
            o = total / pltpu.roll(total, HEAD_DIM, 1)
            for j in range(group):
                hq = g * group + j
                o_ref[sq * tq:(sq + 1) * tq, hq * HEAD_DIM:(hq + 1) * HEAD_DIM] = (
                    o[j * tq:(j + 1) * tq, g * HEAD_DIM:(g + 1) * HEAD_DIM].astype(o_ref.dtype))

    for t in range(len(units) + 2):
        if t < len(units):
            stage_scores(t)
        if 0 <= t - 1 < len(units):
            stage_exp(t - 1)
        if 0 <= t - 2 < len(units):
            stage_values(t - 2)


def _attention(q, k, v, seq_q, seq_k, tq, seqs_per_step=1):
    n_seq = q.shape[0] // seq_q
    q_tiles = seq_q // tq
    assert seqs_per_step == 1 or q_tiles == 1
    q_rows, k_rows = seqs_per_step * tq, seqs_per_step * seq_k
    return pl.pallas_call(
        functools.partial(_attn_kernel, n_seq=seqs_per_step),
        grid=(n_seq // seqs_per_step, q_tiles),
        in_specs=[
            pl.BlockSpec((q_rows, C_WIDTH), lambda b, i: (b * q_tiles + i, 0)),
            pl.BlockSpec((k_rows, KV_WIDTH), lambda b, i: (b, 0)),
            pl.BlockSpec((k_rows, KV_WIDTH), lambda b, i: (b, 0)),
        ],
        out_specs=pl.BlockSpec((q_rows, C_WIDTH), lambda b, i: (b * q_tiles + i, 0)),
        out_shape=jax.ShapeDtypeStruct(q.shape, BF16),
        scratch_shapes=[pltpu.VMEM((C_KV_HEADS, k_rows, LANES), BF16),
                        pltpu.VMEM((C_KV_HEADS, k_rows, LANES), BF16)],
        compiler_params=_params(("parallel", "arbitrary"), 48),
        name="attention",
    )(q, k, v)


def _rope_tables(seq):
    pos = jnp.arange(seq)
    row_id = (pos // GRID_W).astype(F32)
    col_id = (pos % GRID_W).astype(F32)
    inv = ROPE_THETA ** (-jnp.arange(ROPE_PAIRS, dtype=F32) / ROPE_PAIRS)
    lane = jnp.arange(LANES) % HEAD_DIM
    use_col = (lane // (2 * ROPE_PAIRS)) == 1
    first = (lane % (2 * ROPE_PAIRS)) < ROPE_PAIRS
    freq = inv[lane % ROPE_PAIRS]
    ang = jnp.where(use_col[None, :], col_id[:, None], row_id[:, None]) * freq[None, :]
    cos, sin = jnp.cos(ang), jnp.sin(ang)
    return cos, jnp.where(first[None, :], -sin, 0.0), jnp.where(first[None, :], 0.0, sin)


def _state_to_block_diag(s):
    eye = jnp.eye(A_HEADS, dtype=s.dtype)
    st = jnp.swapaxes(s, -1, -2)
    bd = st[..., :, :, None, :] * eye[:, None, :, None]
    return bd.reshape(s.shape[:-3] + (A_WIDTH, A_WIDTH))


def kernel(x_prompt, x_sample, cache_k, cache_v, state_hgrn, c, c_ctx, w_mod, b_mod, ln_g, ln_b, ffn1_w_in, ffn1_w_out, ffn2_w_in, ffn2_w_out, w_in, w_out, hgrn_lb_logits, hgrn_norm_g, conv_w, conv_b, conv_ln_g, conv_ln_b, q_norm_g, k_norm_g):
    batch, seq, _ = x_prompt.shape
    dec_batch, dec_seq, _ = x_sample.shape
    past = cache_k.shape[2]
    assert seq % ROW_TILE == 0 or ROW_TILE % seq == 0
    assert dec_seq % ROW_TILE == 0 and dec_batch + 1 <= SUBLANES
    assert KV_WIDTH == LANES and C_HEADS % (2 * C_KV_HEADS) == 0

    bf = lambda w: w.astype(BF16)
    ffn1_w_in, ffn1_w_out, ffn2_w_in, ffn2_w_out, w_in, w_out = map(
        bf, (ffn1_w_in, ffn1_w_out, ffn2_w_in, ffn2_w_out, w_in, w_out))

    conds = jnp.zeros((SUBLANES, D_MODEL), F32).at[0].set(c_ctx).at[1:1 + dec_batch].set(c)
    mod = _modulation(conds, w_mod, b_mod).reshape(DEPTH, SUBLANES, N_MOD, D_MODEL)

    ln_g3 = ln_g.reshape(DEPTH * 3, 1, D_MODEL)
    ln_b3 = ln_b.reshape(DEPTH * 3, 1, D_MODEL)
    qg = jnp.tile(q_norm_g, (1, C_HEADS)).reshape(DEPTH, 1, C_WIDTH)
    kg = jnp.tile(k_norm_g, (1, C_KV_HEADS)).reshape(DEPTH, 1, KV_WIDTH)
    norm_g = hgrn_norm_g.reshape(DEPTH, 1, A_WIDTH)
    conv_b3 = conv_b.reshape(DEPTH, 1, B_WIDTH)
    conv_g3 = conv_ln_g.reshape(DEPTH, 1, B_WIDTH)
    conv_beta3 = conv_ln_b.reshape(DEPTH, 1, B_WIDTH)
    tables = _rope_tables(dec_seq)
    zero_state = jnp.zeros((1, 2, A_WIDTH, A_WIDTH), F32)
    lat_state = _state_to_block_diag(state_hgrn)

    blocks_per_lat = dec_seq // ROW_TILE
    ctx_cond = lambda i: 0
    lat_cond = lambda i: 1 + i // blocks_per_lat

    xp = x_prompt.reshape(batch * seq, D_MODEL)
    xs = x_sample.reshape(dec_batch * dec_seq, D_MODEL)
    ks, vs, ss = [], [], []
    for l in range(DEPTH):
        mod_l = mod[l]

        def layer_fn(x, cond, sq, rope_tables, s0, k_past, v_past):
            x, zh, u, q, k, v = _ffn(x, mod_l, cond, ffn1_w_in, ffn1_w_out, ln_g3, ln_b3, l, 0,
                                     mixer_in=(w_in, qg, kg, rope_tables, sq))
            o_a, s_fin = _hgrn(zh, s0, hgrn_lb_logits, norm_g, l, sq)
            o_b = _conv(u, conv_w, conv_b3, conv_g3, conv_beta3, l, sq)
            if k_past is None:
                o_c = _attention(q, k, v, sq, sq, sq, seqs_per_step=ATTN_SHORT_SEQS_PER_STEP)
            else:
                n = x.shape[0] // sq
                k_all = jnp.concatenate([k_past, k.reshape(n, sq, KV_WIDTH)], axis=1)
                v_all = jnp.concatenate([v_past, v.reshape(n, sq, KV_WIDTH)], axis=1)
                sk = k_all.shape[1]
                o_c = _attention(q, k_all.reshape(n * sk, KV_WIDTH), v_all.reshape(n * sk, KV_WIDTH),
                                 sq, sk, ATTN_Q_TILE_LONG)
            x, = _ffn(x, mod_l, cond, ffn2_w_in, ffn2_w_out, ln_g3, ln_b3, l, 2, mixer_out=(o_a, o_b, o_c, w_out))
            return x, k, v, s_fin

        xp, k_l, v_l, s_l = layer_fn(xp, ctx_cond, seq, None, zero_state, None, None)
        ks.append(k_l.reshape(batch, seq, C_KV_HEADS, HEAD_DIM))
        vs.append(v_l.reshape(batch, seq, C_KV_HEADS, HEAD_DIM))
        ss.append(s_l)
        xs, _, _, _ = layer_fn(xs, lat_cond, dec_seq, tables, lat_state[:, l],
                               cache_k[:, l].reshape(dec_batch, past, KV_WIDTH),
                               cache_v[:, l].reshape(dec_batch, past, KV_WIDTH))

    return (xp.reshape(batch, seq, D_MODEL), xs.reshape(dec_batch, dec_seq, D_MODEL),
            jnp.stack(ks, axis=1), jnp.stack(vs, axis=1), jnp.stack(ss, axis=1))
```

```python
import functools

import jax
import jax.numpy as jnp
from jax import lax
from jax.experimental import pallas as pl
from jax.experimental.pallas import tpu as pltpu

F32 = jnp.float32
BF16 = jnp.bfloat16

D_MODEL = 1024
DEPTH = 2
GRID_W = 64
HEAD_DIM = 64
A_HEADS = 4
A_DK = 64
A_WIDTH = 256
B_WIDTH = 256
CONV_K = 31
CONV_PAD = 15
C_HEADS = 8
C_KV_HEADS = 2
C_WIDTH = 512
KV_WIDTH = C_KV_HEADS * HEAD_DIM
D_FF = 2816
ROPE_THETA = 10000.0
ROPE_PAIRS = 16
N_MOD = 9
ALPHA = (2 * DEPTH) ** 0.25
F_MIN = 1e-6
LOG2_E = 1.4426950408889634
IN_WIDTH = 5 * A_WIDTH + 2 * B_WIDTH + C_WIDTH + 2 * KV_WIDTH
HG_WIDTH = 5 * A_WIDTH

LANES = 128
SUBLANES = 8
VMEM_BYTES_V7X = 64 * 1024 * 1024

ROW_TILE = 512
FFN_CHUNK = 256
HGRN_CHUNK = 128
HGRN_BASE = 8
CONV_TILE = 256
CONV_HALO = 16
ATTN_Q_TILE_LONG = 256
ATTN_KEY_BLOCK = 512
ATTN_SHORT_SEQS_PER_STEP = 4


def _params(semantics, vmem_mb):
    return pltpu.CompilerParams(dimension_semantics=semantics,
                                vmem_limit_bytes=min(vmem_mb * 1024 * 1024, VMEM_BYTES_V7X - (4 << 20)))


def _resident(block_shape, index_map):
    return pl.BlockSpec(block_shape, index_map, pipeline_mode=pl.Buffered(1))


def _sigmoid(x):
    return 1.0 / (1.0 + jnp.exp(-x))


def _layer_norm(y, g, b, eps=1e-5):
    mu = jnp.mean(y, axis=-1, keepdims=True)
    d = y - mu
    var = jnp.mean(d * d, axis=-1, keepdims=True)
    return d * lax.rsqrt(var + eps) * g + b


def _dot(a, b):
    return jnp.dot(a, b, preferred_element_type=F32)


def _dot_nt(a, b):
    return lax.dot_general(a, b, (((1,), (1,)), ((), ())), preferred_element_type=F32)


def _dot_tn(a, b):
    return lax.dot_general(a, b, (((0,), (0,)), ((), ())), preferred_element_type=F32)


def _segment_ones(width, seg):
    r = lax.broadcasted_iota(jnp.int32, (width, width), 0) // seg
    c = lax.broadcasted_iota(jnp.int32, (width, width), 1) // seg
    return (r == c).astype(BF16)


def _segment_sum(x, ones_bd):
    hi = x.astype(BF16)
    lo = (x - hi.astype(F32)).astype(BF16)
    return _dot(hi, ones_bd) + _dot(lo, ones_bd)


def _head_rms_norm(x, gain, eps=1e-6):
    ones_bd = _segment_ones(LANES, HEAD_DIM)
    cols = []
    for c in range(x.shape[1] // LANES):
        xc = x[:, c * LANES:(c + 1) * LANES]
        ms = _segment_sum(xc * xc, ones_bd) * (1.0 / HEAD_DIM)
        cols.append(xc * lax.rsqrt(ms + eps) * gain[:, c * LANES:(c + 1) * LANES])
    return cols[0] if len(cols) == 1 else jnp.concatenate(cols, axis=1)


def _mod_kernel(c_ref, w_ref, b_ref, o_ref):
    c = c_ref[...]
    s = (c * _sigmoid(c)).astype(BF16)
    o_ref[...] = _dot(s, w_ref[...].astype(BF16)) + b_ref[...]


def _modulation(conds, w_mod, b_mod):
    tn = D_MODEL
    return pl.pallas_call(
        _mod_kernel,
        grid=(DEPTH, N_MOD * D_MODEL // tn),
        in_specs=[
            pl.BlockSpec((SUBLANES, D_MODEL), lambda l, j: (0, 0)),
            pl.BlockSpec((None, D_MODEL, tn), lambda l, j: (l, 0, j)),
            pl.BlockSpec((None, 1, tn), lambda l, j: (l, 0, j)),
        ],
        out_specs=pl.BlockSpec((None, SUBLANES, tn), lambda l, j: (l, 0, j)),
        out_shape=jax.ShapeDtypeStruct((DEPTH, SUBLANES, N_MOD * D_MODEL), F32),
        compiler_params=_params(("arbitrary", "arbitrary"), 32),
        name="modulation",
    )(conds, w_mod, b_mod.reshape(DEPTH, 1, N_MOD * D_MODEL))


def _mod_spec(cond_of_block):
    return pl.BlockSpec((None, N_MOD, D_MODEL), lambda i: (cond_of_block(i), 0, 0))


def _ffn_kernel(*refs, sub, with_mixer_out, mixer_in_rope):
    refs = list(refs)
    x_ref = refs.pop(0)
    if with_mixer_out:
        oa_ref, ob_ref, oc_ref = refs[:3]
        del refs[:3]
    mod_ref = refs.pop(0)
    if with_mixer_out:
        wmix_ref, gmix_ref, bmix_ref = refs[:3]
        del refs[:3]
        y = _dot(oa_ref[...], wmix_ref[0:A_WIDTH, :])
        y = y + _dot(ob_ref[...], wmix_ref[A_WIDTH:A_WIDTH + B_WIDTH, :])
        y = y + _dot(oc_ref[...], wmix_ref[A_WIDTH + B_WIDTH:, :])
        x = _layer_norm(ALPHA * x_ref[...] + mod_ref[5:6, :] * y, gmix_ref[...], bmix_ref[...])
    else:
        x = x_ref[...]
    wg_ref, wu_ref, wo_ref, g_ref, b_ref = refs[:5]
    del refs[:5]
    if mixer_in_rope is not None:
        n_in = 6 if mixer_in_rope else 3
        mixer_in_refs, refs = refs[:n_in], refs[n_in:]
    o_ref = refs.pop(0)
    shift = mod_ref[3 * sub:3 * sub + 1, :]
    scale = mod_ref[3 * sub + 1:3 * sub + 2, :]
    gate = mod_ref[3 * sub + 2:3 * sub + 3, :]
    h = (x * (1.0 + scale) + shift).astype(BF16)
    acc = jnp.zeros(x.shape, F32)
    for j in range(D_FF // FFN_CHUNK):
        cols = slice(j * FFN_CHUNK, (j + 1) * FFN_CHUNK)
        gt = _dot(h, wg_ref[:, cols])
        up = _dot(h, wu_ref[:, cols])
        act = (gt * _sigmoid(gt) * up).astype(BF16)
        acc = acc + _dot(act, wo_ref[cols, :])
    y = ALPHA * x + 0.5 * gate * acc
    x_new = _layer_norm(y, g_ref[...], b_ref[...])
    o_ref[...] = x_new
    if mixer_in_rope is not None:
        _mixer_in(x_new, mod_ref, *mixer_in_refs, *refs)


def _ffn(x, mod_l, cond_of_block, w_in, w_out, ln_g, ln_b, layer, sub, mixer_out=None, mixer_in=None):
    rows = x.shape[0]
    row_spec = lambda width: pl.BlockSpec((ROW_TILE, width), lambda i: (i, 0))
    ln_spec = lambda idx: _resident((None, 1, D_MODEL), lambda i: (layer * 3 + idx, 0, 0))
    in_specs, args = [row_spec(D_MODEL)], [x]
    if mixer_out is not None:
        o_a, o_b, o_c, w_mix = mixer_out
        in_specs += [row_spec(A_WIDTH), row_spec(B_WIDTH), row_spec(C_WIDTH)]
        args += [o_a, o_b, o_c]
    in_specs.append(_mod_spec(cond_of_block))
    args.append(mod_l)
    if mixer_out is not None:
        in_specs += [_resident((None, D_MODEL, D_MODEL), lambda i: (layer, 0, 0)), ln_spec(1), ln_spec(1)]
        args += [w_mix, ln_g, ln_b]
    in_specs += [
        _resident((None, D_MODEL, D_FF), lambda i: (layer, 0, 0)),
        _resident((None, D_MODEL, D_FF), lambda i: (layer, 0, 1)),
        _resident((None, D_FF, D_MODEL), lambda i: (layer, 0, 0)),
        ln_spec(sub), ln_spec(sub),
    ]
    args += [w_in, w_in, w_out, ln_g, ln_b]
    out_specs = [row_spec(D_MODEL)]
    out_shape = [jax.ShapeDtypeStruct((rows, D_MODEL), F32)]
    rope = None
    if mixer_in is not None:
        w_mix_in, qg, kg, rope_tables, seq = mixer_in
        rope = rope_tables is not None
        in_specs += [
            _resident((None, D_MODEL, IN_WIDTH), lambda i: (layer, 0, 0)),
            _resident((None, 1, C_WIDTH), lambda i: (layer, 0, 0)),
            _resident((None, 1, KV_WIDTH), lambda i: (layer, 0, 0)),
        ]
        args += [w_mix_in, qg, kg]
        if rope:
            per_seq = seq // ROW_TILE
            in_specs += [pl.BlockSpec((ROW_TILE, LANES), lambda i: (i % per_seq, 0))] * 3
            args += list(rope_tables)
        widths = ((HG_WIDTH, F32), (B_WIDTH, F32), (C_WIDTH, BF16), (KV_WIDTH, F32), (KV_WIDTH, F32))
        out_specs += [row_spec(w) for w, _ in widths]
        out_shape += [jax.ShapeDtypeStruct((rows, w), dt) for w, dt in widths]
    return pl.pallas_call(
        functools.partial(_ffn_kernel, sub=sub, with_mixer_out=mixer_out is not None, mixer_in_rope=rope),
        grid=(rows // ROW_TILE,),
        in_specs=in_specs,
        out_specs=out_specs,
        out_shape=out_shape,
        compiler_params=_params(("parallel",), 56),
        name=f"ffn{sub // 2 + 1}",
    )(*args)


def _mixer_in(x, mod_ref, w_ref, qg_ref, kg_ref, *refs):
    rope = len(refs) == 8
    if rope:
        cos_ref, s1_ref, s2_ref = refs[:3]
    zh_ref, u_ref, q_ref, k_ref, v_ref = refs[-5:]
    h = (x * (1.0 + mod_ref[4:5, :]) + mod_ref[3:4, :]).astype(BF16)

    zh_ref[...] = _dot(h, w_ref[:, 0:HG_WIDTH])
    o = HG_WIDTH
    glu_a = _dot(h, w_ref[:, o:o + B_WIDTH])
    glu_b = _dot(h, w_ref[:, o + B_WIDTH:o + 2 * B_WIDTH])
    u_ref[...] = glu_a * _sigmoid(glu_b)
    o += 2 * B_WIDTH
    cq = _dot(h, w_ref[:, o:o + C_WIDTH])
    ck = _dot(h, w_ref[:, o + C_WIDTH:o + C_WIDTH + KV_WIDTH])
    v_ref[...] = _dot(h, w_ref[:, o + C_WIDTH + KV_WIDTH:o + C_WIDTH + 2 * KV_WIDTH])

    qn = _head_rms_norm(cq, qg_ref[...])
    kn = _head_rms_norm(ck, kg_ref[...])
    if rope:
        cos, s1, s2 = cos_ref[...], s1_ref[...], s2_ref[...]

        def rot(t):
            cols = []
            for c in range(t.shape[1] // LANES):
                tc = t[:, c * LANES:(c + 1) * LANES]
                cols.append(tc * cos + pltpu.roll(tc, LANES - ROPE_PAIRS, 1) * s1
                            + pltpu.roll(tc, ROPE_PAIRS, 1) * s2)
            return cols[0] if len(cols) == 1 else jnp.concatenate(cols, axis=1)

        qn, kn = rot(qn), rot(kn)
    q_ref[...] = (qn * (HEAD_DIM ** -0.5 * LOG2_E)).astype(BF16)
    k_ref[...] = kn


def _cumsum_rows(x, reverse):
    n = x.shape[0]
    row = lax.broadcasted_iota(jnp.int32, x.shape, 0)
    s = 1
    while s < n:
        if reverse:
            x = x + jnp.where(row < n - s, pltpu.roll(x, n - s, 0), 0.0)
        else:
            x = x + jnp.where(row >= s, pltpu.roll(x, s, 0), 0.0)
        s *= 2
    return x


def _hgrn_kernel(zh_ref, s0_ref, lbl_ref, ng_ref, o_ref, sfin_ref, fwd_ref, bwd_ref, *, layer, seq):
    C, W = HGRN_CHUNK, A_WIDTH
    n_chunks = seq // C

    lg = lbl_ref[...]
    e = jnp.exp(lg - jnp.max(lg, axis=0, keepdims=True))
    soft = e / jnp.sum(e, axis=0, keepdims=True)
    lb = jnp.zeros((1, W), F32)
    for j in range(1, layer + 1):
        lb = lb + soft[j:j + 1, :]

    lane_head = lax.broadcasted_iota(jnp.int32, (1, W), 1) // A_DK
    head_masks = [lane_head == h for h in range(A_HEADS)]
    t_idx = lax.broadcasted_iota(jnp.int32, (C, A_HEADS * C), 0)
    s_idx = lax.broadcasted_iota(jnp.int32, (C, A_HEADS * C), 1) % C
    pair_fwd = jnp.where(s_idx <= t_idx, t_idx ^ s_idx, 2 * C)
    pair_bwd = jnp.where(s_idx >= t_idx, t_idx ^ s_idx, 2 * C)
    bd_mask = (lax.broadcasted_iota(jnp.int32, (W, W), 0) // A_DK
               == lax.broadcasted_iota(jnp.int32, (W, W), 1) // A_DK)
    def stack_heads(t_bf):
        return jnp.concatenate([jnp.where(m, t_bf, jnp.zeros_like(t_bf)) for m in head_masks], axis=0)

    class Unit:
        def __init__(self, n, reverse):
            self.reverse = reverse
            self.rows = pl.ds(pl.multiple_of(n * C, C), C)
            self.pair = pair_bwd if reverse else pair_fwd
            self.scores = jnp.zeros((C, A_HEADS * C), F32)

        def gates(self):
            rows, reverse = self.rows, self.reverse
            self.q = zh_ref[rows, 0:W]
            self.v_bf = zh_ref[rows, W:2 * W].astype(BF16)
            zf = zh_ref[rows, (3 if reverse else 2) * W:(4 if reverse else 3) * W]
            f = lb + (1.0 - lb) * _sigmoid(zf)
            self.kk = 1.0 - f
            self.a = _cumsum_rows(jnp.log(jnp.maximum(f, F_MIN)), reverse) * LOG2_E

        def level(self, h):
            a3 = self.a.reshape(C // h, h, W)
            zero = jnp.zeros((1, 1, W), F32)
            if self.reverse:
                own = a3[:, 0:1, :]
                other = jnp.concatenate([own[1:], zero], axis=0)
            else:
                own = a3[:, h - 1:h, :]
                other = jnp.concatenate([zero, own[:-1]], axis=0)
            qt = (self.q.reshape(C // h, h, W) * jnp.exp2(a3 - other)).reshape(C, W).astype(BF16)
            kt = (self.kk.reshape(C // h, h, W) * jnp.exp2(own - a3)).reshape(C, W).astype(BF16)
            self.scores = jnp.where(self.pair < 2 * h, _dot_nt(qt, stack_heads(kt)), self.scores)

        def base(self):
            hb = HGRN_BASE
            a3 = self.a.reshape(C // hb, hb, W)
            ref_row = a3[:, hb // 2:hb // 2 + 1, :] if self.reverse else a3[:, hb // 2 - 1:hb // 2, :]
            qt = (self.q.reshape(C // hb, hb, W) * jnp.exp2(a3 - ref_row)).reshape(C, W).astype(BF16)
            kt = (self.kk.reshape(C // hb, hb, W) * jnp.exp2(ref_row - a3)).reshape(C, W).astype(BF16)
            self.scores = jnp.where(self.pair < hb, _dot_nt(qt, stack_heads(kt)), self.scores)

        def intra(self):
            self.o = _dot(self.scores.astype(BF16), stack_heads(self.v_bf))
            self.qe = (self.q * jnp.exp2(self.a)).astype(BF16)
            self.a_last = self.a[0:1, :] if self.reverse else self.a[C - 1:C, :]
            self.kv = jnp.where(bd_mask, _dot_tn(self.v_bf, (self.kk * jnp.exp2(self.a_last - self.a)).astype(BF16)),
                                0.0)

        def carry(self, st):
            out_ref = bwd_ref if self.reverse else fwd_ref
            out_ref[self.rows, :] = self.o + _dot_nt(self.qe, st.astype(BF16))
            return st * jnp.exp2(self.a_last) + self.kv

        def run(self, st):
            self.gates()
            h = C // 2
            while h >= HGRN_BASE:
                self.level(h)
                h //= 2
            self.base()
            self.intra()
            return self.carry(st)

    def scan_body(i, carry):
        st_f, st_b = carry
        return Unit(i, False).run(st_f), Unit(n_chunks - 1 - i, True).run(st_b)

    finals = lax.fori_loop(0, n_chunks, scan_body, (s0_ref[0], s0_ref[1]), unroll=2)

    ones_bd = _segment_ones(LANES, A_DK)

    def norm_body(n, carry):
        rows = pl.ds(pl.multiple_of(n * C, C), C)
        tot = fwd_ref[rows, :] + bwd_ref[rows, :]
        zg = zh_ref[rows, 4 * W:5 * W]
        cols = []
        for c in range(W // LANES):
            lanes = slice(c * LANES, (c + 1) * LANES)
            tc = tot[:, lanes]
            ms = _segment_sum(tc * tc, ones_bd) * (1.0 / A_DK)
            cols.append(tc * lax.rsqrt(ms + 1e-6) * ng_ref[:, lanes])
        on = jnp.concatenate(cols, axis=1)
        o_ref[rows, :] = (on * (zg * _sigmoid(zg))).astype(o_ref.dtype)
        return carry

    lax.fori_loop(0, n_chunks, norm_body, 0)

    for d, st in enumerate(finals):
        s_kv = st.T
        for h in range(A_HEADS):
            blk = slice(h * A_DK, (h + 1) * A_DK)
            sfin_ref[d, h] = s_kv[blk, blk]


def _hgrn(zh, s0, lb_logits, norm_g, layer, seq):
    rows = zh.shape[0]
    n_seq = rows // seq
    per_seq_state = s0.shape[0] != 1
    W = A_WIDTH
    return pl.pallas_call(
        functools.partial(_hgrn_kernel, layer=layer, seq=seq),
        grid=(n_seq,),
        in_specs=[
            pl.BlockSpec((seq, HG_WIDTH), lambda b: (b, 0)),
            pl.BlockSpec((None, 2, W, W), (lambda b: (b, 0, 0, 0)) if per_seq_state else (lambda b: (0, 0, 0, 0))),
            _resident((DEPTH, W), lambda b: (0, 0)),
            _resident((None, 1, W), lambda b: (layer, 0, 0)),
        ],
        out_specs=[
            pl.BlockSpec((seq, W), lambda b: (b, 0)),
            pl.BlockSpec((None, 2, A_HEADS, A_DK, A_DK), lambda b: (b, 0, 0, 0, 0)),
        ],
        out_shape=[
            jax.ShapeDtypeStruct((rows, W), BF16),
            jax.ShapeDtypeStruct((n_seq, 2, A_HEADS, A_DK, A_DK), F32),
        ],
        scratch_shapes=[pltpu.VMEM((seq, W), F32), pltpu.VMEM((seq, W), F32)],
        compiler_params=_params(("parallel",), 48),
        name="hgrn2",
    )(zh, s0, lb_logits, norm_g)


def _conv_kernel(up_ref, uc_ref, un_ref, w_ref, b_ref, g_ref, beta_ref, o_ref, sh_ref, *, tiles_per_seq):
    i = pl.program_id(0)
    T, H, S = CONV_TILE, CONV_HALO, SUBLANES
    P = T + 2 * H
    first = (i % tiles_per_seq) == 0
    last = (i % tiles_per_seq) == tiles_per_seq - 1
    sh_ref[0, 0:H, :] = jnp.where(first, 0.0, up_ref[...])
    sh_ref[0, H:H + T, :] = uc_ref[...]
    sh_ref[0, H + T:P, :] = jnp.where(last, 0.0, un_ref[...])
    for b in range(1, S):
        sh_ref[b, 0:P - S, :] = sh_ref[0, b:b + P - S, :]
    acc = jnp.zeros((T, B_WIDTH), F32)
    for j in range(CONV_K):
        start = H - CONV_PAD + j
        a, b = start // S, start % S
        acc = acc + w_ref[j:j + 1, :] * sh_ref[b, a * S:a * S + T, :]
    y = _layer_norm(acc + b_ref[...], g_ref[...], beta_ref[...])
    o_ref[...] = (y * _sigmoid(y)).astype(o_ref.dtype)


def _conv(u, conv_w, conv_b, ln_g, ln_b, layer, seq):
    rows = u.shape[0]
    T, H = CONV_TILE, CONV_HALO
    tiles_per_seq = seq // T
    ratio = T // H
    n_halo_blocks = rows // H
    return pl.pallas_call(
        functools.partial(_conv_kernel, tiles_per_seq=tiles_per_seq),
        grid=(rows // T,),
        in_specs=[
            pl.BlockSpec((H, B_WIDTH), lambda i: (jnp.maximum(i * ratio - 1, 0), 0)),
            pl.BlockSpec((T, B_WIDTH), lambda i: (i, 0)),
            pl.BlockSpec((H, B_WIDTH), lambda i: (jnp.minimum((i + 1) * ratio, n_halo_blocks - 1), 0)),
            _resident((None, CONV_K, B_WIDTH), lambda i: (layer, 0, 0)),
            _resident((None, 1, B_WIDTH), lambda i: (layer, 0, 0)),
            _resident((None, 1, B_WIDTH), lambda i: (layer, 0, 0)),
            _resident((None, 1, B_WIDTH), lambda i: (layer, 0, 0)),
        ],
        out_specs=pl.BlockSpec((T, B_WIDTH), lambda i: (i, 0)),
        out_shape=jax.ShapeDtypeStruct((rows, B_WIDTH), BF16),
        scratch_shapes=[pltpu.VMEM((SUBLANES, T + 2 * H, B_WIDTH), F32)],
        compiler_params=_params(("parallel",), 16),
        name="conv_module",
    )(u, u, u, conv_w, conv_b, ln_g, ln_b)


def _attn_kernel(q_ref, k_ref, v_ref, o_ref, kd_ref, va_ref, *, n_seq, single_q_tile):
    tq = q_ref.shape[0] // n_seq
    seq_k = k_ref.shape[0] // n_seq
    kb = min(seq_k, ATTN_KEY_BLOCK)
    group = C_HEADS // C_KV_HEADS
    half = lax.broadcasted_iota(jnp.int32, (1, LANES), 1) // HEAD_DIM

    def prepare():
        k = k_ref[...]
        k_swapped = pltpu.roll(k, HEAD_DIM, 1)
        v = v_ref[...]
        v_swapped = pltpu.roll(v, HEAD_DIM, 1)
        for g in range(C_KV_HEADS):
            kd_ref[g] = jnp.where(half == g, k, k_swapped).astype(BF16)
            va_ref[g, :, 0:LANES] = jnp.where(half == g, v, v_swapped).astype(BF16)
            va_ref[g, :, LANES:2 * LANES] = jnp.ones(v.shape, BF16)

    if single_q_tile:
        prepare()
    else:
        pl.when(pl.program_id(1) == 0)(prepare)

    n_blocks = seq_k // kb
    chains = [(sq, g) for sq in range(n_seq) for g in range(C_KV_HEADS)]
    units = [(c, b) for b in range(n_blocks) for c in range(len(chains))]
    q_stacked, run_max, acc = {}, {}, {}
    scores, probs, rescale = {}, {}, {}

    def stage_scores(u):
        c, b = units[u]
        sq, g = chains[c]
        if c not in q_stacked:
            heads = []
            for j in range(group):
                hq = g * group + j
                col = q_ref[sq * tq:(sq + 1) * tq, (hq // 2) * LANES:(hq // 2 + 1) * LANES]
                heads.append(jnp.where(half == hq % 2, col, jnp.zeros_like(col)))
            q_stacked[c] = jnp.concatenate(heads, axis=0)
        keys = slice(sq * seq_k + b * kb, sq * seq_k + (b + 1) * kb)
        scores[u] = _dot_nt(q_stacked[c], kd_ref[g, keys, :])

    def stage_exp(u):
        c, b = units[u]
        s = scores.pop(u)
        m_blk = jnp.max(s, axis=-1, keepdims=True)
        m_new = m_blk if b == 0 else jnp.maximum(run_max[c], m_blk)
        probs[u] = jnp.exp2(s - m_new).astype(BF16)
        if b > 0:
            rescale[u] = jnp.exp2(run_max[c] - m_new)
        run_max[c] = m_new

    def stage_values(u):
        c, b = units[u]
        sq, g = chains[c]
        keys = slice(sq * seq_k + b * kb, sq * seq_k + (b + 1) * kb)
        pv = _dot(probs.pop(u), va_ref[g, keys, :])
        acc[c] = pv if b == 0 else rescale.pop(u) * acc[c] + pv
        if b == n_blocks - 1:
            total = acc.pop(c)
            o = total[:, 0:LANES] / total[:, LANES:2 * LANES]
            for j in range(group):
                hq = g * group + j
                o_ref[sq * tq:(sq + 1) * tq, hq * HEAD_DIM:(hq + 1) * HEAD_DIM] = (
                    o[j * tq:(j + 1) * tq, (hq % 2) * HEAD_DIM:(hq % 2 + 1) * HEAD_DIM].astype(o_ref.dtype))

    for t in range(len(units) + 2):
        if t < len(units):
            stage_scores(t)
        if 0 <= t - 1 < len(units):
            stage_exp(t - 1)
        if 0 <= t - 2 < len(units):
            stage_values(t - 2)


def _attention(q, k, v, seq_q, seq_k, tq, seqs_per_step=1):
    n_seq = q.shape[0] // seq_q
    q_tiles = seq_q // tq
    assert seqs_per_step == 1 or q_tiles == 1
    q_rows, k_rows = seqs_per_step * tq, seqs_per_step * seq_k
    return pl.pallas_call(
        functools.partial(_attn_kernel, n_seq=seqs_per_step, single_q_tile=q_tiles == 1),
        grid=(n_seq // seqs_per_step, q_tiles),
        in_specs=[
            pl.BlockSpec((q_rows, C_WIDTH), lambda b, i: (b * q_tiles + i, 0)),
            pl.BlockSpec((k_rows, KV_WIDTH), lambda b, i: (b, 0)),
            pl.BlockSpec((k_rows, KV_WIDTH), lambda b, i: (b, 0)),
        ],
        out_specs=pl.BlockSpec((q_rows, C_WIDTH), lambda b, i: (b * q_tiles + i, 0)),
        out_shape=jax.ShapeDtypeStruct(q.shape, BF16),
        scratch_shapes=[pltpu.VMEM((C_KV_HEADS, k_rows, LANES), BF16),
                        pltpu.VMEM((C_KV_HEADS, k_rows, 2 * LANES), BF16)],
        compiler_params=_params(("parallel", "arbitrary"), 48),
        name="attention",
    )(q, k, v)


def _rope_tables(seq):
    pos = jnp.arange(seq)
    row_id = (pos // GRID_W).astype(F32)
    col_id = (pos % GRID_W).astype(F32)
    inv = ROPE_THETA ** (-jnp.arange(ROPE_PAIRS, dtype=F32) / ROPE_PAIRS)
    lane = jnp.arange(LANES) % HEAD_DIM
    use_col = (lane // (2 * ROPE_PAIRS)) == 1
    first = (lane % (2 * ROPE_PAIRS)) < ROPE_PAIRS
    freq = inv[lane % ROPE_PAIRS]
    ang = jnp.where(use_col[None, :], col_id[:, None], row_id[:, None]) * freq[None, :]
    cos, sin = jnp.cos(ang), jnp.sin(ang)
    return cos, jnp.where(first[None, :], -sin, 0.0), jnp.where(first[None, :], 0.0, sin)


def _state_to_block_diag(s):
    eye = jnp.eye(A_HEADS, dtype=s.dtype)
    st = jnp.swapaxes(s, -1, -2)
    bd = st[..., :, :, None, :] * eye[:, None, :, None]
    return bd.reshape(s.shape[:-3] + (A_WIDTH, A_WIDTH))


def kernel(x_prompt, x_sample, cache_k, cache_v, state_hgrn, c, c_ctx, w_mod, b_mod, ln_g, ln_b, ffn1_w_in, ffn1_w_out, ffn2_w_in, ffn2_w_out, w_in, w_out, hgrn_lb_logits, hgrn_norm_g, conv_w, conv_b, conv_ln_g, conv_ln_b, q_norm_g, k_norm_g):
    batch, seq, _ = x_prompt.shape
    dec_batch, dec_seq, _ = x_sample.shape
    past = cache_k.shape[2]
    assert seq % ROW_TILE == 0 or ROW_TILE % seq == 0
    assert dec_seq % ROW_TILE == 0 and dec_batch + 1 <= SUBLANES
    assert KV_WIDTH == LANES and C_HEADS % (2 * C_KV_HEADS) == 0

    bf = lambda w: w.astype(BF16)
    ffn1_w_in, ffn1_w_out, ffn2_w_in, ffn2_w_out, w_in, w_out = map(
        bf, (ffn1_w_in, ffn1_w_out, ffn2_w_in, ffn2_w_out, w_in, w_out))

    conds = jnp.zeros((SUBLANES, D_MODEL), F32).at[0].set(c_ctx).at[1:1 + dec_batch].set(c)
    mod = _modulation(conds, w_mod, b_mod).reshape(DEPTH, SUBLANES, N_MOD, D_MODEL)

    ln_g3 = ln_g.reshape(DEPTH * 3, 1, D_MODEL)
    ln_b3 = ln_b.reshape(DEPTH * 3, 1, D_MODEL)
    qg = jnp.tile(q_norm_g, (1, C_HEADS)).reshape(DEPTH, 1, C_WIDTH)
    kg = jnp.tile(k_norm_g, (1, C_KV_HEADS)).reshape(DEPTH, 1, KV_WIDTH)
    norm_g = hgrn_norm_g.reshape(DEPTH, 1, A_WIDTH)
    conv_b3 = conv_b.reshape(DEPTH, 1, B_WIDTH)
    conv_g3 = conv_ln_g.reshape(DEPTH, 1, B_WIDTH)
    conv_beta3 = conv_ln_b.reshape(DEPTH, 1, B_WIDTH)
    tables = _rope_tables(dec_seq)
    zero_state = jnp.zeros((1, 2, A_WIDTH, A_WIDTH), F32)
    lat_state = _state_to_block_diag(state_hgrn)

    blocks_per_lat = dec_seq // ROW_TILE
    ctx_cond = lambda i: 0
    lat_cond = lambda i: 1 + i // blocks_per_lat

    xp = x_prompt.reshape(batch * seq, D_MODEL)
    xs = x_sample.reshape(dec_batch * dec_seq, D_MODEL)
    ks, vs, ss = [], [], []
    for l in range(DEPTH):
        mod_l = mod[l]

        def layer_fn(x, cond, sq, rope_tables, s0, k_past, v_past):
            x, zh, u, q, k, v = _ffn(x, mod_l, cond, ffn1_w_in, ffn1_w_out, ln_g3, ln_b3, l, 0,
                                     mixer_in=(w_in, qg, kg, rope_tables, sq))
            o_a, s_fin = _hgrn(zh, s0, hgrn_lb_logits, norm_g, l, sq)
            o_b = _conv(u, conv_w, conv_b3, conv_g3, conv_beta3, l, sq)
            if k_past is None:
                o_c = _attention(q, k, v, sq, sq, sq, seqs_per_step=ATTN_SHORT_SEQS_PER_STEP)
            else:
                n = x.shape[0] // sq
                k_all = jnp.concatenate([k_past, k.reshape(n, sq, KV_WIDTH)], axis=1)
                v_all = jnp.concatenate([v_past, v.reshape(n, sq, KV_WIDTH)], axis=1)
                sk = k_all.shape[1]
                o_c = _attention(q, k_all.reshape(n * sk, KV_WIDTH), v_all.reshape(n * sk, KV_WIDTH),
                                 sq, sk, ATTN_Q_TILE_LONG)
            x, = _ffn(x, mod_l, cond, ffn2_w_in, ffn2_w_out, ln_g3, ln_b3, l, 2, mixer_out=(o_a, o_b, o_c, w_out))
            return x, k, v, s_fin

        xp, k_l, v_l, s_l = layer_fn(xp, ctx_cond, seq, None, zero_state, None, None)
        ks.append(k_l.reshape(batch, seq, C_KV_HEADS, HEAD_DIM))
        vs.append(v_l.reshape(batch, seq, C_KV_HEADS, HEAD_DIM))
        ss.append(s_l)
        xs, _, _, _ = layer_fn(xs, lat_cond, dec_seq, tables, lat_state[:, l],
                               cache_k[:, l].reshape(dec_batch, past, KV_WIDTH),
                               cache_v[:, l].reshape(dec_batch, past, KV_WIDTH))

    return (xp.reshape(batch, seq, D_MODEL), xs.reshape(dec_batch, dec_seq, D_MODEL),
            jnp.stack(ks, axis=1), jnp.stack(vs, axis=1), jnp.stack(ss, axis=1))
```

```python
import functools

import numpy as np
import jax
import jax.numpy as jnp
from jax import lax
from jax.experimental import pallas as pl
from jax.experimental.pallas import tpu as pltpu

F32 = jnp.float32
BF16 = jnp.bfloat16

D_MODEL = 1024
DEPTH = 2
GRID_W = 64
HEAD_DIM = 64
A_HEADS = 4
A_DK = 64
A_WIDTH = 256
B_WIDTH = 256
CONV_K = 31
CONV_PAD = 15
C_HEADS = 8
C_KV_HEADS = 2
C_WIDTH = 512
KV_WIDTH = C_KV_HEADS * HEAD_DIM
D_FF = 2816
ROPE_THETA = 10000.0
ROPE_PAIRS = 16
N_MOD = 9
ALPHA = (2 * DEPTH) ** 0.25
F_MIN = 1e-6
LOG2_E = 1.4426950408889634
IN_WIDTH = 5 * A_WIDTH + 2 * B_WIDTH + C_WIDTH + 2 * KV_WIDTH
HG_WIDTH = 5 * A_WIDTH

LANES = 128
SUBLANES = 8
VMEM_BYTES_V7X = 64 * 1024 * 1024

ROW_TILE = 512
ROW_TILE_WIDE = 1024
FFN_CHUNK = 256
HGRN_CHUNK = 128
HGRN_BASE = 8
CONV_TILE = 256
CONV_HALO = 16
ATTN_Q_TILE_LONG = 256
ATTN_KEY_BLOCK = 512
ATTN_SHORT_SEQS_PER_STEP = 4


def _params(semantics, vmem_mb):
    return pltpu.CompilerParams(dimension_semantics=semantics,
                                vmem_limit_bytes=min(vmem_mb * 1024 * 1024, VMEM_BYTES_V7X - (4 << 20)))


def _resident(block_shape, index_map):
    return pl.BlockSpec(block_shape, index_map, pipeline_mode=pl.Buffered(1))


def _sigmoid(x):
    return 1.0 / (1.0 + jnp.exp(-x))


def _layer_norm(y, g, b, eps=1e-5):
    mu = jnp.mean(y, axis=-1, keepdims=True)
    d = y - mu
    var = jnp.mean(d * d, axis=-1, keepdims=True)
    return d * lax.rsqrt(var + eps) * g + b


def _dot(a, b):
    return jnp.dot(a, b, preferred_element_type=F32)


def _dot_nt(a, b):
    return lax.dot_general(a, b, (((1,), (1,)), ((), ())), preferred_element_type=F32)


def _dot_tn(a, b):
    return lax.dot_general(a, b, (((0,), (0,)), ((), ())), preferred_element_type=F32)


def _segment_ones(width, seg):
    r = lax.broadcasted_iota(jnp.int32, (width, width), 0) // seg
    c = lax.broadcasted_iota(jnp.int32, (width, width), 1) // seg
    return (r == c).astype(BF16)


def _segment_sum(x, ones_bd):
    hi = x.astype(BF16)
    lo = (x - hi.astype(F32)).astype(BF16)
    return _dot(hi, ones_bd) + _dot(lo, ones_bd)


def _head_rms_norm(x, gain, eps=1e-6):
    ones_bd = _segment_ones(LANES, HEAD_DIM)
    cols = []
    for c in range(x.shape[1] // LANES):
        xc = x[:, c * LANES:(c + 1) * LANES]
        ms = _segment_sum(xc * xc, ones_bd) * (1.0 / HEAD_DIM)
        cols.append(xc * lax.rsqrt(ms + eps) * gain[:, c * LANES:(c + 1) * LANES])
    return cols[0] if len(cols) == 1 else jnp.concatenate(cols, axis=1)


def _mod_kernel(c_ref, w_ref, b_ref, o_ref):
    c = c_ref[...]
    s = (c * _sigmoid(c)).astype(BF16)
    o_ref[...] = _dot(s, w_ref[...].astype(BF16)) + b_ref[...]


def _modulation(conds, w_mod, b_mod):
    tn = D_MODEL
    return pl.pallas_call(
        _mod_kernel,
        grid=(DEPTH, N_MOD * D_MODEL // tn),
        in_specs=[
            pl.BlockSpec((SUBLANES, D_MODEL), lambda l, j: (0, 0)),
            pl.BlockSpec((None, D_MODEL, tn), lambda l, j: (l, 0, j)),
            pl.BlockSpec((None, 1, tn), lambda l, j: (l, 0, j)),
        ],
        out_specs=pl.BlockSpec((None, SUBLANES, tn), lambda l, j: (l, 0, j)),
        out_shape=jax.ShapeDtypeStruct((DEPTH, SUBLANES, N_MOD * D_MODEL), F32),
        compiler_params=_params(("arbitrary", "arbitrary"), 32),
        name="modulation",
    )(conds, w_mod, b_mod.reshape(DEPTH, 1, N_MOD * D_MODEL))


def _ffn_kernel(*refs, sub, with_mixer_out, mixer_in_rope):
    refs = list(refs)
    x_ref = refs.pop(0)
    if with_mixer_out:
        oa_ref, ob_ref, oc_ref = refs[:3]
        del refs[:3]
    mod_ref = refs.pop(0)
    if with_mixer_out:
        wmix_ref, gmix_ref, bmix_ref = refs[:3]
        del refs[:3]
    wg_ref, wu_ref, wo_ref, g_ref, b_ref = refs[:5]
    del refs[:5]
    if mixer_in_rope is not None:
        n_in = 6 if mixer_in_rope else 3
        mixer_in_refs, refs = refs[:n_in], refs[n_in:]
    o_ref = refs.pop(0)
    mixer_in_outs = refs

    if with_mixer_out:
        y = _dot(oa_ref[...], wmix_ref[0:A_WIDTH, :])
        y = y + _dot(ob_ref[...], wmix_ref[A_WIDTH:A_WIDTH + B_WIDTH, :])
        y = y + _dot(oc_ref[...], wmix_ref[A_WIDTH + B_WIDTH:, :])
        x = _layer_norm(ALPHA * x_ref[...] + mod_ref[5:6, :] * y, gmix_ref[...], bmix_ref[...])
    else:
        x = x_ref[...]
    shift = mod_ref[3 * sub:3 * sub + 1, :]
    scale = mod_ref[3 * sub + 1:3 * sub + 2, :]
    gate = mod_ref[3 * sub + 2:3 * sub + 3, :]
    h = (x * (1.0 + scale) + shift).astype(BF16)
    acc = jnp.zeros(x.shape, F32)
    for j in range(D_FF // FFN_CHUNK):
        cols = slice(j * FFN_CHUNK, (j + 1) * FFN_CHUNK)
        gt = _dot(h, wg_ref[:, cols])
        up = _dot(h, wu_ref[:, cols])
        act = (gt * _sigmoid(gt) * up).astype(BF16)
        acc = acc + _dot(act, wo_ref[cols, :])
    y = ALPHA * x + 0.5 * gate * acc
    x_new = _layer_norm(y, g_ref[...], b_ref[...])
    o_ref[...] = x_new
    if mixer_in_rope is not None:
        _mixer_in(x_new, mod_ref, *mixer_in_refs, *mixer_in_outs)


def _ffn(x, mod_l, cond_of_row, w_in, w_out, ln_g, ln_b, layer, sub, tile, mixer_out=None, mixer_in=None):
    rows = x.shape[0]
    row_spec = lambda width: pl.BlockSpec((tile, width), lambda i: (i, 0))
    cond_of_block = lambda i: cond_of_row(i * tile)
    ln_spec = lambda idx: _resident((None, 1, D_MODEL), lambda i: (layer * 3 + idx, 0, 0))
    in_specs, args = [row_spec(D_MODEL)], [x]
    if mixer_out is not None:
        o_a, o_b, o_c, w_mix = mixer_out
        in_specs += [row_spec(A_WIDTH), row_spec(B_WIDTH), row_spec(C_WIDTH)]
        args += [o_a, o_b, o_c]
    in_specs.append(pl.BlockSpec((None, N_MOD, D_MODEL), lambda i: (cond_of_block(i), 0, 0)))
    args.append(mod_l)
    if mixer_out is not None:
        in_specs += [_resident((None, D_MODEL, D_MODEL), lambda i: (layer, 0, 0)), ln_spec(1), ln_spec(1)]
        args += [w_mix, ln_g, ln_b]
    in_specs += [
        _resident((None, D_MODEL, D_FF), lambda i: (layer, 0, 0)),
        _resident((None, D_MODEL, D_FF), lambda i: (layer, 0, 1)),
        _resident((None, D_FF, D_MODEL), lambda i: (layer, 0, 0)),
        ln_spec(sub), ln_spec(sub),
    ]
    args += [w_in, w_in, w_out, ln_g, ln_b]
    out_specs = [row_spec(D_MODEL)]
    out_shape = [jax.ShapeDtypeStruct((rows, D_MODEL), F32)]
    rope = None
    if mixer_in is not None:
        w_mix_in, qg, kg, rope_tables, seq = mixer_in
        rope = rope_tables is not None
        in_specs += [
            _resident((None, D_MODEL, IN_WIDTH), lambda i: (layer, 0, 0)),
            _resident((None, 1, C_WIDTH), lambda i: (layer, 0, 0)),
            _resident((None, 1, KV_WIDTH), lambda i: (layer, 0, 0)),
        ]
        args += [w_mix_in, qg, kg]
        if rope:
            per_seq = seq // tile
            in_specs += [pl.BlockSpec((tile, LANES), lambda i: (i % per_seq, 0))] * 3
            args += list(rope_tables)
        widths = ((HG_WIDTH, F32), (B_WIDTH, F32), (C_WIDTH, BF16), (KV_WIDTH, F32), (KV_WIDTH, F32))
        out_specs += [row_spec(w) for w, _ in widths]
        out_shape += [jax.ShapeDtypeStruct((rows, w), dt) for w, dt in widths]
    return pl.pallas_call(
        functools.partial(_ffn_kernel, sub=sub, with_mixer_out=mixer_out is not None, mixer_in_rope=rope),
        grid=(rows // tile,),
        in_specs=in_specs,
        out_specs=out_specs,
        out_shape=out_shape,
        compiler_params=_params(("parallel",), 56),
        name=f"ffn{sub // 2 + 1}",
    )(*args)


def _mixer_in(x, mod_ref, w_ref, qg_ref, kg_ref, *refs):
    rope = len(refs) == 8
    if rope:
        cos_ref, s1_ref, s2_ref = refs[:3]
    zh_ref, u_ref, q_ref, k_ref, v_ref = refs[-5:]
    h = (x * (1.0 + mod_ref[4:5, :]) + mod_ref[3:4, :]).astype(BF16)

    zh_ref[...] = _dot(h, w_ref[:, 0:HG_WIDTH])
    o = HG_WIDTH
    glu_a = _dot(h, w_ref[:, o:o + B_WIDTH])
    glu_b = _dot(h, w_ref[:, o + B_WIDTH:o + 2 * B_WIDTH])
    u_ref[...] = glu_a * _sigmoid(glu_b)
    o += 2 * B_WIDTH
    cq = _dot(h, w_ref[:, o:o + C_WIDTH])
    ck = _dot(h, w_ref[:, o + C_WIDTH:o + C_WIDTH + KV_WIDTH])
    v_ref[...] = _dot(h, w_ref[:, o + C_WIDTH + KV_WIDTH:o + C_WIDTH + 2 * KV_WIDTH])

    qn = _head_rms_norm(cq, qg_ref[...])
    kn = _head_rms_norm(ck, kg_ref[...])
    if rope:
        cos, s1, s2 = cos_ref[...], s1_ref[...], s2_ref[...]

        def rot(t):
            cols = []
            for c in range(t.shape[1] // LANES):
                tc = t[:, c * LANES:(c + 1) * LANES]
                cols.append(tc * cos + pltpu.roll(tc, LANES - ROPE_PAIRS, 1) * s1
                            + pltpu.roll(tc, ROPE_PAIRS, 1) * s2)
            return cols[0] if len(cols) == 1 else jnp.concatenate(cols, axis=1)

        qn, kn = rot(qn), rot(kn)
    q_ref[...] = (qn * (HEAD_DIM ** -0.5 * LOG2_E)).astype(BF16)
    k_ref[...] = kn


def _cumsum_rows(x, reverse):
    n = x.shape[0]
    row = lax.broadcasted_iota(jnp.int32, x.shape, 0)
    s = 1
    while s < n:
        if reverse:
            x = x + jnp.where(row < n - s, pltpu.roll(x, n - s, 0), 0.0)
        else:
            x = x + jnp.where(row >= s, pltpu.roll(x, s, 0), 0.0)
        s *= 2
    return x


def _hgrn_kernel(zh_ref, s0_ref, lbl_ref, ng_ref, o_ref, sfin_ref, fwd_ref, bwd_ref, *, layer, seq):
    C, W = HGRN_CHUNK, A_WIDTH
    n_chunks = seq // C

    lg = lbl_ref[...]
    e = jnp.exp(lg - jnp.max(lg, axis=0, keepdims=True))
    soft = e / jnp.sum(e, axis=0, keepdims=True)
    lb = jnp.zeros((1, W), F32)
    for j in range(1, layer + 1):
        lb = lb + soft[j:j + 1, :]

    lane_head = lax.broadcasted_iota(jnp.int32, (1, W), 1) // A_DK
    head_masks = [lane_head == h for h in range(A_HEADS)]
    t_idx = lax.broadcasted_iota(jnp.int32, (C, A_HEADS * C), 0)
    s_idx = lax.broadcasted_iota(jnp.int32, (C, A_HEADS * C), 1) % C
    pair_fwd = jnp.where(s_idx <= t_idx, t_idx ^ s_idx, 2 * C)
    pair_bwd = jnp.where(s_idx >= t_idx, t_idx ^ s_idx, 2 * C)
    bd_mask = (lax.broadcasted_iota(jnp.int32, (W, W), 0) // A_DK
               == lax.broadcasted_iota(jnp.int32, (W, W), 1) // A_DK)
    def stack_heads(t_bf):
        return jnp.concatenate([jnp.where(m, t_bf, jnp.zeros_like(t_bf)) for m in head_masks], axis=0)

    class Unit:
        def __init__(self, n, reverse):
            self.reverse = reverse
            self.rows = pl.ds(pl.multiple_of(n * C, C), C)
            self.pair = pair_bwd if reverse else pair_fwd
            self.scores = jnp.zeros((C, A_HEADS * C), F32)

        def gates(self):
            rows, reverse = self.rows, self.reverse
            self.q = zh_ref[rows, 0:W]
            self.v_bf = zh_ref[rows, W:2 * W].astype(BF16)
            zf = zh_ref[rows, (3 if reverse else 2) * W:(4 if reverse else 3) * W]
            f = lb + (1.0 - lb) * _sigmoid(zf)
            self.kk = 1.0 - f
            self.a = _cumsum_rows(jnp.log(jnp.maximum(f, F_MIN)), reverse) * LOG2_E

        def level(self, h):
            a3 = self.a.reshape(C // h, h, W)
            zero = jnp.zeros((1, 1, W), F32)
            if self.reverse:
                own = a3[:, 0:1, :]
                other = jnp.concatenate([own[1:], zero], axis=0)
            else:
                own = a3[:, h - 1:h, :]
                other = jnp.concatenate([zero, own[:-1]], axis=0)
            qt = (self.q.reshape(C // h, h, W) * jnp.exp2(a3 - other)).reshape(C, W).astype(BF16)
            kt = (self.kk.reshape(C // h, h, W) * jnp.exp2(own - a3)).reshape(C, W).astype(BF16)
            self.scores = jnp.where(self.pair < 2 * h, _dot_nt(qt, stack_heads(kt)), self.scores)

        def base(self):
            hb = HGRN_BASE
            a3 = self.a.reshape(C // hb, hb, W)
            ref_row = a3[:, hb // 2:hb // 2 + 1, :] if self.reverse else a3[:, hb // 2 - 1:hb // 2, :]
            qt = (self.q.reshape(C // hb, hb, W) * jnp.exp2(a3 - ref_row)).reshape(C, W).astype(BF16)
            kt = (self.kk.reshape(C // hb, hb, W) * jnp.exp2(ref_row - a3)).reshape(C, W).astype(BF16)
            self.scores = jnp.where(self.pair < hb, _dot_nt(qt, stack_heads(kt)), self.scores)

        def intra(self):
            self.o = _dot(self.scores.astype(BF16), stack_heads(self.v_bf))
            self.qe = (self.q * jnp.exp2(self.a)).astype(BF16)
            self.a_last = self.a[0:1, :] if self.reverse else self.a[C - 1:C, :]
            self.kv = jnp.where(bd_mask, _dot_tn(self.v_bf, (self.kk * jnp.exp2(self.a_last - self.a)).astype(BF16)),
                                0.0)

        def carry(self, st):
            out_ref = bwd_ref if self.reverse else fwd_ref
            out_ref[self.rows, :] = self.o + _dot_nt(self.qe, st.astype(BF16))
            return st * jnp.exp2(self.a_last) + self.kv

        def run(self, st):
            self.gates()
            h = C // 2
            while h >= HGRN_BASE:
                self.level(h)
                h //= 2
            self.base()
            self.intra()
            return self.carry(st)

    def scan_body(i, carry):
        st_f, st_b = carry
        return Unit(i, False).run(st_f), Unit(n_chunks - 1 - i, True).run(st_b)

    finals = lax.fori_loop(0, n_chunks, scan_body, (s0_ref[0], s0_ref[1]), unroll=2)

    ones_bd = _segment_ones(LANES, A_DK)

    def norm_body(n, carry):
        rows = pl.ds(pl.multiple_of(n * C, C), C)
        tot = fwd_ref[rows, :] + bwd_ref[rows, :]
        zg = zh_ref[rows, 4 * W:5 * W]
        cols = []
        for c in range(W // LANES):
            lanes = slice(c * LANES, (c + 1) * LANES)
            tc = tot[:, lanes]
            ms = _segment_sum(tc * tc, ones_bd) * (1.0 / A_DK)
            cols.append(tc * lax.rsqrt(ms + 1e-6) * ng_ref[:, lanes])
        on = jnp.concatenate(cols, axis=1)
        o_ref[rows, :] = (on * (zg * _sigmoid(zg))).astype(o_ref.dtype)
        return carry

    lax.fori_loop(0, n_chunks, norm_body, 0)

    for d, st in enumerate(finals):
        s_kv = st.T
        for h in range(A_HEADS):
            blk = slice(h * A_DK, (h + 1) * A_DK)
            sfin_ref[d, h] = s_kv[blk, blk]


def _hgrn(zh, s0, lb_logits, norm_g, layer, seq):
    rows = zh.shape[0]
    n_seq = rows // seq
    per_seq_state = s0.shape[0] != 1
    W = A_WIDTH
    return pl.pallas_call(
        functools.partial(_hgrn_kernel, layer=layer, seq=seq),
        grid=(n_seq,),
        in_specs=[
            pl.BlockSpec((seq, HG_WIDTH), lambda b: (b, 0)),
            pl.BlockSpec((None, 2, W, W), (lambda b: (b, 0, 0, 0)) if per_seq_state else (lambda b: (0, 0, 0, 0))),
            _resident((DEPTH, W), lambda b: (0, 0)),
            _resident((None, 1, W), lambda b: (layer, 0, 0)),
        ],
        out_specs=[
            pl.BlockSpec((seq, W), lambda b: (b, 0)),
            pl.BlockSpec((None, 2, A_HEADS, A_DK, A_DK), lambda b: (b, 0, 0, 0, 0)),
        ],
        out_shape=[
            jax.ShapeDtypeStruct((rows, W), BF16),
            jax.ShapeDtypeStruct((n_seq, 2, A_HEADS, A_DK, A_DK), F32),
        ],
        scratch_shapes=[pltpu.VMEM((seq, W), F32), pltpu.VMEM((seq, W), F32)],
        compiler_params=_params(("parallel",), 48),
        name="hgrn2",
    )(zh, s0, lb_logits, norm_g)


def _conv_kernel(up_ref, uc_ref, un_ref, w_ref, b_ref, g_ref, beta_ref, o_ref, sh_ref, *, tiles_per_seq):
    i = pl.program_id(0)
    T, H, S = CONV_TILE, CONV_HALO, SUBLANES
    P = T + 2 * H
    first = (i % tiles_per_seq) == 0
    last = (i % tiles_per_seq) == tiles_per_seq - 1
    sh_ref[0, 0:H, :] = jnp.where(first, 0.0, up_ref[...])
    sh_ref[0, H:H + T, :] = uc_ref[...]
    sh_ref[0, H + T:P, :] = jnp.where(last, 0.0, un_ref[...])
    for b in range(1, S):
        sh_ref[b, 0:P - S, :] = sh_ref[0, b:b + P - S, :]
    acc = jnp.zeros((T, B_WIDTH), F32)
    for j in range(CONV_K):
        start = H - CONV_PAD + j
        a, b = start // S, start % S
        acc = acc + w_ref[j:j + 1, :] * sh_ref[b, a * S:a * S + T, :]
    y = _layer_norm(acc + b_ref[...], g_ref[...], beta_ref[...])
    o_ref[...] = (y * _sigmoid(y)).astype(o_ref.dtype)


def _conv(u, conv_w, conv_b, ln_g, ln_b, layer, seq):
    rows = u.shape[0]
    T, H = CONV_TILE, CONV_HALO
    tiles_per_seq = seq // T
    ratio = T // H
    n_halo_blocks = rows // H
    return pl.pallas_call(
        functools.partial(_conv_kernel, tiles_per_seq=tiles_per_seq),
        grid=(rows // T,),
        in_specs=[
            pl.BlockSpec((H, B_WIDTH), lambda i: (jnp.maximum(i * ratio - 1, 0), 0)),
            pl.BlockSpec((T, B_WIDTH), lambda i: (i, 0)),
            pl.BlockSpec((H, B_WIDTH), lambda i: (jnp.minimum((i + 1) * ratio, n_halo_blocks - 1), 0)),
            _resident((None, CONV_K, B_WIDTH), lambda i: (layer, 0, 0)),
            _resident((None, 1, B_WIDTH), lambda i: (layer, 0, 0)),
            _resident((None, 1, B_WIDTH), lambda i: (layer, 0, 0)),
            _resident((None, 1, B_WIDTH), lambda i: (layer, 0, 0)),
        ],
        out_specs=pl.BlockSpec((T, B_WIDTH), lambda i: (i, 0)),
        out_shape=jax.ShapeDtypeStruct((rows, B_WIDTH), BF16),
        scratch_shapes=[pltpu.VMEM((SUBLANES, T + 2 * H, B_WIDTH), F32)],
        compiler_params=_params(("parallel",), 16),
        name="conv_module",
    )(u, u, u, conv_w, conv_b, ln_g, ln_b)


def _attn_kernel(q_ref, k_ref, v_ref, o_ref, kd_ref, va_ref, *, n_seq, single_q_tile):
    tq = q_ref.shape[0] // n_seq
    seq_k = k_ref.shape[0] // n_seq
    kb = min(seq_k, ATTN_KEY_BLOCK)
    group = C_HEADS // C_KV_HEADS
    half = lax.broadcasted_iota(jnp.int32, (1, LANES), 1) // HEAD_DIM

    def prepare():
        k = k_ref[...]
        k_swapped = pltpu.roll(k, HEAD_DIM, 1)
        v = v_ref[...]
        v_swapped = pltpu.roll(v, HEAD_DIM, 1)
        for g in range(C_KV_HEADS):
            kd_ref[g] = jnp.where(half == g, k, k_swapped).astype(BF16)
            va_ref[g, :, 0:LANES] = jnp.where(half == g, v, v_swapped).astype(BF16)
            va_ref[g, :, LANES:2 * LANES] = jnp.ones(v.shape, BF16)

    if single_q_tile:
        prepare()
    else:
        pl.when(pl.program_id(1) == 0)(prepare)

    n_blocks = seq_k // kb
    chains = [(sq, g) for sq in range(n_seq) for g in range(C_KV_HEADS)]
    units = [(c, b) for b in range(n_blocks) for c in range(len(chains))]
    q_stacked, run_max, acc = {}, {}, {}
    scores, probs, rescale = {}, {}, {}

    def stage_scores(u):
        c, b = units[u]
        sq, g = chains[c]
        if c not in q_stacked:
            heads = []
            for j in range(group):
                hq = g * group + j
                col = q_ref[sq * tq:(sq + 1) * tq, (hq // 2) * LANES:(hq // 2 + 1) * LANES]
                heads.append(jnp.where(half == hq % 2, col, jnp.zeros_like(col)))
            q_stacked[c] = jnp.concatenate(heads, axis=0)
        keys = slice(sq * seq_k + b * kb, sq * seq_k + (b + 1) * kb)
        scores[u] = _dot_nt(q_stacked[c], kd_ref[g, keys, :])

    def stage_exp(u):
        c, b = units[u]
        s = scores.pop(u)
        m_blk = jnp.max(s, axis=-1, keepdims=True)
        m_new = m_blk if b == 0 else jnp.maximum(run_max[c], m_blk)
        probs[u] = jnp.exp2(s - m_new).astype(BF16)
        if b > 0:
            rescale[u] = jnp.exp2(run_max[c] - m_new)
        run_max[c] = m_new

    def stage_values(u):
        c, b = units[u]
        sq, g = chains[c]
        keys = slice(sq * seq_k + b * kb, sq * seq_k + (b + 1) * kb)
        pv = _dot(probs.pop(u), va_ref[g, keys, :])
        acc[c] = pv if b == 0 else rescale.pop(u) * acc[c] + pv
        if b == n_blocks - 1:
            total = acc.pop(c)
            o = total[:, 0:LANES] / total[:, LANES:2 * LANES]
            for j in range(group):
                hq = g * group + j
                o_ref[sq * tq:(sq + 1) * tq, hq * HEAD_DIM:(hq + 1) * HEAD_DIM] = (
                    o[j * tq:(j + 1) * tq, (hq % 2) * HEAD_DIM:(hq % 2 + 1) * HEAD_DIM].astype(o_ref.dtype))

    for t in range(len(units) + 2):
        if t < len(units):
            stage_scores(t)
        if 0 <= t - 1 < len(units):
            stage_exp(t - 1)
        if 0 <= t - 2 < len(units):
            stage_values(t - 2)


def _attention(q, k, v, seq_q, seq_k, tq, seqs_per_step=1):
    n_seq = q.shape[0] // seq_q
    q_tiles = seq_q // tq
    assert seqs_per_step == 1 or q_tiles == 1
    q_rows, k_rows = seqs_per_step * tq, seqs_per_step * seq_k
    return pl.pallas_call(
        functools.partial(_attn_kernel, n_seq=seqs_per_step, single_q_tile=q_tiles == 1),
        grid=(n_seq // seqs_per_step, q_tiles),
        in_specs=[
            pl.BlockSpec((q_rows, C_WIDTH), lambda b, i: (b * q_tiles + i, 0)),
            pl.BlockSpec((k_rows, KV_WIDTH), lambda b, i: (b, 0)),
            pl.BlockSpec((k_rows, KV_WIDTH), lambda b, i: (b, 0)),
        ],
        out_specs=pl.BlockSpec((q_rows, C_WIDTH), lambda b, i: (b * q_tiles + i, 0)),
        out_shape=jax.ShapeDtypeStruct(q.shape, BF16),
        scratch_shapes=[pltpu.VMEM((C_KV_HEADS, k_rows, LANES), BF16),
                        pltpu.VMEM((C_KV_HEADS, k_rows, 2 * LANES), BF16)],
        compiler_params=_params(("parallel", "arbitrary"), 48),
        name="attention",
    )(q, k, v)


def _rope_tables(seq):
    pos = np.arange(seq)
    row_id = (pos // GRID_W).astype(np.float32)
    col_id = (pos % GRID_W).astype(np.float32)
    inv = (np.float32(ROPE_THETA) ** (-np.arange(ROPE_PAIRS, dtype=np.float32) / ROPE_PAIRS)).astype(np.float32)
    lane = np.arange(LANES) % HEAD_DIM
    use_col = (lane // (2 * ROPE_PAIRS)) == 1
    first = (lane % (2 * ROPE_PAIRS)) < ROPE_PAIRS
    freq = inv[lane % ROPE_PAIRS]
    ang = (np.where(use_col[None, :], col_id[:, None], row_id[:, None]) * freq[None, :]).astype(np.float32)
    cos, sin = np.cos(ang).astype(np.float32), np.sin(ang).astype(np.float32)
    zero = np.float32(0.0)
    return tuple(jnp.asarray(t) for t in (cos, np.where(first[None, :], -sin, zero), np.where(first[None, :], zero, sin)))


def _state_to_block_diag(s):
    eye = jnp.eye(A_HEADS, dtype=s.dtype)
    st = jnp.swapaxes(s, -1, -2)
    bd = st[..., :, :, None, :] * eye[:, None, :, None]
    return bd.reshape(s.shape[:-3] + (A_WIDTH, A_WIDTH))


def kernel(x_prompt, x_sample, cache_k, cache_v, state_hgrn, c, c_ctx, w_mod, b_mod, ln_g, ln_b, ffn1_w_in, ffn1_w_out, ffn2_w_in, ffn2_w_out, w_in, w_out, hgrn_lb_logits, hgrn_norm_g, conv_w, conv_b, conv_ln_g, conv_ln_b, q_norm_g, k_norm_g):
    batch, seq, _ = x_prompt.shape
    dec_batch, dec_seq, _ = x_sample.shape
    past = cache_k.shape[2]
    for tile in (ROW_TILE, ROW_TILE_WIDE):
        assert (batch * seq) % tile == 0 and dec_seq % tile == 0
    assert dec_batch + 1 <= SUBLANES
    assert KV_WIDTH == LANES and C_HEADS % (2 * C_KV_HEADS) == 0

    bf = lambda w: w.astype(BF16)
    ffn1_w_in, ffn1_w_out, ffn2_w_in, ffn2_w_out, w_in, w_out = map(
        bf, (ffn1_w_in, ffn1_w_out, ffn2_w_in, ffn2_w_out, w_in, w_out))

    conds = jnp.zeros((SUBLANES, D_MODEL), F32).at[0].set(c_ctx).at[1:1 + dec_batch].set(c)
    mod = _modulation(conds, w_mod, b_mod).reshape(DEPTH, SUBLANES, N_MOD, D_MODEL)

    ln_g3 = ln_g.reshape(DEPTH * 3, 1, D_MODEL)
    ln_b3 = ln_b.reshape(DEPTH * 3, 1, D_MODEL)
    qg = jnp.tile(q_norm_g, (1, C_HEADS)).reshape(DEPTH, 1, C_WIDTH)
    kg = jnp.tile(k_norm_g, (1, C_KV_HEADS)).reshape(DEPTH, 1, KV_WIDTH)
    norm_g = hgrn_norm_g.reshape(DEPTH, 1, A_WIDTH)
    conv_b3 = conv_b.reshape(DEPTH, 1, B_WIDTH)
    conv_g3 = conv_ln_g.reshape(DEPTH, 1, B_WIDTH)
    conv_beta3 = conv_ln_b.reshape(DEPTH, 1, B_WIDTH)
    tables = _rope_tables(dec_seq)
    zero_state = jnp.zeros((1, 2, A_WIDTH, A_WIDTH), F32)
    lat_state = _state_to_block_diag(state_hgrn)

    ctx_cond = lambda row: 0
    lat_cond = lambda row: 1 + row // dec_seq

    xp = x_prompt.reshape(batch * seq, D_MODEL)
    xs = x_sample.reshape(dec_batch * dec_seq, D_MODEL)
    ks, vs, ss = [], [], []
    for l in range(DEPTH):
        mod_l = mod[l]

        def layer_fn(x, cond, sq, rope_tables, s0, k_past, v_past):
            x, zh, u, q, k, v = _ffn(x, mod_l, cond, ffn1_w_in, ffn1_w_out, ln_g3, ln_b3, l, 0, ROW_TILE,
                                     mixer_in=(w_in, qg, kg, rope_tables, sq))
            o_a, s_fin = _hgrn(zh, s0, hgrn_lb_logits, norm_g, l, sq)
            o_b = _conv(u, conv_w, conv_b3, conv_g3, conv_beta3, l, sq)
            if k_past is None:
                o_c = _attention(q, k, v, sq, sq, sq, seqs_per_step=ATTN_SHORT_SEQS_PER_STEP)
            else:
                n = x.shape[0] // sq
                k_all = jnp.concatenate([k_past, k.reshape(n, sq, KV_WIDTH)], axis=1)
                v_all = jnp.concatenate([v_past, v.reshape(n, sq, KV_WIDTH)], axis=1)
                sk = k_all.shape[1]
                o_c = _attention(q, k_all.reshape(n * sk, KV_WIDTH), v_all.reshape(n * sk, KV_WIDTH),
                                 sq, sk, ATTN_Q_TILE_LONG)
            x, = _ffn(x, mod_l, cond, ffn2_w_in, ffn2_w_out, ln_g3, ln_b3, l, 2, ROW_TILE_WIDE,
                      mixer_out=(o_a, o_b, o_c, w_out))
            return x, k, v, s_fin

        xp, k_l, v_l, s_l = layer_fn(xp, ctx_cond, seq, None, zero_state, None, None)
        ks.append(k_l.reshape(batch, seq, C_KV_HEADS, HEAD_DIM))
        vs.append(v_l.reshape(batch, seq, C_KV_HEADS, HEAD_DIM))
        ss.append(s_l)
        xs, _, _, _ = layer_fn(xs, lat_cond, dec_seq, tables, lat_state[:, l],
                               cache_k[:, l].reshape(dec_batch, past, KV_WIDTH),
                               cache_v[:, l].reshape(dec_batch, past, KV_WIDTH))

    return (xp.reshape(batch, seq, D_MODEL), xs.reshape(dec_batch, dec_seq, D_MODEL),
            jnp.stack(ks, axis=1), jnp.stack(vs, axis=1), jnp.stack(ss, axis=1))
```

```python
import functools

import numpy as np
import jax
import jax.numpy as jnp
from jax import lax
from jax.experimental import pallas as pl
from jax.experimental.pallas import tpu as pltpu

F32 = jnp.float32
BF16 = jnp.bfloat16

D_MODEL = 1024
DEPTH = 2
GRID_W = 64
HEAD_DIM = 64
A_HEADS = 4
A_DK = 64
A_WIDTH = 256
B_WIDTH = 256
CONV_K = 31
CONV_PAD = 15
C_HEADS = 8
C_KV_HEADS = 2
C_WIDTH = 512
KV_WIDTH = C_KV_HEADS * HEAD_DIM
D_FF = 2816
ROPE_THETA = 10000.0
ROPE_PAIRS = 16
N_MOD = 9
ALPHA = (2 * DEPTH) ** 0.25
F_MIN = 1e-6
LOG2_E = 1.4426950408889634
IN_WIDTH = 5 * A_WIDTH + 2 * B_WIDTH + C_WIDTH + 2 * KV_WIDTH
HG_WIDTH = 5 * A_WIDTH

LANES = 128
SUBLANES = 8
VMEM_BYTES_V7X = 64 * 1024 * 1024

ROW_TILE = 512
ROW_TILE_WIDE = 1024
FFN_CHUNK = 256
HGRN_CHUNK = 128
HGRN_BASE = 8
HGRN_NORM_ROWS = 256
CONV_TILE = 256
CONV_HALO = 16
ATTN_Q_TILE_LONG = 256
ATTN_KEY_BLOCK = 512
ATTN_SHORT_SEQS_PER_STEP = 4


def _params(semantics, vmem_mb):
    return pltpu.CompilerParams(dimension_semantics=semantics,
                                vmem_limit_bytes=min(vmem_mb * 1024 * 1024, VMEM_BYTES_V7X - (4 << 20)))


def _resident(block_shape, index_map):
    return pl.BlockSpec(block_shape, index_map, pipeline_mode=pl.Buffered(1))


def _sigmoid(x):
    return 1.0 / (1.0 + jnp.exp(-x))


def _layer_norm(y, g, b, eps=1e-5):
    mu = jnp.mean(y, axis=-1, keepdims=True)
    d = y - mu
    var = jnp.mean(d * d, axis=-1, keepdims=True)
    return d * lax.rsqrt(var + eps) * g + b


def _dot(a, b):
    return jnp.dot(a, b, preferred_element_type=F32)


def _dot_nt(a, b):
    return lax.dot_general(a, b, (((1,), (1,)), ((), ())), preferred_element_type=F32)


def _dot_tn(a, b):
    return lax.dot_general(a, b, (((0,), (0,)), ((), ())), preferred_element_type=F32)


def _segment_ones(width, seg):
    r = lax.broadcasted_iota(jnp.int32, (width, width), 0) // seg
    c = lax.broadcasted_iota(jnp.int32, (width, width), 1) // seg
    return (r == c).astype(BF16)


def _segment_sum(x, ones_bd):
    hi = x.astype(BF16)
    lo = (x - hi.astype(F32)).astype(BF16)
    return _dot(hi, ones_bd) + _dot(lo, ones_bd)


def _head_rms_norm(x, gain, eps=1e-6):
    ones_bd = _segment_ones(LANES, HEAD_DIM)
    cols = []
    for c in range(x.shape[1] // LANES):
        xc = x[:, c * LANES:(c + 1) * LANES]
        ms = _segment_sum(xc * xc, ones_bd) * (1.0 / HEAD_DIM)
        cols.append(xc * lax.rsqrt(ms + eps) * gain[:, c * LANES:(c + 1) * LANES])
    return cols[0] if len(cols) == 1 else jnp.concatenate(cols, axis=1)


def _mod_kernel(c_ref, w_ref, b_ref, o_ref):
    c = c_ref[...]
    s = (c * _sigmoid(c)).astype(BF16)
    o_ref[...] = _dot(s, w_ref[...].astype(BF16)) + b_ref[...]


def _modulation(conds, w_mod, b_mod):
    tn = D_MODEL
    return pl.pallas_call(
        _mod_kernel,
        grid=(DEPTH, N_MOD * D_MODEL // tn),
        in_specs=[
            pl.BlockSpec((SUBLANES, D_MODEL), lambda l, j: (0, 0)),
            pl.BlockSpec((None, D_MODEL, tn), lambda l, j: (l, 0, j)),
            pl.BlockSpec((None, 1, tn), lambda l, j: (l, 0, j)),
        ],
        out_specs=pl.BlockSpec((None, SUBLANES, tn), lambda l, j: (l, 0, j)),
        out_shape=jax.ShapeDtypeStruct((DEPTH, SUBLANES, N_MOD * D_MODEL), F32),
        compiler_params=_params(("arbitrary", "arbitrary"), 32),
        name="modulation",
    )(conds, w_mod, b_mod.reshape(DEPTH, 1, N_MOD * D_MODEL))


def _ffn_kernel(*refs, sub, with_mixer_out, mixer_in_rope):
    refs = list(refs)
    x_ref = refs.pop(0)
    if with_mixer_out:
        oa_ref, ob_ref, oc_ref = refs[:3]
        del refs[:3]
    mod_ref = refs.pop(0)
    if with_mixer_out:
        wmix_ref, gmix_ref, bmix_ref = refs[:3]
        del refs[:3]
    wg_ref, wu_ref, wo_ref, g_ref, b_ref = refs[:5]
    del refs[:5]
    if mixer_in_rope is not None:
        n_in = 6 if mixer_in_rope else 3
        mixer_in_refs, refs = refs[:n_in], refs[n_in:]
    o_ref = refs.pop(0)
    mixer_in_outs = refs

    if with_mixer_out:
        y = _dot(oa_ref[...], wmix_ref[0:A_WIDTH, :])
        y = y + _dot(ob_ref[...], wmix_ref[A_WIDTH:A_WIDTH + B_WIDTH, :])
        y = y + _dot(oc_ref[...], wmix_ref[A_WIDTH + B_WIDTH:, :])
        x = _layer_norm(ALPHA * x_ref[...] + mod_ref[5:6, :] * y, gmix_ref[...], bmix_ref[...])
    else:
        x = x_ref[...]
    shift = mod_ref[3 * sub:3 * sub + 1, :]
    scale = mod_ref[3 * sub + 1:3 * sub + 2, :]
    gate = mod_ref[3 * sub + 2:3 * sub + 3, :]
    h = (x * (1.0 + scale) + shift).astype(BF16)
    acc = jnp.zeros(x.shape, F32)
    for j in range(D_FF // FFN_CHUNK):
        cols = slice(j * FFN_CHUNK, (j + 1) * FFN_CHUNK)
        gt = _dot(h, wg_ref[:, cols])
        up = _dot(h, wu_ref[:, cols])
        act = (gt * _sigmoid(gt) * up).astype(BF16)
        acc = acc + _dot(act, wo_ref[cols, :])
    y = ALPHA * x + 0.5 * gate * acc
    x_new = _layer_norm(y, g_ref[...], b_ref[...])
    o_ref[...] = x_new
    if mixer_in_rope is not None:
        _mixer_in(x_new, mod_ref, *mixer_in_refs, *mixer_in_outs)


def _ffn(x, mod_l, cond_of_row, w_in, w_out, ln_g, ln_b, layer, sub, tile, mixer_out=None, mixer_in=None):
    rows = x.shape[0]
    row_spec = lambda width: pl.BlockSpec((tile, width), lambda i: (i, 0))
    cond_of_block = lambda i: cond_of_row(i * tile)
    ln_spec = lambda idx: _resident((None, 1, D_MODEL), lambda i: (layer * 3 + idx, 0, 0))
    in_specs, args = [row_spec(D_MODEL)], [x]
    if mixer_out is not None:
        o_a, o_b, o_c, w_mix = mixer_out
        in_specs += [row_spec(A_WIDTH), row_spec(B_WIDTH), row_spec(C_WIDTH)]
        args += [o_a, o_b, o_c]
    in_specs.append(pl.BlockSpec((None, N_MOD, D_MODEL), lambda i: (cond_of_block(i), 0, 0)))
    args.append(mod_l)
    if mixer_out is not None:
        in_specs += [_resident((None, D_MODEL, D_MODEL), lambda i: (layer, 0, 0)), ln_spec(1), ln_spec(1)]
        args += [w_mix, ln_g, ln_b]
    in_specs += [
        _resident((None, D_MODEL, D_FF), lambda i: (layer, 0, 0)),
        _resident((None, D_MODEL, D_FF), lambda i: (layer, 0, 1)),
        _resident((None, D_FF, D_MODEL), lambda i: (layer, 0, 0)),
        ln_spec(sub), ln_spec(sub),
    ]
    args += [w_in, w_in, w_out, ln_g, ln_b]
    out_specs = [row_spec(D_MODEL)]
    out_shape = [jax.ShapeDtypeStruct((rows, D_MODEL), F32)]
    rope = None
    if mixer_in is not None:
        w_mix_in, qg, kg, rope_tables, seq = mixer_in
        rope = rope_tables is not None
        in_specs += [
            _resident((None, D_MODEL, IN_WIDTH), lambda i: (layer, 0, 0)),
            _resident((None, 1, C_WIDTH), lambda i: (layer, 0, 0)),
            _resident((None, 1, KV_WIDTH), lambda i: (layer, 0, 0)),
        ]
        args += [w_mix_in, qg, kg]
        if rope:
            per_seq = seq // tile
            in_specs += [pl.BlockSpec((tile, LANES), lambda i: (i % per_seq, 0))] * 3
            args += list(rope_tables)
        widths = ((HG_WIDTH, F32), (B_WIDTH, F32), (C_WIDTH, BF16), (KV_WIDTH, F32), (KV_WIDTH, F32))
        out_specs += [row_spec(w) for w, _ in widths]
        out_shape += [jax.ShapeDtypeStruct((rows, w), dt) for w, dt in widths]
    return pl.pallas_call(
        functools.partial(_ffn_kernel, sub=sub, with_mixer_out=mixer_out is not None, mixer_in_rope=rope),
        grid=(rows // tile,),
        in_specs=in_specs,
        out_specs=out_specs,
        out_shape=out_shape,
        compiler_params=_params(("parallel",), 56),
        name=f"ffn{sub // 2 + 1}",
    )(*args)


def _mixer_in(x, mod_ref, w_ref, qg_ref, kg_ref, *refs):
    rope = len(refs) == 8
    if rope:
        cos_ref, s1_ref, s2_ref = refs[:3]
    zh_ref, u_ref, q_ref, k_ref, v_ref = refs[-5:]
    h = (x * (1.0 + mod_ref[4:5, :]) + mod_ref[3:4, :]).astype(BF16)

    zh_ref[...] = _dot(h, w_ref[:, 0:HG_WIDTH])
    o = HG_WIDTH
    glu_a = _dot(h, w_ref[:, o:o + B_WIDTH])
    glu_b = _dot(h, w_ref[:, o + B_WIDTH:o + 2 * B_WIDTH])
    u_ref[...] = glu_a * _sigmoid(glu_b)
    o += 2 * B_WIDTH
    cq = _dot(h, w_ref[:, o:o + C_WIDTH])
    ck = _dot(h, w_ref[:, o + C_WIDTH:o + C_WIDTH + KV_WIDTH])
    v_ref[...] = _dot(h, w_ref[:, o + C_WIDTH + KV_WIDTH:o + C_WIDTH + 2 * KV_WIDTH])

    qn = _head_rms_norm(cq, qg_ref[...])
    kn = _head_rms_norm(ck, kg_ref[...])
    if rope:
        cos, s1, s2 = cos_ref[...], s1_ref[...], s2_ref[...]

        def rot(t):
            cols = []
            for c in range(t.shape[1] // LANES):
                tc = t[:, c * LANES:(c + 1) * LANES]
                cols.append(tc * cos + pltpu.roll(tc, LANES - ROPE_PAIRS, 1) * s1
                            + pltpu.roll(tc, ROPE_PAIRS, 1) * s2)
            return cols[0] if len(cols) == 1 else jnp.concatenate(cols, axis=1)

        qn, kn = rot(qn), rot(kn)
    q_ref[...] = (qn * (HEAD_DIM ** -0.5 * LOG2_E)).astype(BF16)
    k_ref[...] = kn


def _cumsum_rows(x, reverse):
    n = x.shape[0]
    row = lax.broadcasted_iota(jnp.int32, x.shape, 0)
    s = 1
    while s < n:
        if reverse:
            x = x + jnp.where(row < n - s, pltpu.roll(x, n - s, 0), 0.0)
        else:
            x = x + jnp.where(row >= s, pltpu.roll(x, s, 0), 0.0)
        s *= 2
    return x


def _hgrn_kernel(zh_ref, s0_ref, lbl_ref, ng_ref, o_ref, sfin_ref, acc_ref, qe_ref, decay_ref, kv_ref, *,
                 layer, seq):
    C, W = HGRN_CHUNK, A_WIDTH
    n_chunks = seq // C

    lg = lbl_ref[...]
    e = jnp.exp(lg - jnp.max(lg, axis=0, keepdims=True))
    soft = e / jnp.sum(e, axis=0, keepdims=True)
    lb = jnp.zeros((1, W), F32)
    for j in range(1, layer + 1):
        lb = lb + soft[j:j + 1, :]

    lane_head = lax.broadcasted_iota(jnp.int32, (1, W), 1) // A_DK
    head_masks = [lane_head == h for h in range(A_HEADS)]
    t_idx = lax.broadcasted_iota(jnp.int32, (C, A_HEADS * C), 0)
    s_idx = lax.broadcasted_iota(jnp.int32, (C, A_HEADS * C), 1) % C
    pair_level = t_idx ^ s_idx
    key_not_after = s_idx <= t_idx
    key_not_before = s_idx >= t_idx
    bd_mask = (lax.broadcasted_iota(jnp.int32, (W, W), 0) // A_DK
               == lax.broadcasted_iota(jnp.int32, (W, W), 1) // A_DK)
    def stack_heads(t_bf):
        return jnp.concatenate([jnp.where(m, t_bf, jnp.zeros_like(t_bf)) for m in head_masks], axis=0)

    def chunk_rows(n):
        return pl.ds(pl.multiple_of(n * C, C), C)

    def gates(rows, reverse):
        zf = zh_ref[rows, (3 if reverse else 2) * W:(4 if reverse else 3) * W]
        f = lb + (1.0 - lb) * _sigmoid(zf)
        return 1.0 - f, _cumsum_rows(jnp.log(jnp.maximum(f, F_MIN)), reverse) * LOG2_E

    def intra_body(n, carry):
        rows = chunk_rows(n)
        q = zh_ref[rows, 0:W]
        v_bf = zh_ref[rows, W:2 * W].astype(BF16)
        k_f, a_f = gates(rows, False)
        k_b, a_b = gates(rows, True)

        scores = jnp.zeros((C, A_HEADS * C), F32)
        h = C // 2
        while h >= HGRN_BASE:
            split = lambda t: t.reshape(C // (2 * h), 2, h, W)
            q4, af4, ab4, kf4, kb4 = split(q), split(a_f), split(a_b), split(k_f), split(k_b)
            end_first = af4[:, 0, h - 1:h, :]
            start_second = ab4[:, 1, 0:1, :]
            q_fwd = q4[:, 1] * jnp.exp2(af4[:, 1] - end_first)
            k_fwd = kf4[:, 0] * jnp.exp2(end_first - af4[:, 0])
            q_bwd = q4[:, 0] * jnp.exp2(ab4[:, 0] - start_second)
            k_bwd = kb4[:, 1] * jnp.exp2(start_second - ab4[:, 1])
            qt = jnp.stack([q_bwd, q_fwd], axis=1).reshape(C, W).astype(BF16)
            kt = jnp.stack([k_fwd, k_bwd], axis=1).reshape(C, W).astype(BF16)
            scores = jnp.where(pair_level < 2 * h, _dot_nt(qt, stack_heads(kt)), scores)
            h //= 2
        hb = HGRN_BASE
        block = lambda t: t.reshape(C // hb, hb, W)
        base = []
        for a, kk, ref_at, keep in ((a_f, k_f, hb // 2 - 1, key_not_after), (a_b, k_b, hb // 2, key_not_before)):
            a3 = block(a)
            ref_row = a3[:, ref_at:ref_at + 1, :]
            qt = (block(q) * jnp.exp2(a3 - ref_row)).reshape(C, W).astype(BF16)
            kt = (block(kk) * jnp.exp2(ref_row - a3)).reshape(C, W).astype(BF16)
            base.append(jnp.where(keep, _dot_nt(qt, stack_heads(kt)), 0.0))
        scores = jnp.where(pair_level < hb, base[0] + base[1], scores)
        acc_ref[rows, :] = _dot(scores.astype(BF16), stack_heads(v_bf))

        for d, (a, kk, last) in enumerate(((a_f, k_f, C - 1), (a_b, k_b, 0))):
            a_last = a[last:last + 1, :]
            qe_ref[d, rows, :] = (q * jnp.exp2(a)).astype(BF16)
            decay_ref[n, d:d + 1, :] = a_last
            kv_ref[n, d] = jnp.where(bd_mask, _dot_tn(v_bf, (kk * jnp.exp2(a_last - a)).astype(BF16)), 0.0)
        return carry

    lax.fori_loop(0, n_chunks, intra_body, 0, unroll=2)

    def state_body(i, carry):
        new = []
        for d, (st, n) in enumerate(zip(carry, (i, n_chunks - 1 - i))):
            rows = chunk_rows(n)
            acc_ref[rows, :] += _dot_nt(qe_ref[d, rows, :], st.astype(BF16))
            new.append(st * jnp.exp2(decay_ref[n, d:d + 1, :]) + kv_ref[n, d])
        return tuple(new)

    finals = lax.fori_loop(0, n_chunks, state_body, (s0_ref[0], s0_ref[1]), unroll=min(n_chunks, 4))

    ones_bd = _segment_ones(LANES, A_DK)

    def norm_body(n, carry):
        rows = pl.ds(pl.multiple_of(n * HGRN_NORM_ROWS, HGRN_NORM_ROWS), HGRN_NORM_ROWS)
        tot = acc_ref[rows, :]
        zg = zh_ref[rows, 4 * W:5 * W]
        cols = []
        for c in range(W // LANES):
            lanes = slice(c * LANES, (c + 1) * LANES)
            tc = tot[:, lanes]
            ms = _segment_sum(tc * tc, ones_bd) * (1.0 / A_DK)
            cols.append(tc * lax.rsqrt(ms + 1e-6) * ng_ref[:, lanes])
        on = jnp.concatenate(cols, axis=1)
        o_ref[rows, :] = (on * (zg * _sigmoid(zg))).astype(o_ref.dtype)
        return carry

    lax.fori_loop(0, seq // HGRN_NORM_ROWS, norm_body, 0)

    for d, st in enumerate(finals):
        s_kv = st.T
        for h in range(A_HEADS):
            blk = slice(h * A_DK, (h + 1) * A_DK)
            sfin_ref[d, h] = s_kv[blk, blk]


def _hgrn(zh, s0, lb_logits, norm_g, layer, seq):
    rows = zh.shape[0]
    n_seq = rows // seq
    per_seq_state = s0.shape[0] != 1
    W = A_WIDTH
    return pl.pallas_call(
        functools.partial(_hgrn_kernel, layer=layer, seq=seq),
        grid=(n_seq,),
        in_specs=[
            pl.BlockSpec((seq, HG_WIDTH), lambda b: (b, 0)),
            pl.BlockSpec((None, 2, W, W), (lambda b: (b, 0, 0, 0)) if per_seq_state else (lambda b: (0, 0, 0, 0))),
            _resident((DEPTH, W), lambda b: (0, 0)),
            _resident((None, 1, W), lambda b: (layer, 0, 0)),
        ],
        out_specs=[
            pl.BlockSpec((seq, W), lambda b: (b, 0)),
            pl.BlockSpec((None, 2, A_HEADS, A_DK, A_DK), lambda b: (b, 0, 0, 0, 0)),
        ],
        out_shape=[
            jax.ShapeDtypeStruct((rows, W), BF16),
            jax.ShapeDtypeStruct((n_seq, 2, A_HEADS, A_DK, A_DK), F32),
        ],
        scratch_shapes=[pltpu.VMEM((seq, W), F32), pltpu.VMEM((2, seq, W), BF16),
                        pltpu.VMEM((seq // HGRN_CHUNK, 2, W), F32),
                        pltpu.VMEM((seq // HGRN_CHUNK, 2, W, W), F32)],
        compiler_params=_params(("parallel",), 48),
        name="hgrn2",
    )(zh, s0, lb_logits, norm_g)


def _conv_kernel(up_ref, uc_ref, un_ref, w_ref, b_ref, g_ref, beta_ref, o_ref, sh_ref, *, tiles_per_seq):
    i = pl.program_id(0)
    T, H, S = CONV_TILE, CONV_HALO, SUBLANES
    P = T + 2 * H
    first = (i % tiles_per_seq) == 0
    last = (i % tiles_per_seq) == tiles_per_seq - 1
    sh_ref[0, 0:H, :] = jnp.where(first, 0.0, up_ref[...])
    sh_ref[0, H:H + T, :] = uc_ref[...]
    sh_ref[0, H + T:P, :] = jnp.where(last, 0.0, un_ref[...])
    for b in range(1, S):
        sh_ref[b, 0:P - S, :] = sh_ref[0, b:b + P - S, :]
    acc = jnp.zeros((T, B_WIDTH), F32)
    for j in range(CONV_K):
        start = H - CONV_PAD + j
        a, b = start // S, start % S
        acc = acc + w_ref[j:j + 1, :] * sh_ref[b, a * S:a * S + T, :]
    y = _layer_norm(acc + b_ref[...], g_ref[...], beta_ref[...])
    o_ref[...] = (y * _sigmoid(y)).astype(o_ref.dtype)


def _conv(u, conv_w, conv_b, ln_g, ln_b, layer, seq):
    rows = u.shape[0]
    T, H = CONV_TILE, CONV_HALO
    tiles_per_seq = seq // T
    ratio = T // H
    n_halo_blocks = rows // H
    return pl.pallas_call(
        functools.partial(_conv_kernel, tiles_per_seq=tiles_per_seq),
        grid=(rows // T,),
        in_specs=[
            pl.BlockSpec((H, B_WIDTH), lambda i: (jnp.maximum(i * ratio - 1, 0), 0)),
            pl.BlockSpec((T, B_WIDTH), lambda i: (i, 0)),
            pl.BlockSpec((H, B_WIDTH), lambda i: (jnp.minimum((i + 1) * ratio, n_halo_blocks - 1), 0)),
            _resident((None, CONV_K, B_WIDTH), lambda i: (layer, 0, 0)),
            _resident((None, 1, B_WIDTH), lambda i: (layer, 0, 0)),
            _resident((None, 1, B_WIDTH), lambda i: (layer, 0, 0)),
            _resident((None, 1, B_WIDTH), lambda i: (layer, 0, 0)),
        ],
        out_specs=pl.BlockSpec((T, B_WIDTH), lambda i: (i, 0)),
        out_shape=jax.ShapeDtypeStruct((rows, B_WIDTH), BF16),
        scratch_shapes=[pltpu.VMEM((SUBLANES, T + 2 * H, B_WIDTH), F32)],
        compiler_params=_params(("parallel",), 16),
        name="conv_module",
    )(u, u, u, conv_w, conv_b, ln_g, ln_b)


def _attn_kernel(q_ref, k_ref, v_ref, o_ref, kd_ref, va_ref, *, n_seq, single_q_tile):
    tq = q_ref.shape[0] // n_seq
    seq_k = k_ref.shape[0] // n_seq
    kb = min(seq_k, ATTN_KEY_BLOCK)
    group = C_HEADS // C_KV_HEADS
    half = lax.broadcasted_iota(jnp.int32, (1, LANES), 1) // HEAD_DIM

    def prepare():
        k = k_ref[...]
        k_swapped = pltpu.roll(k, HEAD_DIM, 1)
        v = v_ref[...]
        v_swapped = pltpu.roll(v, HEAD_DIM, 1)
        for g in range(C_KV_HEADS):
            kd_ref[g] = jnp.where(half == g, k, k_swapped).astype(BF16)
            va_ref[g, :, 0:LANES] = jnp.where(half == g, v, v_swapped).astype(BF16)
            va_ref[g, :, LANES:2 * LANES] = jnp.ones(v.shape, BF16)

    if single_q_tile:
        prepare()
    else:
        pl.when(pl.program_id(1) == 0)(prepare)

    n_blocks = seq_k // kb
    chains = [(sq, g) for sq in range(n_seq) for g in range(C_KV_HEADS)]
    units = [(c, b) for b in range(n_blocks) for c in range(len(chains))]
    q_stacked, run_max, acc = {}, {}, {}
    scores, probs, rescale = {}, {}, {}

    def stage_scores(u):
        c, b = units[u]
        sq, g = chains[c]
        if c not in q_stacked:
            heads = []
            for j in range(group):
                hq = g * group + j
                col = q_ref[sq * tq:(sq + 1) * tq, (hq // 2) * LANES:(hq // 2 + 1) * LANES]
                heads.append(jnp.where(half == hq % 2, col, jnp.zeros_like(col)))
            q_stacked[c] = jnp.concatenate(heads, axis=0)
        keys = slice(sq * seq_k + b * kb, sq * seq_k + (b + 1) * kb)
        scores[u] = _dot_nt(q_stacked[c], kd_ref[g, keys, :])

    def stage_exp(u):
        c, b = units[u]
        s = scores.pop(u)
        m_blk = jnp.max(s, axis=-1, keepdims=True)
        m_new = m_blk if b == 0 else jnp.maximum(run_max[c], m_blk)
        probs[u] = jnp.exp2(s - m_new).astype(BF16)
        if b > 0:
            rescale[u] = jnp.exp2(run_max[c] - m_new)
        run_max[c] = m_new

    def stage_values(u):
        c, b = units[u]
        sq, g = chains[c]
        keys = slice(sq * seq_k + b * kb, sq * seq_k + (b + 1) * kb)
        pv = _dot(probs.pop(u), va_ref[g, keys, :])
        acc[c] = pv if b == 0 else rescale.pop(u) * acc[c] + pv
        if b == n_blocks - 1:
            total = acc.pop(c)
            o = total[:, 0:LANES] / total[:, LANES:2 * LANES]
            for j in range(group):
                hq = g * group + j
                o_ref[sq * tq:(sq + 1) * tq, hq * HEAD_DIM:(hq + 1) * HEAD_DIM] = (
                    o[j * tq:(j + 1) * tq, (hq % 2) * HEAD_DIM:(hq % 2 + 1) * HEAD_DIM].astype(o_ref.dtype))

    for t in range(len(units) + 2):
        if t < len(units):
            stage_scores(t)
        if 0 <= t - 1 < len(units):
            stage_exp(t - 1)
        if 0 <= t - 2 < len(units):
            stage_values(t - 2)


def _attention(q, k, v, seq_q, seq_k, tq, seqs_per_step=1):
    n_seq = q.shape[0] // seq_q
    q_tiles = seq_q // tq
    assert seqs_per_step == 1 or q_tiles == 1
    q_rows, k_rows = seqs_per_step * tq, seqs_per_step * seq_k
    return pl.pallas_call(
        functools.partial(_attn_kernel, n_seq=seqs_per_step, single_q_tile=q_tiles == 1),
        grid=(n_seq // seqs_per_step, q_tiles),
        in_specs=[
            pl.BlockSpec((q_rows, C_WIDTH), lambda b, i: (b * q_tiles + i, 0)),
            pl.BlockSpec((k_rows, KV_WIDTH), lambda b, i: (b, 0)),
            pl.BlockSpec((k_rows, KV_WIDTH), lambda b, i: (b, 0)),
        ],
        out_specs=pl.BlockSpec((q_rows, C_WIDTH), lambda b, i: (b * q_tiles + i, 0)),
        out_shape=jax.ShapeDtypeStruct(q.shape, BF16),
        scratch_shapes=[pltpu.VMEM((C_KV_HEADS, k_rows, LANES), BF16),
                        pltpu.VMEM((C_KV_HEADS, k_rows, 2 * LANES), BF16)],
        compiler_params=_params(("parallel", "arbitrary"), 48),
        name="attention",
    )(q, k, v)


def _rope_tables(seq):
    pos = np.arange(seq)
    row_id = (pos // GRID_W).astype(np.float32)
    col_id = (pos % GRID_W).astype(np.float32)
    inv = (np.float32(ROPE_THETA) ** (-np.arange(ROPE_PAIRS, dtype=np.float32) / ROPE_PAIRS)).astype(np.float32)
    lane = np.arange(LANES) % HEAD_DIM
    use_col = (lane // (2 * ROPE_PAIRS)) == 1
    first = (lane % (2 * ROPE_PAIRS)) < ROPE_PAIRS
    freq = inv[lane % ROPE_PAIRS]
    ang = (np.where(use_col[None, :], col_id[:, None], row_id[:, None]) * freq[None, :]).astype(np.float32)
    cos, sin = np.cos(ang).astype(np.float32), np.sin(ang).astype(np.float32)
    zero = np.float32(0.0)
    return tuple(jnp.asarray(t) for t in (cos, np.where(first[None, :], -sin, zero), np.where(first[None, :], zero, sin)))


def _state_to_block_diag(s):
    eye = jnp.eye(A_HEADS, dtype=s.dtype)
    st = jnp.swapaxes(s, -1, -2)
    bd = st[..., :, :, None, :] * eye[:, None, :, None]
    return bd.reshape(s.shape[:-3] + (A_WIDTH, A_WIDTH))


def kernel(x_prompt, x_sample, cache_k, cache_v, state_hgrn, c, c_ctx, w_mod, b_mod, ln_g, ln_b, ffn1_w_in, ffn1_w_out, ffn2_w_in, ffn2_w_out, w_in, w_out, hgrn_lb_logits, hgrn_norm_g, conv_w, conv_b, conv_ln_g, conv_ln_b, q_norm_g, k_norm_g):
    batch, seq, _ = x_prompt.shape
    dec_batch, dec_seq, _ = x_sample.shape
    past = cache_k.shape[2]
    for tile in (ROW_TILE, ROW_TILE_WIDE):
        assert (batch * seq) % tile == 0 and dec_seq % tile == 0
    assert dec_batch + 1 <= SUBLANES
    assert KV_WIDTH == LANES and C_HEADS % (2 * C_KV_HEADS) == 0

    bf = lambda w: w.astype(BF16)
    ffn1_w_in, ffn1_w_out, ffn2_w_in, ffn2_w_out, w_in, w_out = map(
        bf, (ffn1_w_in, ffn1_w_out, ffn2_w_in, ffn2_w_out, w_in, w_out))

    conds = jnp.zeros((SUBLANES, D_MODEL), F32).at[0].set(c_ctx).at[1:1 + dec_batch].set(c)
    mod = _modulation(conds, w_mod, b_mod).reshape(DEPTH, SUBLANES, N_MOD, D_MODEL)

    ln_g3 = ln_g.reshape(DEPTH * 3, 1, D_MODEL)
    ln_b3 = ln_b.reshape(DEPTH * 3, 1, D_MODEL)
    qg = jnp.tile(q_norm_g, (1, C_HEADS)).reshape(DEPTH, 1, C_WIDTH)
    kg = jnp.tile(k_norm_g, (1, C_KV_HEADS)).reshape(DEPTH, 1, KV_WIDTH)
    norm_g = hgrn_norm_g.reshape(DEPTH, 1, A_WIDTH)
    conv_b3 = conv_b.reshape(DEPTH, 1, B_WIDTH)
    conv_g3 = conv_ln_g.reshape(DEPTH, 1, B_WIDTH)
    conv_beta3 = conv_ln_b.reshape(DEPTH, 1, B_WIDTH)
    tables = _rope_tables(dec_seq)
    zero_state = jnp.zeros((1, 2, A_WIDTH, A_WIDTH), F32)
    lat_state = _state_to_block_diag(state_hgrn)

    ctx_cond = lambda row: 0
    lat_cond = lambda row: 1 + row // dec_seq

    xp = x_prompt.reshape(batch * seq, D_MODEL)
    xs = x_sample.reshape(dec_batch * dec_seq, D_MODEL)
    ks, vs, ss = [], [], []
    for l in range(DEPTH):
        mod_l = mod[l]

        def layer_fn(x, cond, sq, rope_tables, s0, k_past, v_past):
            x, zh, u, q, k, v = _ffn(x, mod_l, cond, ffn1_w_in, ffn1_w_out, ln_g3, ln_b3, l, 0, ROW_TILE,
                                     mixer_in=(w_in, qg, kg, rope_tables, sq))
            o_a, s_fin = _hgrn(zh, s0, hgrn_lb_logits, norm_g, l, sq)
            o_b = _conv(u, conv_w, conv_b3, conv_g3, conv_beta3, l, sq)
            if k_past is None:
                o_c = _attention(q, k, v, sq, sq, sq, seqs_per_step=ATTN_SHORT_SEQS_PER_STEP)
            else:
                n = x.shape[0] // sq
                k_all = jnp.concatenate([k_past, k.reshape(n, sq, KV_WIDTH)], axis=1)
                v_all = jnp.concatenate([v_past, v.reshape(n, sq, KV_WIDTH)], axis=1)
                sk = k_all.shape[1]
                o_c = _attention(q, k_all.reshape(n * sk, KV_WIDTH), v_all.reshape(n * sk, KV_WIDTH),
                                 sq, sk, ATTN_Q_TILE_LONG)
            x, = _ffn(x, mod_l, cond, ffn2_w_in, ffn2_w_out, ln_g3, ln_b3, l, 2, ROW_TILE_WIDE,
                      mixer_out=(o_a, o_b, o_c, w_out))
            return x, k, v, s_fin

        xp, k_l, v_l, s_l = layer_fn(xp, ctx_cond, seq, None, zero_state, None, None)
        ks.append(k_l.reshape(batch, seq, C_KV_HEADS, HEAD_DIM))
        vs.append(v_l.reshape(batch, seq, C_KV_HEADS, HEAD_DIM))
        ss.append(s_l)
        xs, _, _, _ = layer_fn(xs, lat_cond, dec_seq, tables, lat_state[:, l],
                               cache_k[:, l].reshape(dec_batch, past, KV_WIDTH),
                               cache_v[:, l].reshape(dec_batch, past, KV_WIDTH))

    return (xp.reshape(batch, seq, D_MODEL), xs.reshape(dec_batch, dec_seq, D_MODEL),
            jnp.stack(ks, axis=1), jnp.stack(vs, axis=1), jnp.stack(ss, axis=1))
```

```python
import functools

import numpy as np
import jax
import jax.numpy as jnp
from jax import lax
from jax.experimental import pallas as pl
from jax.experimental.pallas import tpu as pltpu

F32 = jnp.float32
BF16 = jnp.bfloat16

D_MODEL = 1024
DEPTH = 2
GRID_W = 64
HEAD_DIM = 64
A_HEADS = 4
A_DK = 64
A_WIDTH = 256
B_WIDTH = 256
CONV_K = 31
CONV_PAD = 15
C_HEADS = 8
C_KV_HEADS = 2
C_WIDTH = 512
KV_WIDTH = C_KV_HEADS * HEAD_DIM
D_FF = 2816
ROPE_THETA = 10000.0
ROPE_PAIRS = 16
N_MOD = 9
ALPHA = (2 * DEPTH) ** 0.25
F_MIN = 1e-6
LOG2_E = 1.4426950408889634
IN_WIDTH = 5 * A_WIDTH + 2 * B_WIDTH + C_WIDTH + 2 * KV_WIDTH
HG_WIDTH = 5 * A_WIDTH

LANES = 128
SUBLANES = 8
VMEM_BYTES_V7X = 64 * 1024 * 1024

ROW_TILE = 512
ROW_TILE_WIDE = 512
FFN_CHUNK = 256
HGRN_CHUNK = 128
HGRN_BASE = 8
HGRN_NORM_ROWS = 256
CONV_TILE = 256
CONV_HALO = 16
ATTN_Q_TILE_LONG = 256
ATTN_KEY_BLOCK = 512
ATTN_SHORT_SEQS_PER_STEP = 4


def _params(semantics, vmem_mb):
    return pltpu.CompilerParams(dimension_semantics=semantics,
                                vmem_limit_bytes=min(vmem_mb * 1024 * 1024, VMEM_BYTES_V7X - (4 << 20)))


def _resident(block_shape, index_map):
    return pl.BlockSpec(block_shape, index_map, pipeline_mode=pl.Buffered(1))


def _sigmoid(x):
    return 1.0 / (1.0 + jnp.exp(-x))


def _layer_norm(y, g, b, eps=1e-5):
    mu = jnp.mean(y, axis=-1, keepdims=True)
    d = y - mu
    var = jnp.mean(d * d, axis=-1, keepdims=True)
    return d * lax.rsqrt(var + eps) * g + b


def _dot(a, b):
    return jnp.dot(a, b, preferred_element_type=F32)


def _dot_nt(a, b):
    return lax.dot_general(a, b, (((1,), (1,)), ((), ())), preferred_element_type=F32)


def _dot_tn(a, b):
    return lax.dot_general(a, b, (((0,), (0,)), ((), ())), preferred_element_type=F32)


def _segment_ones(width, seg):
    r = lax.broadcasted_iota(jnp.int32, (width, width), 0) // seg
    c = lax.broadcasted_iota(jnp.int32, (width, width), 1) // seg
    return (r == c).astype(BF16)


def _segment_sum(x, ones_bd):
    hi = x.astype(BF16)
    lo = (x - hi.astype(F32)).astype(BF16)
    return _dot(hi, ones_bd) + _dot(lo, ones_bd)


def _head_rms_norm(x, gain, eps=1e-6):
    ones_bd = _segment_ones(LANES, HEAD_DIM)
    cols = []
    for c in range(x.shape[1] // LANES):
        xc = x[:, c * LANES:(c + 1) * LANES]
        ms = _segment_sum(xc * xc, ones_bd) * (1.0 / HEAD_DIM)
        cols.append(xc * lax.rsqrt(ms + eps) * gain[:, c * LANES:(c + 1) * LANES])
    return cols[0] if len(cols) == 1 else jnp.concatenate(cols, axis=1)


def _mod_kernel(c_ref, w_ref, b_ref, o_ref):
    c = c_ref[...]
    s = (c * _sigmoid(c)).astype(BF16)
    o_ref[...] = _dot(s, w_ref[...].astype(BF16)) + b_ref[...]


def _modulation(conds, w_mod, b_mod):
    tn = D_MODEL
    return pl.pallas_call(
        _mod_kernel,
        grid=(DEPTH, N_MOD * D_MODEL // tn),
        in_specs=[
            pl.BlockSpec((SUBLANES, D_MODEL), lambda l, j: (0, 0)),
            pl.BlockSpec((None, D_MODEL, tn), lambda l, j: (l, 0, j)),
            pl.BlockSpec((None, 1, tn), lambda l, j: (l, 0, j)),
        ],
        out_specs=pl.BlockSpec((None, SUBLANES, tn), lambda l, j: (l, 0, j)),
        out_shape=jax.ShapeDtypeStruct((DEPTH, SUBLANES, N_MOD * D_MODEL), F32),
        compiler_params=_params(("arbitrary", "arbitrary"), 32),
        name="modulation",
    )(conds, w_mod, b_mod.reshape(DEPTH, 1, N_MOD * D_MODEL))


def _ffn_kernel(*refs, sub, with_mixer_out, mixer_in_rope):
    refs = list(refs)
    x_ref = refs.pop(0)
    if with_mixer_out:
        oa_ref, ob_ref, oc_ref = refs[:3]
        del refs[:3]
    mod_ref = refs.pop(0)
    if with_mixer_out:
        wmix_ref, gmix_ref, bmix_ref = refs[:3]
        del refs[:3]
    wg_ref, wu_ref, wo_ref, g_ref, b_ref = refs[:5]
    del refs[:5]
    if mixer_in_rope is not None:
        n_in = 6 if mixer_in_rope else 3
        mixer_in_refs, refs = refs[:n_in], refs[n_in:]
    o_ref = refs.pop(0)
    mixer_in_outs = refs

    if with_mixer_out:
        y = _dot(oa_ref[...], wmix_ref[0:A_WIDTH, :].astype(BF16))
        y = y + _dot(ob_ref[...], wmix_ref[A_WIDTH:A_WIDTH + B_WIDTH, :].astype(BF16))
        y = y + _dot(oc_ref[...], wmix_ref[A_WIDTH + B_WIDTH:, :].astype(BF16))
        x = _layer_norm(ALPHA * x_ref[...] + mod_ref[5:6, :] * y, gmix_ref[...], bmix_ref[...])
    else:
        x = x_ref[...]
    shift = mod_ref[3 * sub:3 * sub + 1, :]
    scale = mod_ref[3 * sub + 1:3 * sub + 2, :]
    gate = mod_ref[3 * sub + 2:3 * sub + 3, :]
    h = (x * (1.0 + scale) + shift).astype(BF16)
    acc = jnp.zeros(x.shape, F32)
    for j in range(D_FF // FFN_CHUNK):
        cols = slice(j * FFN_CHUNK, (j + 1) * FFN_CHUNK)
        gt = _dot(h, wg_ref[:, cols].astype(BF16))
        up = _dot(h, wu_ref[:, cols].astype(BF16))
        act = (gt * _sigmoid(gt) * up).astype(BF16)
        acc = acc + _dot(act, wo_ref[cols, :].astype(BF16))
    y = ALPHA * x + 0.5 * gate * acc
    x_new = _layer_norm(y, g_ref[...], b_ref[...])
    o_ref[...] = x_new
    if mixer_in_rope is not None:
        _mixer_in(x_new, mod_ref, *mixer_in_refs, *mixer_in_outs)


def _ffn(x, mod_l, cond_of_row, w_in, w_out, ln_g, ln_b, layer, sub, tile, mixer_out=None, mixer_in=None):
    rows = x.shape[0]
    row_spec = lambda width: pl.BlockSpec((tile, width), lambda i: (i, 0))
    cond_of_block = lambda i: cond_of_row(i * tile)
    ln_spec = lambda idx: _resident((None, 1, D_MODEL), lambda i: (layer * 3 + idx, 0, 0))
    in_specs, args = [row_spec(D_MODEL)], [x]
    if mixer_out is not None:
        o_a, o_b, o_c, w_mix = mixer_out
        in_specs += [row_spec(A_WIDTH), row_spec(B_WIDTH), row_spec(C_WIDTH)]
        args += [o_a, o_b, o_c]
    in_specs.append(pl.BlockSpec((None, N_MOD, D_MODEL), lambda i: (cond_of_block(i), 0, 0)))
    args.append(mod_l)
    if mixer_out is not None:
        in_specs += [_resident((None, D_MODEL, D_MODEL), lambda i: (layer, 0, 0)), ln_spec(1), ln_spec(1)]
        args += [w_mix, ln_g, ln_b]
    in_specs += [
        _resident((None, D_MODEL, D_FF), lambda i: (layer, 0, 0)),
        _resident((None, D_MODEL, D_FF), lambda i: (layer, 0, 1)),
        _resident((None, D_FF, D_MODEL), lambda i: (layer, 0, 0)),
        ln_spec(sub), ln_spec(sub),
    ]
    args += [w_in, w_in, w_out, ln_g, ln_b]
    out_specs = [row_spec(D_MODEL)]
    out_shape = [jax.ShapeDtypeStruct((rows, D_MODEL), F32)]
    rope = None
    if mixer_in is not None:
        w_mix_in, qg, kg, rope_tables, seq = mixer_in
        rope = rope_tables is not None
        in_specs += [
            _resident((None, D_MODEL, IN_WIDTH), lambda i: (layer, 0, 0)),
            _resident((None, 1, C_WIDTH), lambda i: (layer, 0, 0)),
            _resident((None, 1, KV_WIDTH), lambda i: (layer, 0, 0)),
        ]
        args += [w_mix_in, qg, kg]
        if rope:
            per_seq = seq // tile
            in_specs += [pl.BlockSpec((tile, LANES), lambda i: (i % per_seq, 0))] * 3
            args += list(rope_tables)
        widths = ((HG_WIDTH, F32), (B_WIDTH, F32), (C_WIDTH, BF16), (KV_WIDTH, F32), (KV_WIDTH, F32))
        out_specs += [row_spec(w) for w, _ in widths]
        out_shape += [jax.ShapeDtypeStruct((rows, w), dt) for w, dt in widths]
    return pl.pallas_call(
        functools.partial(_ffn_kernel, sub=sub, with_mixer_out=mixer_out is not None, mixer_in_rope=rope),
        grid=(rows // tile,),
        in_specs=in_specs,
        out_specs=out_specs,
        out_shape=out_shape,
        compiler_params=_params(("parallel",), 56),
        name=f"ffn{sub // 2 + 1}",
    )(*args)


def _mixer_in(x, mod_ref, w_ref, qg_ref, kg_ref, *refs):
    rope = len(refs) == 8
    if rope:
        cos_ref, s1_ref, s2_ref = refs[:3]
    zh_ref, u_ref, q_ref, k_ref, v_ref = refs[-5:]
    h = (x * (1.0 + mod_ref[4:5, :]) + mod_ref[3:4, :]).astype(BF16)

    zh_ref[...] = _dot(h, w_ref[:, 0:HG_WIDTH])
    o = HG_WIDTH
    glu_a = _dot(h, w_ref[:, o:o + B_WIDTH])
    glu_b = _dot(h, w_ref[:, o + B_WIDTH:o + 2 * B_WIDTH])
    u_ref[...] = glu_a * _sigmoid(glu_b)
    o += 2 * B_WIDTH
    cq = _dot(h, w_ref[:, o:o + C_WIDTH])
    ck = _dot(h, w_ref[:, o + C_WIDTH:o + C_WIDTH + KV_WIDTH])
    v_ref[...] = _dot(h, w_ref[:, o + C_WIDTH + KV_WIDTH:o + C_WIDTH + 2 * KV_WIDTH])

    qn = _head_rms_norm(cq, qg_ref[...])
    kn = _head_rms_norm(ck, kg_ref[...])
    if rope:
        cos, s1, s2 = cos_ref[...], s1_ref[...], s2_ref[...]

        def rot(t):
            cols = []
            for c in range(t.shape[1] // LANES):
                tc = t[:, c * LANES:(c + 1) * LANES]
                cols.append(tc * cos + pltpu.roll(tc, LANES - ROPE_PAIRS, 1) * s1
                            + pltpu.roll(tc, ROPE_PAIRS, 1) * s2)
            return cols[0] if len(cols) == 1 else jnp.concatenate(cols, axis=1)

        qn, kn = rot(qn), rot(kn)
    q_ref[...] = (qn * (HEAD_DIM ** -0.5 * LOG2_E)).astype(BF16)
    k_ref[...] = kn


def _cumsum_rows(x, reverse):
    n = x.shape[0]
    row = lax.broadcasted_iota(jnp.int32, x.shape, 0)
    s = 1
    while s < n:
        if reverse:
            x = x + jnp.where(row < n - s, pltpu.roll(x, n - s, 0), 0.0)
        else:
            x = x + jnp.where(row >= s, pltpu.roll(x, s, 0), 0.0)
        s *= 2
    return x


def _hgrn_kernel(zh_ref, s0_ref, lbl_ref, ng_ref, o_ref, sfin_ref, acc_ref, qe_ref, decay_ref, kv_ref, *,
                 layer, seq):
    C, W = HGRN_CHUNK, A_WIDTH
    n_chunks = seq // C

    lg = lbl_ref[...]
    e = jnp.exp(lg - jnp.max(lg, axis=0, keepdims=True))
    soft = e / jnp.sum(e, axis=0, keepdims=True)
    lb = jnp.zeros((1, W), F32)
    for j in range(1, layer + 1):
        lb = lb + soft[j:j + 1, :]

    lane_head = lax.broadcasted_iota(jnp.int32, (1, W), 1) // A_DK
    head_masks = [lane_head == h for h in range(A_HEADS)]
    t_idx = lax.broadcasted_iota(jnp.int32, (C, A_HEADS * C), 0)
    s_idx = lax.broadcasted_iota(jnp.int32, (C, A_HEADS * C), 1) % C
    pair_level = t_idx ^ s_idx
    key_not_after = s_idx <= t_idx
    key_not_before = s_idx >= t_idx
    bd_mask = (lax.broadcasted_iota(jnp.int32, (W, W), 0) // A_DK
               == lax.broadcasted_iota(jnp.int32, (W, W), 1) // A_DK)
    def stack_heads(t_bf):
        return jnp.concatenate([jnp.where(m, t_bf, jnp.zeros_like(t_bf)) for m in head_masks], axis=0)

    def chunk_rows(n):
        return pl.ds(pl.multiple_of(n * C, C), C)

    def gates(rows, reverse):
        zf = zh_ref[rows, (3 if reverse else 2) * W:(4 if reverse else 3) * W]
        f = lb + (1.0 - lb) * _sigmoid(zf)
        return 1.0 - f, _cumsum_rows(jnp.log(jnp.maximum(f, F_MIN)), reverse) * LOG2_E

    def intra_body(n, carry):
        rows = chunk_rows(n)
        q = zh_ref[rows, 0:W]
        v_bf = zh_ref[rows, W:2 * W].astype(BF16)
        k_f, a_f = gates(rows, False)
        k_b, a_b = gates(rows, True)

        scores = jnp.zeros((C, A_HEADS * C), F32)
        h = C // 2
        while h >= HGRN_BASE:
            split = lambda t: t.reshape(C // (2 * h), 2, h, W)
            q4, af4, ab4, kf4, kb4 = split(q), split(a_f), split(a_b), split(k_f), split(k_b)
            end_first = af4[:, 0, h - 1:h, :]
            start_second = ab4[:, 1, 0:1, :]
            q_fwd = q4[:, 1] * jnp.exp2(af4[:, 1] - end_first)
            k_fwd = kf4[:, 0] * jnp.exp2(end_first - af4[:, 0])
            q_bwd = q4[:, 0] * jnp.exp2(ab4[:, 0] - start_second)
            k_bwd = kb4[:, 1] * jnp.exp2(start_second - ab4[:, 1])
            qt = jnp.stack([q_bwd, q_fwd], axis=1).reshape(C, W).astype(BF16)
            kt = jnp.stack([k_fwd, k_bwd], axis=1).reshape(C, W).astype(BF16)
            scores = jnp.where(pair_level < 2 * h, _dot_nt(qt, stack_heads(kt)), scores)
            h //= 2
        hb = HGRN_BASE
        block = lambda t: t.reshape(C // hb, hb, W)
        base = []
        for a, kk, ref_at, keep in ((a_f, k_f, hb // 2 - 1, key_not_after), (a_b, k_b, hb // 2, key_not_before)):
            a3 = block(a)
            ref_row = a3[:, ref_at:ref_at + 1, :]
            qt = (block(q) * jnp.exp2(a3 - ref_row)).reshape(C, W).astype(BF16)
            kt = (block(kk) * jnp.exp2(ref_row - a3)).reshape(C, W).astype(BF16)
            base.append(jnp.where(keep, _dot_nt(qt, stack_heads(kt)), 0.0))
        scores = jnp.where(pair_level < hb, base[0] + base[1], scores)
        acc_ref[rows, :] = _dot(scores.astype(BF16), stack_heads(v_bf))

        for d, (a, kk, last) in enumerate(((a_f, k_f, C - 1), (a_b, k_b, 0))):
            a_last = a[last:last + 1, :]
            qe_ref[d, rows, :] = (q * jnp.exp2(a)).astype(BF16)
            decay_ref[n, d:d + 1, :] = a_last
            kv_ref[n, d] = jnp.where(bd_mask, _dot_tn(v_bf, (kk * jnp.exp2(a_last - a)).astype(BF16)), 0.0)
        return carry

    lax.fori_loop(0, n_chunks, intra_body, 0, unroll=2)

    def state_body(i, carry):
        new = []
        for d, (st, n) in enumerate(zip(carry, (i, n_chunks - 1 - i))):
            rows = chunk_rows(n)
            acc_ref[rows, :] += _dot_nt(qe_ref[d, rows, :], st.astype(BF16))
            new.append(st * jnp.exp2(decay_ref[n, d:d + 1, :]) + kv_ref[n, d])
        return tuple(new)

    finals = lax.fori_loop(0, n_chunks, state_body, (s0_ref[0], s0_ref[1]), unroll=min(n_chunks, 4))

    ones_bd = _segment_ones(LANES, A_DK)

    def norm_body(n, carry):
        rows = pl.ds(pl.multiple_of(n * HGRN_NORM_ROWS, HGRN_NORM_ROWS), HGRN_NORM_ROWS)
        tot = acc_ref[rows, :]
        zg = zh_ref[rows, 4 * W:5 * W]
        cols = []
        for c in range(W // LANES):
            lanes = slice(c * LANES, (c + 1) * LANES)
            tc = tot[:, lanes]
            ms = _segment_sum(tc * tc, ones_bd) * (1.0 / A_DK)
            cols.append(tc * lax.rsqrt(ms + 1e-6) * ng_ref[:, lanes])
        on = jnp.concatenate(cols, axis=1)
        o_ref[rows, :] = (on * (zg * _sigmoid(zg))).astype(o_ref.dtype)
        return carry

    lax.fori_loop(0, seq // HGRN_NORM_ROWS, norm_body, 0)

    for d, st in enumerate(finals):
        s_kv = st.T
        for h in range(A_HEADS):
            blk = slice(h * A_DK, (h + 1) * A_DK)
            sfin_ref[d, h] = s_kv[blk, blk]


def _hgrn(zh, s0, lb_logits, norm_g, layer, seq):
    rows = zh.shape[0]
    n_seq = rows // seq
    per_seq_state = s0.shape[0] != 1
    W = A_WIDTH
    return pl.pallas_call(
        functools.partial(_hgrn_kernel, layer=layer, seq=seq),
        grid=(n_seq,),
        in_specs=[
            pl.BlockSpec((seq, HG_WIDTH), lambda b: (b, 0)),
            pl.BlockSpec((None, 2, W, W), (lambda b: (b, 0, 0, 0)) if per_seq_state else (lambda b: (0, 0, 0, 0))),
            _resident((DEPTH, W), lambda b: (0, 0)),
            _resident((None, 1, W), lambda b: (layer, 0, 0)),
        ],
        out_specs=[
            pl.BlockSpec((seq, W), lambda b: (b, 0)),
            pl.BlockSpec((None, 2, A_HEADS, A_DK, A_DK), lambda b: (b, 0, 0, 0, 0)),
        ],
        out_shape=[
            jax.ShapeDtypeStruct((rows, W), BF16),
            jax.ShapeDtypeStruct((n_seq, 2, A_HEADS, A_DK, A_DK), F32),
        ],
        scratch_shapes=[pltpu.VMEM((seq, W), F32), pltpu.VMEM((2, seq, W), BF16),
                        pltpu.VMEM((seq // HGRN_CHUNK, 2, W), F32),
                        pltpu.VMEM((seq // HGRN_CHUNK, 2, W, W), F32)],
        compiler_params=_params(("parallel",), 48),
        name="hgrn2",
    )(zh, s0, lb_logits, norm_g)


def _conv_kernel(up_ref, uc_ref, un_ref, w_ref, b_ref, g_ref, beta_ref, o_ref, sh_ref, *, tiles_per_seq):
    i = pl.program_id(0)
    T, H, S = CONV_TILE, CONV_HALO, SUBLANES
    P = T + 2 * H
    first = (i % tiles_per_seq) == 0
    last = (i % tiles_per_seq) == tiles_per_seq - 1
    sh_ref[0, 0:H, :] = jnp.where(first, 0.0, up_ref[...])
    sh_ref[0, H:H + T, :] = uc_ref[...]
    sh_ref[0, H + T:P, :] = jnp.where(last, 0.0, un_ref[...])
    for b in range(1, S):
        sh_ref[b, 0:P - S, :] = sh_ref[0, b:b + P - S, :]
    acc = jnp.zeros((T, B_WIDTH), F32)
    for j in range(CONV_K):
        start = H - CONV_PAD + j
        a, b = start // S, start % S
        acc = acc + w_ref[j:j + 1, :] * sh_ref[b, a * S:a * S + T, :]
    y = _layer_norm(acc + b_ref[...], g_ref[...], beta_ref[...])
    o_ref[...] = (y * _sigmoid(y)).astype(o_ref.dtype)


def _conv(u, conv_w, conv_b, ln_g, ln_b, layer, seq):
    rows = u.shape[0]
    T, H = CONV_TILE, CONV_HALO
    tiles_per_seq = seq // T
    ratio = T // H
    n_halo_blocks = rows // H
    return pl.pallas_call(
        functools.partial(_conv_kernel, tiles_per_seq=tiles_per_seq),
        grid=(rows // T,),
        in_specs=[
            pl.BlockSpec((H, B_WIDTH), lambda i: (jnp.maximum(i * ratio - 1, 0), 0)),
            pl.BlockSpec((T, B_WIDTH), lambda i: (i, 0)),
            pl.BlockSpec((H, B_WIDTH), lambda i: (jnp.minimum((i + 1) * ratio, n_halo_blocks - 1), 0)),
            _resident((None, CONV_K, B_WIDTH), lambda i: (layer, 0, 0)),
            _resident((None, 1, B_WIDTH), lambda i: (layer, 0, 0)),
            _resident((None, 1, B_WIDTH), lambda i: (layer, 0, 0)),
            _resident((None, 1, B_WIDTH), lambda i: (layer, 0, 0)),
        ],
        out_specs=pl.BlockSpec((T, B_WIDTH), lambda i: (i, 0)),
        out_shape=jax.ShapeDtypeStruct((rows, B_WIDTH), BF16),
        scratch_shapes=[pltpu.VMEM((SUBLANES, T + 2 * H, B_WIDTH), F32)],
        compiler_params=_params(("parallel",), 16),
        name="conv_module",
    )(u, u, u, conv_w, conv_b, ln_g, ln_b)


def _attn_kernel(q_ref, k_ref, v_ref, o_ref, kd_ref, va_ref, *, n_seq, single_q_tile):
    tq = q_ref.shape[0] // n_seq
    seq_k = k_ref.shape[0] // n_seq
    kb = min(seq_k, ATTN_KEY_BLOCK)
    group = C_HEADS // C_KV_HEADS
    half = lax.broadcasted_iota(jnp.int32, (1, LANES), 1) // HEAD_DIM

    def prepare():
        k = k_ref[...]
        k_swapped = pltpu.roll(k, HEAD_DIM, 1)
        v = v_ref[...]
        v_swapped = pltpu.roll(v, HEAD_DIM, 1)
        for g in range(C_KV_HEADS):
            kd_ref[g] = jnp.where(half == g, k, k_swapped).astype(BF16)
            va_ref[g, :, 0:LANES] = jnp.where(half == g, v, v_swapped).astype(BF16)
            va_ref[g, :, LANES:2 * LANES] = jnp.ones(v.shape, BF16)

    if single_q_tile:
        prepare()
    else:
        pl.when(pl.program_id(1) == 0)(prepare)

    n_blocks = seq_k // kb
    chains = [(sq, g) for sq in range(n_seq) for g in range(C_KV_HEADS)]
    units = [(c, b) for b in range(n_blocks) for c in range(len(chains))]
    q_stacked, run_max, acc = {}, {}, {}
    scores, probs, rescale = {}, {}, {}

    def stage_scores(u):
        c, b = units[u]
        sq, g = chains[c]
        if c not in q_stacked:
            heads = []
            for j in range(group):
                hq = g * group + j
                col = q_ref[sq * tq:(sq + 1) * tq, (hq // 2) * LANES:(hq // 2 + 1) * LANES]
                heads.append(jnp.where(half == hq % 2, col, jnp.zeros_like(col)))
            q_stacked[c] = jnp.concatenate(heads, axis=0)
        keys = slice(sq * seq_k + b * kb, sq * seq_k + (b + 1) * kb)
        scores[u] = _dot_nt(q_stacked[c], kd_ref[g, keys, :])

    def stage_exp(u):
        c, b = units[u]
        s = scores.pop(u)
        m_blk = jnp.max(s, axis=-1, keepdims=True)
        m_new = m_blk if b == 0 else jnp.maximum(run_max[c], m_blk)
        probs[u] = jnp.exp2(s - m_new).astype(BF16)
        if b > 0:
            rescale[u] = jnp.exp2(run_max[c] - m_new)
        run_max[c] = m_new

    def stage_values(u):
        c, b = units[u]
        sq, g = chains[c]
        keys = slice(sq * seq_k + b * kb, sq * seq_k + (b + 1) * kb)
        pv = _dot(probs.pop(u), va_ref[g, keys, :])
        acc[c] = pv if b == 0 else rescale.pop(u) * acc[c] + pv
        if b == n_blocks - 1:
            total = acc.pop(c)
            o = total[:, 0:LANES] / total[:, LANES:2 * LANES]
            for j in range(group):
                hq = g * group + j
                o_ref[sq * tq:(sq + 1) * tq, hq * HEAD_DIM:(hq + 1) * HEAD_DIM] = (
                    o[j * tq:(j + 1) * tq, (hq % 2) * HEAD_DIM:(hq % 2 + 1) * HEAD_DIM].astype(o_ref.dtype))

    for t in range(len(units) + 2):
        if t < len(units):
            stage_scores(t)
        if 0 <= t - 1 < len(units):
            stage_exp(t - 1)
        if 0 <= t - 2 < len(units):
            stage_values(t - 2)


def _attention(q, k, v, seq_q, seq_k, tq, seqs_per_step=1):
    n_seq = q.shape[0] // seq_q
    q_tiles = seq_q // tq
    assert seqs_per_step == 1 or q_tiles == 1
    q_rows, k_rows = seqs_per_step * tq, seqs_per_step * seq_k
    return pl.pallas_call(
        functools.partial(_attn_kernel, n_seq=seqs_per_step, single_q_tile=q_tiles == 1),
        grid=(n_seq // seqs_per_step, q_tiles),
        in_specs=[
            pl.BlockSpec((q_rows, C_WIDTH), lambda b, i: (b * q_tiles + i, 0)),
            pl.BlockSpec((k_rows, KV_WIDTH), lambda b, i: (b, 0)),
            pl.BlockSpec((k_rows, KV_WIDTH), lambda b, i: (b, 0)),
        ],
        out_specs=pl.BlockSpec((q_rows, C_WIDTH), lambda b, i: (b * q_tiles + i, 0)),
        out_shape=jax.ShapeDtypeStruct(q.shape, BF16),
        scratch_shapes=[pltpu.VMEM((C_KV_HEADS, k_rows, LANES), BF16),
                        pltpu.VMEM((C_KV_HEADS, k_rows, 2 * LANES), BF16)],
        compiler_params=_params(("parallel", "arbitrary"), 48),
        name="attention",
    )(q, k, v)


def _rope_tables(seq):
    pos = np.arange(seq)
    row_id = (pos // GRID_W).astype(np.float32)
    col_id = (pos % GRID_W).astype(np.float32)
    inv = (np.float32(ROPE_THETA) ** (-np.arange(ROPE_PAIRS, dtype=np.float32) / ROPE_PAIRS)).astype(np.float32)
    lane = np.arange(LANES) % HEAD_DIM
    use_col = (lane // (2 * ROPE_PAIRS)) == 1
    first = (lane % (2 * ROPE_PAIRS)) < ROPE_PAIRS
    freq = inv[lane % ROPE_PAIRS]
    ang = (np.where(use_col[None, :], col_id[:, None], row_id[:, None]) * freq[None, :]).astype(np.float32)
    cos, sin = np.cos(ang).astype(np.float32), np.sin(ang).astype(np.float32)
    zero = np.float32(0.0)
    return tuple(jnp.asarray(t) for t in (cos, np.where(first[None, :], -sin, zero), np.where(first[None, :], zero, sin)))


def _state_to_block_diag(s):
    eye = jnp.eye(A_HEADS, dtype=s.dtype)
    st = jnp.swapaxes(s, -1, -2)
    bd = st[..., :, :, None, :] * eye[:, None, :, None]
    return bd.reshape(s.shape[:-3] + (A_WIDTH, A_WIDTH))


def kernel(x_prompt, x_sample, cache_k, cache_v, state_hgrn, c, c_ctx, w_mod, b_mod, ln_g, ln_b, ffn1_w_in, ffn1_w_out, ffn2_w_in, ffn2_w_out, w_in, w_out, hgrn_lb_logits, hgrn_norm_g, conv_w, conv_b, conv_ln_g, conv_ln_b, q_norm_g, k_norm_g):
    batch, seq, _ = x_prompt.shape
    dec_batch, dec_seq, _ = x_sample.shape
    past = cache_k.shape[2]
    for tile in (ROW_TILE, ROW_TILE_WIDE):
        assert (batch * seq) % tile == 0 and dec_seq % tile == 0
    assert dec_batch + 1 <= SUBLANES
    assert KV_WIDTH == LANES and C_HEADS % (2 * C_KV_HEADS) == 0

    bf = lambda w: w.astype(BF16)
    ffn1_w_in, ffn1_w_out, w_in = map(bf, (ffn1_w_in, ffn1_w_out, w_in))

    conds = jnp.zeros((SUBLANES, D_MODEL), F32).at[0].set(c_ctx).at[1:1 + dec_batch].set(c)
    mod = _modulation(conds, w_mod, b_mod).reshape(DEPTH, SUBLANES, N_MOD, D_MODEL)

    ln_g3 = ln_g.reshape(DEPTH * 3, 1, D_MODEL)
    ln_b3 = ln_b.reshape(DEPTH * 3, 1, D_MODEL)
    qg = jnp.tile(q_norm_g, (1, C_HEADS)).reshape(DEPTH, 1, C_WIDTH)
    kg = jnp.tile(k_norm_g, (1, C_KV_HEADS)).reshape(DEPTH, 1, KV_WIDTH)
    norm_g = hgrn_norm_g.reshape(DEPTH, 1, A_WIDTH)
    conv_b3 = conv_b.reshape(DEPTH, 1, B_WIDTH)
    conv_g3 = conv_ln_g.reshape(DEPTH, 1, B_WIDTH)
    conv_beta3 = conv_ln_b.reshape(DEPTH, 1, B_WIDTH)
    tables = _rope_tables(dec_seq)
    zero_state = jnp.zeros((1, 2, A_WIDTH, A_WIDTH), F32)
    lat_state = _state_to_block_diag(state_hgrn)

    ctx_cond = lambda row: 0
    lat_cond = lambda row: 1 + row // dec_seq

    xp = x_prompt.reshape(batch * seq, D_MODEL)
    xs = x_sample.reshape(dec_batch * dec_seq, D_MODEL)
    ks, vs, ss = [], [], []
    for l in range(DEPTH):
        mod_l = mod[l]

        def layer_fn(x, cond, sq, rope_tables, s0, k_past, v_past):
            x, zh, u, q, k, v = _ffn(x, mod_l, cond, ffn1_w_in, ffn1_w_out, ln_g3, ln_b3, l, 0, ROW_TILE,
                                     mixer_in=(w_in, qg, kg, rope_tables, sq))
            o_a, s_fin = _hgrn(zh, s0, hgrn_lb_logits, norm_g, l, sq)
            o_b = _conv(u, conv_w, conv_b3, conv_g3, conv_beta3, l, sq)
            if k_past is None:
                o_c = _attention(q, k, v, sq, sq, sq, seqs_per_step=ATTN_SHORT_SEQS_PER_STEP)
            else:
                n = x.shape[0] // sq
                k_all = jnp.concatenate([k_past, k.reshape(n, sq, KV_WIDTH)], axis=1)
                v_all = jnp.concatenate([v_past, v.reshape(n, sq, KV_WIDTH)], axis=1)
                sk = k_all.shape[1]
                o_c = _attention(q, k_all.reshape(n * sk, KV_WIDTH), v_all.reshape(n * sk, KV_WIDTH),
                                 sq, sk, ATTN_Q_TILE_LONG)
            x, = _ffn(x, mod_l, cond, ffn2_w_in, ffn2_w_out, ln_g3, ln_b3, l, 2, ROW_TILE_WIDE,
                      mixer_out=(o_a, o_b, o_c, w_out))
            return x, k, v, s_fin

        xp, k_l, v_l, s_l = layer_fn(xp, ctx_cond, seq, None, zero_state, None, None)
        ks.append(k_l.reshape(batch, seq, C_KV_HEADS, HEAD_DIM))
        vs.append(v_l.reshape(batch, seq, C_KV_HEADS, HEAD_DIM))
        ss.append(s_l)
        xs, _, _, _ = layer_fn(xs, lat_cond, dec_seq, tables, lat_state[:, l],
                               cache_k[:, l].reshape(dec_batch, past, KV_WIDTH),
                               cache_v[:, l].reshape(dec_batch, past, KV_WIDTH))

    return (xp.reshape(batch, seq, D_MODEL), xs.reshape(dec_batch, dec_seq, D_MODEL),
            jnp.stack(ks, axis=1), jnp.stack(vs, axis=1), jnp.stack(ss, axis=1))
```

```python
import functools

import numpy as np
import jax
import jax.numpy as jnp
from jax import lax
from jax.experimental import pallas as pl
from jax.experimental.pallas import tpu as pltpu

F32 = jnp.float32
BF16 = jnp.bfloat16

D_MODEL = 1024
DEPTH = 2
GRID_W = 64
HEAD_DIM = 64
A_HEADS = 4
A_DK = 64
A_WIDTH = 256
B_WIDTH = 256
CONV_K = 31
CONV_PAD = 15
C_HEADS = 8
C_KV_HEADS = 2
C_WIDTH = 512
KV_WIDTH = C_KV_HEADS * HEAD_DIM
D_FF = 2816
ROPE_THETA = 10000.0
ROPE_PAIRS = 16
N_MOD = 9
ALPHA = (2 * DEPTH) ** 0.25
F_MIN = 1e-6
LOG2_E = 1.4426950408889634
IN_WIDTH = 5 * A_WIDTH + 2 * B_WIDTH + C_WIDTH + 2 * KV_WIDTH
HG_WIDTH = 5 * A_WIDTH

LANES = 128
SUBLANES = 8
VMEM_BYTES_V7X = 64 * 1024 * 1024

ROW_TILE = 512
ROW_TILE_WIDE = 512
FFN_CHUNK = 256
HGRN_CHUNK = 128
HGRN_BASE = 8
HGRN_NORM_ROWS = 256
CONV_TILE = 256
CONV_HALO = 16
ATTN_Q_TILE_LONG = 256
ATTN_KEY_BLOCK = 512
ATTN_SHORT_SEQS_PER_STEP = 4


def _params(semantics, vmem_mb):
    return pltpu.CompilerParams(dimension_semantics=semantics,
                                vmem_limit_bytes=min(vmem_mb * 1024 * 1024, VMEM_BYTES_V7X - (4 << 20)))


def _resident(block_shape, index_map):
    return pl.BlockSpec(block_shape, index_map, pipeline_mode=pl.Buffered(1))


def _sigmoid(x):
    return 1.0 / (1.0 + jnp.exp(-x))


def _layer_norm(y, g, b, eps=1e-5):
    mu = jnp.mean(y, axis=-1, keepdims=True)
    d = y - mu
    var = jnp.mean(d * d, axis=-1, keepdims=True)
    return d * lax.rsqrt(var + eps) * g + b


def _dot(a, b):
    return jnp.dot(a, b, preferred_element_type=F32)


def _dot_nt(a, b):
    return lax.dot_general(a, b, (((1,), (1,)), ((), ())), preferred_element_type=F32)


def _dot_tn(a, b):
    return lax.dot_general(a, b, (((0,), (0,)), ((), ())), preferred_element_type=F32)


def _segment_ones(width, seg):
    r = lax.broadcasted_iota(jnp.int32, (width, width), 0) // seg
    c = lax.broadcasted_iota(jnp.int32, (width, width), 1) // seg
    return (r == c).astype(BF16)


def _segment_sum(x, ones_bd):
    hi = x.astype(BF16)
    lo = (x - hi.astype(F32)).astype(BF16)
    return _dot(hi, ones_bd) + _dot(lo, ones_bd)


def _head_rms_norm(x, gain, eps=1e-6):
    ones_bd = _segment_ones(LANES, HEAD_DIM)
    cols = []
    for c in range(x.shape[1] // LANES):
        xc = x[:, c * LANES:(c + 1) * LANES]
        ms = _segment_sum(xc * xc, ones_bd) * (1.0 / HEAD_DIM)
        cols.append(xc * lax.rsqrt(ms + eps) * gain[:, c * LANES:(c + 1) * LANES])
    return cols[0] if len(cols) == 1 else jnp.concatenate(cols, axis=1)


def _mod_kernel(c_ref, w_ref, b_ref, o_ref):
    c = c_ref[...]
    s = (c * _sigmoid(c)).astype(BF16)
    o_ref[...] = _dot(s, w_ref[...].astype(BF16)) + b_ref[...]


def _modulation(conds, w_mod, b_mod):
    tn = D_MODEL
    return pl.pallas_call(
        _mod_kernel,
        grid=(DEPTH, N_MOD * D_MODEL // tn),
        in_specs=[
            pl.BlockSpec((SUBLANES, D_MODEL), lambda l, j: (0, 0)),
            pl.BlockSpec((None, D_MODEL, tn), lambda l, j: (l, 0, j)),
            pl.BlockSpec((None, 1, tn), lambda l, j: (l, 0, j)),
        ],
        out_specs=pl.BlockSpec((None, SUBLANES, tn), lambda l, j: (l, 0, j)),
        out_shape=jax.ShapeDtypeStruct((DEPTH, SUBLANES, N_MOD * D_MODEL), F32),
        compiler_params=_params(("arbitrary", "arbitrary"), 32),
        name="modulation",
    )(conds, w_mod, b_mod.reshape(DEPTH, 1, N_MOD * D_MODEL))


def _ffn_kernel(*refs, sub, with_mixer_out, mixer_in_rope):
    refs = list(refs)
    x_ref = refs.pop(0)
    if with_mixer_out:
        oa_ref, ob_ref, oc_ref = refs[:3]
        del refs[:3]
    mod_ref = refs.pop(0)
    if with_mixer_out:
        wmix_ref, gmix_ref, bmix_ref = refs[:3]
        del refs[:3]
    wg_ref, wu_ref, wo_ref, g_ref, b_ref = refs[:5]
    del refs[:5]
    if mixer_in_rope is not None:
        n_in = 6 if mixer_in_rope else 3
        mixer_in_refs, refs = refs[:n_in], refs[n_in:]
    o_ref = refs.pop(0)
    mixer_in_outs = refs

    if with_mixer_out:
        y = _dot(oa_ref[...], wmix_ref[0:A_WIDTH, :].astype(BF16))
        y = y + _dot(ob_ref[...], wmix_ref[A_WIDTH:A_WIDTH + B_WIDTH, :].astype(BF16))
        y = y + _dot(oc_ref[...], wmix_ref[A_WIDTH + B_WIDTH:, :].astype(BF16))
        x = _layer_norm(ALPHA * x_ref[...] + mod_ref[5:6, :] * y, gmix_ref[...], bmix_ref[...])
    else:
        x = x_ref[...]
    shift = mod_ref[3 * sub:3 * sub + 1, :]
    scale = mod_ref[3 * sub + 1:3 * sub + 2, :]
    gate = mod_ref[3 * sub + 2:3 * sub + 3, :]
    h = (x * (1.0 + scale) + shift).astype(BF16)
    acc = jnp.zeros(x.shape, F32)
    for j in range(D_FF // FFN_CHUNK):
        cols = slice(j * FFN_CHUNK, (j + 1) * FFN_CHUNK)
        gt = _dot(h, wg_ref[:, cols].astype(BF16))
        up = _dot(h, wu_ref[:, cols].astype(BF16))
        act = (gt * _sigmoid(gt) * up).astype(BF16)
        acc = acc + _dot(act, wo_ref[cols, :].astype(BF16))
    y = ALPHA * x + 0.5 * gate * acc
    x_new = _layer_norm(y, g_ref[...], b_ref[...])
    o_ref[...] = x_new
    if mixer_in_rope is not None:
        _mixer_in(x_new, mod_ref, *mixer_in_refs, *mixer_in_outs)


def _ffn(x, mod_l, cond_of_row, w_in, w_out, ln_g, ln_b, layer, sub, tile, mixer_out=None, mixer_in=None):
    rows = x.shape[0]
    row_spec = lambda width: pl.BlockSpec((tile, width), lambda i: (i, 0))
    cond_of_block = lambda i: cond_of_row(i * tile)
    ln_spec = lambda idx: _resident((None, 1, D_MODEL), lambda i: (layer * 3 + idx, 0, 0))
    in_specs, args = [row_spec(D_MODEL)], [x]
    if mixer_out is not None:
        o_a, o_b, o_c, w_mix = mixer_out
        in_specs += [row_spec(A_WIDTH), row_spec(B_WIDTH), row_spec(C_WIDTH)]
        args += [o_a, o_b, o_c]
    in_specs.append(pl.BlockSpec((None, N_MOD, D_MODEL), lambda i: (cond_of_block(i), 0, 0)))
    args.append(mod_l)
    if mixer_out is not None:
        in_specs += [_resident((None, D_MODEL, D_MODEL), lambda i: (layer, 0, 0)), ln_spec(1), ln_spec(1)]
        args += [w_mix, ln_g, ln_b]
    in_specs += [
        _resident((None, D_MODEL, D_FF), lambda i: (layer, 0, 0)),
        _resident((None, D_MODEL, D_FF), lambda i: (layer, 0, 1)),
        _resident((None, D_FF, D_MODEL), lambda i: (layer, 0, 0)),
        ln_spec(sub), ln_spec(sub),
    ]
    args += [w_in, w_in, w_out, ln_g, ln_b]
    out_specs = [row_spec(D_MODEL)]
    out_shape = [jax.ShapeDtypeStruct((rows, D_MODEL), F32)]
    rope = None
    if mixer_in is not None:
        w_mix_in, qg, kg, rope_tables, seq = mixer_in
        rope = rope_tables is not None
        in_specs += [
            _resident((None, D_MODEL, IN_WIDTH), lambda i: (layer, 0, 0)),
            _resident((None, 1, C_WIDTH), lambda i: (layer, 0, 0)),
            _resident((None, 1, KV_WIDTH), lambda i: (layer, 0, 0)),
        ]
        args += [w_mix_in, qg, kg]
        if rope:
            per_seq = seq // tile
            in_specs += [pl.BlockSpec((tile, LANES), lambda i: (i % per_seq, 0))] * 3
            args += list(rope_tables)
        widths = ((HG_WIDTH, F32), (B_WIDTH, F32), (C_WIDTH, BF16), (KV_WIDTH, F32), (KV_WIDTH, F32))
        out_specs += [row_spec(w) for w, _ in widths]
        out_shape += [jax.ShapeDtypeStruct((rows, w), dt) for w, dt in widths]
    return pl.pallas_call(
        functools.partial(_ffn_kernel, sub=sub, with_mixer_out=mixer_out is not None, mixer_in_rope=rope),
        grid=(rows // tile,),
        in_specs=in_specs,
        out_specs=out_specs,
        out_shape=out_shape,
        compiler_params=_params(("parallel",), 56),
        name=f"ffn{sub // 2 + 1}",
    )(*args)


def _mixer_in(x, mod_ref, w_ref, qg_ref, kg_ref, *refs):
    rope = len(refs) == 8
    if rope:
        cos_ref, s1_ref, s2_ref = refs[:3]
    zh_ref, u_ref, q_ref, k_ref, v_ref = refs[-5:]
    h = (x * (1.0 + mod_ref[4:5, :]) + mod_ref[3:4, :]).astype(BF16)

    zh_ref[...] = _dot(h, w_ref[:, 0:HG_WIDTH])
    o = HG_WIDTH
    glu_a = _dot(h, w_ref[:, o:o + B_WIDTH])
    glu_b = _dot(h, w_ref[:, o + B_WIDTH:o + 2 * B_WIDTH])
    u_ref[...] = glu_a * _sigmoid(glu_b)
    o += 2 * B_WIDTH
    cq = _dot(h, w_ref[:, o:o + C_WIDTH])
    ck = _dot(h, w_ref[:, o + C_WIDTH:o + C_WIDTH + KV_WIDTH])
    v_ref[...] = _dot(h, w_ref[:, o + C_WIDTH + KV_WIDTH:o + C_WIDTH + 2 * KV_WIDTH])

    qn = _head_rms_norm(cq, qg_ref[...])
    kn = _head_rms_norm(ck, kg_ref[...])
    if rope:
        cos, s1, s2 = cos_ref[...], s1_ref[...], s2_ref[...]

        def rot(t):
            cols = []
            for c in range(t.shape[1] // LANES):
                tc = t[:, c * LANES:(c + 1) * LANES]
                cols.append(tc * cos + pltpu.roll(tc, LANES - ROPE_PAIRS, 1) * s1
                            + pltpu.roll(tc, ROPE_PAIRS, 1) * s2)
            return cols[0] if len(cols) == 1 else jnp.concatenate(cols, axis=1)

        qn, kn = rot(qn), rot(kn)
    q_ref[...] = (qn * (HEAD_DIM ** -0.5 * LOG2_E)).astype(BF16)
    k_ref[...] = kn


def _cumsum_rows(x, reverse):
    n, w = x.shape
    row = lax.broadcasted_iota(jnp.int32, (SUBLANES, w), 0)
    groups = []
    for g in range(n // SUBLANES):
        grp = x[g * SUBLANES:(g + 1) * SUBLANES, :]
        s = 1
        while s < SUBLANES:
            if reverse:
                grp = grp + jnp.where(row < SUBLANES - s, pltpu.roll(grp, SUBLANES - s, 0), 0.0)
            else:
                grp = grp + jnp.where(row >= s, pltpu.roll(grp, s, 0), 0.0)
            s *= 2
        groups.append(grp)
    order = range(len(groups) - 1, -1, -1) if reverse else range(len(groups))
    carry = None
    for g in order:
        if carry is not None:
            groups[g] = groups[g] + carry
        carry = groups[g][0:1, :] if reverse else groups[g][SUBLANES - 1:SUBLANES, :]
    return jnp.concatenate(groups, axis=0)


def _hgrn_kernel(zh_ref, s0_ref, lbl_ref, ng_ref, o_ref, sfin_ref, acc_ref, qe_ref, decay_ref, kv_ref, *,
                 layer, seq):
    C, W = HGRN_CHUNK, A_WIDTH
    n_chunks = seq // C

    lg = lbl_ref[...]
    e = jnp.exp(lg - jnp.max(lg, axis=0, keepdims=True))
    soft = e / jnp.sum(e, axis=0, keepdims=True)
    lb = jnp.zeros((1, W), F32)
    for j in range(1, layer + 1):
        lb = lb + soft[j:j + 1, :]

    lane_head = lax.broadcasted_iota(jnp.int32, (1, W), 1) // A_DK
    head_masks = [lane_head == h for h in range(A_HEADS)]
    t_idx = lax.broadcasted_iota(jnp.int32, (C, A_HEADS * C), 0)
    s_idx = lax.broadcasted_iota(jnp.int32, (C, A_HEADS * C), 1) % C
    pair_level = t_idx ^ s_idx
    key_not_after = s_idx <= t_idx
    key_not_before = s_idx >= t_idx
    bd_mask = (lax.broadcasted_iota(jnp.int32, (W, W), 0) // A_DK
               == lax.broadcasted_iota(jnp.int32, (W, W), 1) // A_DK)
    def stack_heads(t_bf):
        return jnp.concatenate([jnp.where(m, t_bf, jnp.zeros_like(t_bf)) for m in head_masks], axis=0)

    def chunk_rows(n):
        return pl.ds(pl.multiple_of(n * C, C), C)

    def gates(rows, reverse):
        zf = zh_ref[rows, (3 if reverse else 2) * W:(4 if reverse else 3) * W]
        f = lb + (1.0 - lb) * _sigmoid(zf)
        return 1.0 - f, _cumsum_rows(jnp.log(jnp.maximum(f, F_MIN)), reverse) * LOG2_E

    def intra_body(n, carry):
        rows = chunk_rows(n)
        q = zh_ref[rows, 0:W]
        v_bf = zh_ref[rows, W:2 * W].astype(BF16)
        k_f, a_f = gates(rows, False)
        k_b, a_b = gates(rows, True)

        scores = jnp.zeros((C, A_HEADS * C), F32)
        h = C // 2
        while h >= HGRN_BASE:
            split = lambda t: t.reshape(C // (2 * h), 2, h, W)
            q4, af4, ab4, kf4, kb4 = split(q), split(a_f), split(a_b), split(k_f), split(k_b)
            end_first = af4[:, 0, h - 1:h, :]
            start_second = ab4[:, 1, 0:1, :]
            q_fwd = q4[:, 1] * jnp.exp2(af4[:, 1] - end_first)
            k_fwd = kf4[:, 0] * jnp.exp2(end_first - af4[:, 0])
            q_bwd = q4[:, 0] * jnp.exp2(ab4[:, 0] - start_second)
            k_bwd = kb4[:, 1] * jnp.exp2(start_second - ab4[:, 1])
            qt = jnp.stack([q_bwd, q_fwd], axis=1).reshape(C, W).astype(BF16)
            kt = jnp.stack([k_fwd, k_bwd], axis=1).reshape(C, W).astype(BF16)
            scores = jnp.where(pair_level < 2 * h, _dot_nt(qt, stack_heads(kt)), scores)
            h //= 2
        hb = HGRN_BASE
        block = lambda t: t.reshape(C // hb, hb, W)
        base = []
        for a, kk, ref_at, keep in ((a_f, k_f, hb // 2 - 1, key_not_after), (a_b, k_b, hb // 2, key_not_before)):
            a3 = block(a)
            ref_row = a3[:, ref_at:ref_at + 1, :]
            qt = (block(q) * jnp.exp2(a3 - ref_row)).reshape(C, W).astype(BF16)
            kt = (block(kk) * jnp.exp2(ref_row - a3)).reshape(C, W).astype(BF16)
            base.append(jnp.where(keep, _dot_nt(qt, stack_heads(kt)), 0.0))
        scores = jnp.where(pair_level < hb, base[0] + base[1], scores)
        acc_ref[rows, :] = _dot(scores.astype(BF16), stack_heads(v_bf))

        for d, (a, kk, last) in enumerate(((a_f, k_f, C - 1), (a_b, k_b, 0))):
            a_last = a[last:last + 1, :]
            qe_ref[d, rows, :] = (q * jnp.exp2(a)).astype(BF16)
            decay_ref[n, d:d + 1, :] = a_last
            kv_ref[n, d] = jnp.where(bd_mask, _dot_tn(v_bf, (kk * jnp.exp2(a_last - a)).astype(BF16)), 0.0)
        return carry

    lax.fori_loop(0, n_chunks, intra_body, 0, unroll=2)

    def state_body(i, carry):
        new = []
        for d, (st, n) in enumerate(zip(carry, (i, n_chunks - 1 - i))):
            rows = chunk_rows(n)
            acc_ref[rows, :] += _dot_nt(qe_ref[d, rows, :], st.astype(BF16))
            new.append(st * jnp.exp2(decay_ref[n, d:d + 1, :]) + kv_ref[n, d])
        return tuple(new)

    finals = lax.fori_loop(0, n_chunks, state_body, (s0_ref[0], s0_ref[1]), unroll=min(n_chunks, 4))

    ones_bd = _segment_ones(LANES, A_DK)

    def norm_body(n, carry):
        rows = pl.ds(pl.multiple_of(n * HGRN_NORM_ROWS, HGRN_NORM_ROWS), HGRN_NORM_ROWS)
        tot = acc_ref[rows, :]
        zg = zh_ref[rows, 4 * W:5 * W]
        cols = []
        for c in range(W // LANES):
            lanes = slice(c * LANES, (c + 1) * LANES)
            tc = tot[:, lanes]
            ms = _segment_sum(tc * tc, ones_bd) * (1.0 / A_DK)
            cols.append(tc * lax.rsqrt(ms + 1e-6) * ng_ref[:, lanes])
        on = jnp.concatenate(cols, axis=1)
        o_ref[rows, :] = (on * (zg * _sigmoid(zg))).astype(o_ref.dtype)
        return carry

    lax.fori_loop(0, seq // HGRN_NORM_ROWS, norm_body, 0)

    for d, st in enumerate(finals):
        s_kv = st.T
        for h in range(A_HEADS):
            blk = slice(h * A_DK, (h + 1) * A_DK)
            sfin_ref[d, h] = s_kv[blk, blk]


def _hgrn(zh, s0, lb_logits, norm_g, layer, seq):
    rows = zh.shape[0]
    n_seq = rows // seq
    per_seq_state = s0.shape[0] != 1
    W = A_WIDTH
    return pl.pallas_call(
        functools.partial(_hgrn_kernel, layer=layer, seq=seq),
        grid=(n_seq,),
        in_specs=[
            pl.BlockSpec((seq, HG_WIDTH), lambda b: (b, 0)),
            pl.BlockSpec((None, 2, W, W), (lambda b: (b, 0, 0, 0)) if per_seq_state else (lambda b: (0, 0, 0, 0))),
            _resident((DEPTH, W), lambda b: (0, 0)),
            _resident((None, 1, W), lambda b: (layer, 0, 0)),
        ],
        out_specs=[
            pl.BlockSpec((seq, W), lambda b: (b, 0)),
            pl.BlockSpec((None, 2, A_HEADS, A_DK, A_DK), lambda b: (b, 0, 0, 0, 0)),
        ],
        out_shape=[
            jax.ShapeDtypeStruct((rows, W), BF16),
            jax.ShapeDtypeStruct((n_seq, 2, A_HEADS, A_DK, A_DK), F32),
        ],
        scratch_shapes=[pltpu.VMEM((seq, W), F32), pltpu.VMEM((2, seq, W), BF16),
                        pltpu.VMEM((seq // HGRN_CHUNK, 2, W), F32),
                        pltpu.VMEM((seq // HGRN_CHUNK, 2, W, W), F32)],
        compiler_params=_params(("parallel",), 48),
        name="hgrn2",
    )(zh, s0, lb_logits, norm_g)


def _conv_kernel(up_ref, uc_ref, un_ref, w_ref, b_ref, g_ref, beta_ref, o_ref, sh_ref, *, tiles_per_seq):
    i = pl.program_id(0)
    T, H, S = CONV_TILE, CONV_HALO, SUBLANES
    P = T + 2 * H
    first = (i % tiles_per_seq) == 0
    last = (i % tiles_per_seq) == tiles_per_seq - 1
    sh_ref[0, 0:H, :] = jnp.where(first, 0.0, up_ref[...])
    sh_ref[0, H:H + T, :] = uc_ref[...]
    sh_ref[0, H + T:P, :] = jnp.where(last, 0.0, un_ref[...])
    for b in range(1, S):
        sh_ref[b, 0:P - S, :] = sh_ref[0, b:b + P - S, :]
    acc = jnp.zeros((T, B_WIDTH), F32)
    for j in range(CONV_K):
        start = H - CONV_PAD + j
        a, b = start // S, start % S
        acc = acc + w_ref[j:j + 1, :] * sh_ref[b, a * S:a * S + T, :]
    y = _layer_norm(acc + b_ref[...], g_ref[...], beta_ref[...])
    o_ref[...] = (y * _sigmoid(y)).astype(o_ref.dtype)


def _conv(u, conv_w, conv_b, ln_g, ln_b, layer, seq):
    rows = u.shape[0]
    T, H = CONV_TILE, CONV_HALO
    tiles_per_seq = seq // T
    ratio = T // H
    n_halo_blocks = rows // H
    return pl.pallas_call(
        functools.partial(_conv_kernel, tiles_per_seq=tiles_per_seq),
        grid=(rows // T,),
        in_specs=[
            pl.BlockSpec((H, B_WIDTH), lambda i: (jnp.maximum(i * ratio - 1, 0), 0)),
            pl.BlockSpec((T, B_WIDTH), lambda i: (i, 0)),
            pl.BlockSpec((H, B_WIDTH), lambda i: (jnp.minimum((i + 1) * ratio, n_halo_blocks - 1), 0)),
            _resident((None, CONV_K, B_WIDTH), lambda i: (layer, 0, 0)),
            _resident((None, 1, B_WIDTH), lambda i: (layer, 0, 0)),
            _resident((None, 1, B_WIDTH), lambda i: (layer, 0, 0)),
            _resident((None, 1, B_WIDTH), lambda i: (layer, 0, 0)),
        ],
        out_specs=pl.BlockSpec((T, B_WIDTH), lambda i: (i, 0)),
        out_shape=jax.ShapeDtypeStruct((rows, B_WIDTH), BF16),
        scratch_shapes=[pltpu.VMEM((SUBLANES, T + 2 * H, B_WIDTH), F32)],
        compiler_params=_params(("parallel",), 16),
        name="conv_module",
    )(u, u, u, conv_w, conv_b, ln_g, ln_b)


def _attn_kernel(*refs, n_seq, single_q_tile, with_past):
    if with_past:
        q_ref, k_ref, v_ref, kp_ref, vp_ref, o_ref, kd_ref, va_ref = refs
        sources = ((kp_ref, vp_ref), (k_ref, v_ref))
    else:
        q_ref, k_ref, v_ref, o_ref, kd_ref, va_ref = refs
        sources = ((k_ref, v_ref),)
    tq = q_ref.shape[0] // n_seq
    seq_k = kd_ref.shape[1] // n_seq
    kb = min(seq_k, ATTN_KEY_BLOCK)
    group = C_HEADS // C_KV_HEADS
    half = lax.broadcasted_iota(jnp.int32, (1, LANES), 1) // HEAD_DIM

    def prepare():
        start = 0
        for ks_ref, vs_ref in sources:
            rows = slice(start, start + ks_ref.shape[0])
            start += ks_ref.shape[0]
            k = ks_ref[...]
            k_swapped = pltpu.roll(k, HEAD_DIM, 1)
            v = vs_ref[...]
            v_swapped = pltpu.roll(v, HEAD_DIM, 1)
            for g in range(C_KV_HEADS):
                kd_ref[g, rows, :] = jnp.where(half == g, k, k_swapped).astype(BF16)
                va_ref[g, rows, 0:LANES] = jnp.where(half == g, v, v_swapped).astype(BF16)
                va_ref[g, rows, LANES:2 * LANES] = jnp.ones(v.shape, BF16)

    if single_q_tile:
        prepare()
    else:
        pl.when(pl.program_id(1) == 0)(prepare)

    n_blocks = seq_k // kb
    chains = [(sq, g) for sq in range(n_seq) for g in range(C_KV_HEADS)]
    units = [(c, b) for b in range(n_blocks) for c in range(len(chains))]
    q_stacked, run_max, acc = {}, {}, {}
    scores, probs, rescale = {}, {}, {}

    def stage_scores(u):
        c, b = units[u]
        sq, g = chains[c]
        if c not in q_stacked:
            heads = []
            for j in range(group):
                hq = g * group + j
                col = q_ref[sq * tq:(sq + 1) * tq, (hq // 2) * LANES:(hq // 2 + 1) * LANES]
                heads.append(jnp.where(half == hq % 2, col, jnp.zeros_like(col)))
            q_stacked[c] = jnp.concatenate(heads, axis=0)
        keys = slice(sq * seq_k + b * kb, sq * seq_k + (b + 1) * kb)
        scores[u] = _dot_nt(q_stacked[c], kd_ref[g, keys, :])

    def stage_exp(u):
        c, b = units[u]
        s = scores.pop(u)
        m_blk = jnp.max(s, axis=-1, keepdims=True)
        m_new = m_blk if b == 0 else jnp.maximum(run_max[c], m_blk)
        probs[u] = jnp.exp2(s - m_new).astype(BF16)
        if b > 0:
            rescale[u] = jnp.exp2(run_max[c] - m_new)
        run_max[c] = m_new

    def stage_values(u):
        c, b = units[u]
        sq, g = chains[c]
        keys = slice(sq * seq_k + b * kb, sq * seq_k + (b + 1) * kb)
        pv = _dot(probs.pop(u), va_ref[g, keys, :])
        acc[c] = pv if b == 0 else rescale.pop(u) * acc[c] + pv
        if b == n_blocks - 1:
            total = acc.pop(c)
            o = total[:, 0:LANES] / total[:, LANES:2 * LANES]
            for j in range(group):
                hq = g * group + j
                o_ref[sq * tq:(sq + 1) * tq, hq * HEAD_DIM:(hq + 1) * HEAD_DIM] = (
                    o[j * tq:(j + 1) * tq, (hq % 2) * HEAD_DIM:(hq % 2 + 1) * HEAD_DIM].astype(o_ref.dtype))

    for t in range(len(units) + 2):
        if t < len(units):
            stage_scores(t)
        if 0 <= t - 1 < len(units):
            stage_exp(t - 1)
        if 0 <= t - 2 < len(units):
            stage_values(t - 2)


def _attention(q, k, v, seq, tq, seqs_per_step=1, past=None):
    n_seq = q.shape[0] // seq
    q_tiles = seq // tq
    assert seqs_per_step == 1 or (q_tiles == 1 and past is None)
    q_rows, k_rows = seqs_per_step * tq, seqs_per_step * seq
    in_specs = [
        pl.BlockSpec((q_rows, C_WIDTH), lambda b, i: (b * q_tiles + i, 0)),
        pl.BlockSpec((k_rows, KV_WIDTH), lambda b, i: (b, 0)),
        pl.BlockSpec((k_rows, KV_WIDTH), lambda b, i: (b, 0)),
    ]
    args = [q, k, v]
    keys_per_step = k_rows
    if past is not None:
        n_past = past[0].shape[1]
        in_specs += [pl.BlockSpec((None, n_past, KV_WIDTH), lambda b, i: (b, 0, 0))] * 2
        args += list(past)
        keys_per_step += n_past
    return pl.pallas_call(
        functools.partial(_attn_kernel, n_seq=seqs_per_step, single_q_tile=q_tiles == 1, with_past=past is not None),
        grid=(n_seq // seqs_per_step, q_tiles),
        in_specs=in_specs,
        out_specs=pl.BlockSpec((q_rows, C_WIDTH), lambda b, i: (b * q_tiles + i, 0)),
        out_shape=jax.ShapeDtypeStruct(q.shape, BF16),
        scratch_shapes=[pltpu.VMEM((C_KV_HEADS, keys_per_step, LANES), BF16),
                        pltpu.VMEM((C_KV_HEADS, keys_per_step, 2 * LANES), BF16)],
        compiler_params=_params(("parallel", "arbitrary"), 48),
        name="attention",
    )(*args)


def _rope_tables(seq):
    pos = np.arange(seq)
    row_id = (pos // GRID_W).astype(np.float32)
    col_id = (pos % GRID_W).astype(np.float32)
    inv = (np.float32(ROPE_THETA) ** (-np.arange(ROPE_PAIRS, dtype=np.float32) / ROPE_PAIRS)).astype(np.float32)
    lane = np.arange(LANES) % HEAD_DIM
    use_col = (lane // (2 * ROPE_PAIRS)) == 1
    first = (lane % (2 * ROPE_PAIRS)) < ROPE_PAIRS
    freq = inv[lane % ROPE_PAIRS]
    ang = (np.where(use_col[None, :], col_id[:, None], row_id[:, None]) * freq[None, :]).astype(np.float32)
    cos, sin = np.cos(ang).astype(np.float32), np.sin(ang).astype(np.float32)
    zero = np.float32(0.0)
    return tuple(jnp.asarray(t) for t in (cos, np.where(first[None, :], -sin, zero), np.where(first[None, :], zero, sin)))


def _state_to_block_diag(s):
    eye = jnp.eye(A_HEADS, dtype=s.dtype)
    st = jnp.swapaxes(s, -1, -2)
    bd = st[..., :, :, None, :] * eye[:, None, :, None]
    return bd.reshape(s.shape[:-3] + (A_WIDTH, A_WIDTH))


def kernel(x_prompt, x_sample, cache_k, cache_v, state_hgrn, c, c_ctx, w_mod, b_mod, ln_g, ln_b, ffn1_w_in, ffn1_w_out, ffn2_w_in, ffn2_w_out, w_in, w_out, hgrn_lb_logits, hgrn_norm_g, conv_w, conv_b, conv_ln_g, conv_ln_b, q_norm_g, k_norm_g):
    batch, seq, _ = x_prompt.shape
    dec_batch, dec_seq, _ = x_sample.shape
    past = cache_k.shape[2]
    for tile in (ROW_TILE, ROW_TILE_WIDE):
        assert (batch * seq) % tile == 0 and dec_seq % tile == 0
    assert dec_batch + 1 <= SUBLANES
    assert KV_WIDTH == LANES and C_HEADS % (2 * C_KV_HEADS) == 0

    bf = lambda w: w.astype(BF16)
    ffn1_w_in, ffn1_w_out, w_in = map(bf, (ffn1_w_in, ffn1_w_out, w_in))

    conds = jnp.zeros((SUBLANES, D_MODEL), F32).at[0].set(c_ctx).at[1:1 + dec_batch].set(c)
    mod = _modulation(conds, w_mod, b_mod).reshape(DEPTH, SUBLANES, N_MOD, D_MODEL)

    ln_g3 = ln_g.reshape(DEPTH * 3, 1, D_MODEL)
    ln_b3 = ln_b.reshape(DEPTH * 3, 1, D_MODEL)
    qg = jnp.tile(q_norm_g, (1, C_HEADS)).reshape(DEPTH, 1, C_WIDTH)
    kg = jnp.tile(k_norm_g, (1, C_KV_HEADS)).reshape(DEPTH, 1, KV_WIDTH)
    norm_g = hgrn_norm_g.reshape(DEPTH, 1, A_WIDTH)
    conv_b3 = conv_b.reshape(DEPTH, 1, B_WIDTH)
    conv_g3 = conv_ln_g.reshape(DEPTH, 1, B_WIDTH)
    conv_beta3 = conv_ln_b.reshape(DEPTH, 1, B_WIDTH)
    tables = _rope_tables(dec_seq)
    zero_state = jnp.zeros((1, 2, A_WIDTH, A_WIDTH), F32)
    lat_state = _state_to_block_diag(state_hgrn)

    ctx_cond = lambda row: 0
    lat_cond = lambda row: 1 + row // dec_seq

    xp = x_prompt.reshape(batch * seq, D_MODEL)
    xs = x_sample.reshape(dec_batch * dec_seq, D_MODEL)
    ks, vs, ss = [], [], []
    for l in range(DEPTH):
        mod_l = mod[l]

        def layer_fn(x, cond, sq, rope_tables, s0, k_past, v_past):
            x, zh, u, q, k, v = _ffn(x, mod_l, cond, ffn1_w_in, ffn1_w_out, ln_g3, ln_b3, l, 0, ROW_TILE,
                                     mixer_in=(w_in, qg, kg, rope_tables, sq))
            o_a, s_fin = _hgrn(zh, s0, hgrn_lb_logits, norm_g, l, sq)
            o_b = _conv(u, conv_w, conv_b3, conv_g3, conv_beta3, l, sq)
            if k_past is None:
                o_c = _attention(q, k, v, sq, sq, seqs_per_step=ATTN_SHORT_SEQS_PER_STEP)
            else:
                o_c = _attention(q, k, v, sq, ATTN_Q_TILE_LONG, past=(k_past, v_past))
            x, = _ffn(x, mod_l, cond, ffn2_w_in, ffn2_w_out, ln_g3, ln_b3, l, 2, ROW_TILE_WIDE,
                      mixer_out=(o_a, o_b, o_c, w_out))
            return x, k, v, s_fin

        xp, k_l, v_l, s_l = layer_fn(xp, ctx_cond, seq, None, zero_state, None, None)
        ks.append(k_l.reshape(batch, seq, C_KV_HEADS, HEAD_DIM))
        vs.append(v_l.reshape(batch, seq, C_KV_HEADS, HEAD_DIM))
        ss.append(s_l)
        xs, _, _, _ = layer_fn(xs, lat_cond, dec_seq, tables, lat_state[:, l],
                               cache_k[:, l].reshape(dec_batch, past, KV_WIDTH),
                               cache_v[:, l].reshape(dec_batch, past, KV_WIDTH))

    return (xp.reshape(batch, seq, D_MODEL), xs.reshape(dec_batch, dec_seq, D_MODEL),
            jnp.stack(ks, axis=1), jnp.stack(vs, axis=1), jnp.stack(ss, axis=1))
```

```python
import functools

import numpy as np
import jax
import jax.numpy as jnp
from jax import lax
from jax.experimental import pallas as pl
from jax.experimental.pallas import tpu as pltpu

F32 = jnp.float32
BF16 = jnp.bfloat16

D_MODEL = 1024
DEPTH = 2
GRID_W = 64
HEAD_DIM = 64
A_HEADS = 4
A_DK = 64
A_WIDTH = 256
B_WIDTH = 256
CONV_K = 31
CONV_PAD = 15
C_HEADS = 8
C_KV_HEADS = 2
C_WIDTH = 512
KV_WIDTH = C_KV_HEADS * HEAD_DIM
D_FF = 2816
ROPE_THETA = 10000.0
ROPE_PAIRS = 16
N_MOD = 9
ALPHA = (2 * DEPTH) ** 0.25
F_MIN = 1e-6
LOG2_E = 1.4426950408889634
IN_WIDTH = 5 * A_WIDTH + 2 * B_WIDTH + C_WIDTH + 2 * KV_WIDTH
HG_WIDTH = 5 * A_WIDTH

LANES = 128
SUBLANES = 8
VMEM_BYTES_V7X = 64 * 1024 * 1024
VMEM_RESERVE_BYTES = 4 * 1024 * 1024

ROW_TILE = 512
FFN_CHUNK = 256
HGRN_CHUNK = 128
HGRN_BASE = 8
HGRN_NORM_ROWS = 256
HGRN_SHORT_SEQS_PER_STEP = 4
CONV_TILE = 256
CONV_HALO = 16
ATTN_Q_TILE_LONG = 256
ATTN_KEY_BLOCK = 512
ATTN_SHORT_SEQS_PER_STEP = 4


def _params(semantics, vmem_mb):
    return pltpu.CompilerParams(dimension_semantics=semantics,
                                vmem_limit_bytes=min(vmem_mb << 20, VMEM_BYTES_V7X - VMEM_RESERVE_BYTES))


def _resident(block_shape, index_map):
    return pl.BlockSpec(block_shape, index_map, pipeline_mode=pl.Buffered(1))


def _sigmoid(x):
    return 1.0 / (1.0 + jnp.exp(-x))


def _layer_norm(y, g, b, eps=1e-5):
    mu = jnp.mean(y, axis=-1, keepdims=True)
    d = y - mu
    var = jnp.mean(d * d, axis=-1, keepdims=True)
    return d * lax.rsqrt(var + eps) * g + b


def _dot(a, b):
    return jnp.dot(a, b, preferred_element_type=F32)


def _dot_nt(a, b):
    return lax.dot_general(a, b, (((1,), (1,)), ((), ())), preferred_element_type=F32)


def _dot_tn(a, b):
    return lax.dot_general(a, b, (((0,), (0,)), ((), ())), preferred_element_type=F32)


def _segment_ones(width, seg):
    r = lax.broadcasted_iota(jnp.int32, (width, width), 0) // seg
    c = lax.broadcasted_iota(jnp.int32, (width, width), 1) // seg
    return (r == c).astype(BF16)


def _segment_sum(x, ones_bd):
    hi = x.astype(BF16)
    lo = (x - hi.astype(F32)).astype(BF16)
    return _dot(hi, ones_bd) + _dot(lo, ones_bd)


def _head_rms_norm(x, gain, eps=1e-6):
    ones_bd = _segment_ones(LANES, HEAD_DIM)
    cols = []
    for c in range(x.shape[1] // LANES):
        xc = x[:, c * LANES:(c + 1) * LANES]
        ms = _segment_sum(xc * xc, ones_bd) * (1.0 / HEAD_DIM)
        cols.append(xc * lax.rsqrt(ms + eps) * gain[:, c * LANES:(c + 1) * LANES])
    return cols[0] if len(cols) == 1 else jnp.concatenate(cols, axis=1)


def _mod_kernel(c_ref, w_ref, b_ref, o_ref):
    c = c_ref[...]
    s = (c * _sigmoid(c)).astype(BF16)
    o_ref[...] = _dot(s, w_ref[...].astype(BF16)) + b_ref[...]


def _modulation(conds, w_mod, b_mod):
    tn = D_MODEL
    return pl.pallas_call(
        _mod_kernel,
        grid=(DEPTH, N_MOD * D_MODEL // tn),
        in_specs=[
            pl.BlockSpec((SUBLANES, D_MODEL), lambda l, j: (0, 0)),
            pl.BlockSpec((None, D_MODEL, tn), lambda l, j: (l, 0, j)),
            pl.BlockSpec((None, 1, tn), lambda l, j: (l, 0, j)),
        ],
        out_specs=pl.BlockSpec((None, SUBLANES, tn), lambda l, j: (l, 0, j)),
        out_shape=jax.ShapeDtypeStruct((DEPTH, SUBLANES, N_MOD * D_MODEL), F32),
        compiler_params=_params(("arbitrary", "arbitrary"), 32),
        name="modulation",
    )(conds, w_mod, b_mod.reshape(DEPTH, 1, N_MOD * D_MODEL))


def _ffn_kernel(*refs, sub, with_mixer_out, mixer_in_rope):
    refs = list(refs)
    x_ref = refs.pop(0)
    if with_mixer_out:
        oa_ref, ob_ref, oc_ref = refs[:3]
        del refs[:3]
    mod_ref = refs.pop(0)
    if with_mixer_out:
        wmix_ref, gmix_ref, bmix_ref = refs[:3]
        del refs[:3]
    wg_ref, wu_ref, wo_ref, g_ref, b_ref = refs[:5]
    del refs[:5]
    if mixer_in_rope is not None:
        n_in = 6 if mixer_in_rope else 3
        mixer_in_refs, refs = refs[:n_in], refs[n_in:]
    o_ref = refs.pop(0)
    mixer_in_outs = refs

    if with_mixer_out:
        y = _dot(oa_ref[...], wmix_ref[0:A_WIDTH, :].astype(BF16))
        y = y + _dot(ob_ref[...], wmix_ref[A_WIDTH:A_WIDTH + B_WIDTH, :].astype(BF16))
        y = y + _dot(oc_ref[...], wmix_ref[A_WIDTH + B_WIDTH:, :].astype(BF16))
        x = _layer_norm(ALPHA * x_ref[...] + mod_ref[5:6, :] * y, gmix_ref[...], bmix_ref[...])
    else:
        x = x_ref[...]
    shift = mod_ref[3 * sub:3 * sub + 1, :]
    scale = mod_ref[3 * sub + 1:3 * sub + 2, :]
    gate = mod_ref[3 * sub + 2:3 * sub + 3, :]
    h = (x * (1.0 + scale) + shift).astype(BF16)
    acc = jnp.zeros(x.shape, F32)
    for j in range(D_FF // FFN_CHUNK):
        cols = slice(j * FFN_CHUNK, (j + 1) * FFN_CHUNK)
        gt = _dot(h, wg_ref[:, cols].astype(BF16))
        up = _dot(h, wu_ref[:, cols].astype(BF16))
        act = (gt * _sigmoid(gt) * up).astype(BF16)
        acc = acc + _dot(act, wo_ref[cols, :].astype(BF16))
    y = ALPHA * x + 0.5 * gate * acc
    x_new = _layer_norm(y, g_ref[...], b_ref[...])
    o_ref[...] = x_new
    if mixer_in_rope is not None:
        _mixer_in(x_new, mod_ref, *mixer_in_refs, *mixer_in_outs)


def _ffn(x, mod_l, cond_of_row, w_in, w_out, ln_g, ln_b, layer, sub, mixer_out=None, mixer_in=None):
    rows = x.shape[0]
    tile = ROW_TILE
    row_spec = lambda width: pl.BlockSpec((tile, width), lambda i: (i, 0))
    cond_of_block = lambda i: cond_of_row(i * tile)
    ln_spec = lambda idx: _resident((None, 1, D_MODEL), lambda i: (layer * 3 + idx, 0, 0))
    in_specs, args = [row_spec(D_MODEL)], [x]
    if mixer_out is not None:
        o_a, o_b, o_c, w_mix = mixer_out
        in_specs += [row_spec(A_WIDTH), row_spec(B_WIDTH), row_spec(C_WIDTH)]
        args += [o_a, o_b, o_c]
    in_specs.append(pl.BlockSpec((None, N_MOD, D_MODEL), lambda i: (cond_of_block(i), 0, 0)))
    args.append(mod_l)
    if mixer_out is not None:
        in_specs += [_resident((None, D_MODEL, D_MODEL), lambda i: (layer, 0, 0)), ln_spec(1), ln_spec(1)]
        args += [w_mix, ln_g, ln_b]
    in_specs += [
        _resident((None, D_MODEL, D_FF), lambda i: (layer, 0, 0)),
        _resident((None, D_MODEL, D_FF), lambda i: (layer, 0, 1)),
        _resident((None, D_FF, D_MODEL), lambda i: (layer, 0, 0)),
        ln_spec(sub), ln_spec(sub),
    ]
    args += [w_in, w_in, w_out, ln_g, ln_b]
    out_specs = [row_spec(D_MODEL)]
    out_shape = [jax.ShapeDtypeStruct((rows, D_MODEL), F32)]
    rope = None
    if mixer_in is not None:
        w_mix_in, qg, kg, rope_tables, seq = mixer_in
        rope = rope_tables is not None
        in_specs += [
            _resident((None, D_MODEL, IN_WIDTH), lambda i: (layer, 0, 0)),
            _resident((None, 1, C_WIDTH), lambda i: (layer, 0, 0)),
            _resident((None, 1, KV_WIDTH), lambda i: (layer, 0, 0)),
        ]
        args += [w_mix_in, qg, kg]
        if rope:
            per_seq = seq // tile
            in_specs += [pl.BlockSpec((tile, LANES), lambda i: (i % per_seq, 0))] * 3
            args += list(rope_tables)
        widths = ((HG_WIDTH, F32), (B_WIDTH, F32), (C_WIDTH, BF16), (KV_WIDTH, F32), (KV_WIDTH, F32))
        out_specs += [row_spec(w) for w, _ in widths]
        out_shape += [jax.ShapeDtypeStruct((rows, w), dt) for w, dt in widths]
    return pl.pallas_call(
        functools.partial(_ffn_kernel, sub=sub, with_mixer_out=mixer_out is not None, mixer_in_rope=rope),
        grid=(rows // tile,),
        in_specs=in_specs,
        out_specs=out_specs,
        out_shape=out_shape,
        compiler_params=_params(("parallel",), 56),
        name=f"ffn{sub // 2 + 1}",
    )(*args)


def _mixer_in(x, mod_ref, w_ref, qg_ref, kg_ref, *refs):
    rope = len(refs) == 8
    if rope:
        cos_ref, s1_ref, s2_ref = refs[:3]
    zh_ref, u_ref, q_ref, k_ref, v_ref = refs[-5:]
    h = (x * (1.0 + mod_ref[4:5, :]) + mod_ref[3:4, :]).astype(BF16)

    zh_ref[...] = _dot(h, w_ref[:, 0:HG_WIDTH])
    o = HG_WIDTH
    glu_a = _dot(h, w_ref[:, o:o + B_WIDTH])
    glu_b = _dot(h, w_ref[:, o + B_WIDTH:o + 2 * B_WIDTH])
    u_ref[...] = glu_a * _sigmoid(glu_b)
    o += 2 * B_WIDTH
    cq = _dot(h, w_ref[:, o:o + C_WIDTH])
    ck = _dot(h, w_ref[:, o + C_WIDTH:o + C_WIDTH + KV_WIDTH])
    v_ref[...] = _dot(h, w_ref[:, o + C_WIDTH + KV_WIDTH:o + C_WIDTH + 2 * KV_WIDTH])

    qn = _head_rms_norm(cq, qg_ref[...])
    kn = _head_rms_norm(ck, kg_ref[...])
    if rope:
        cos, s1, s2 = cos_ref[...], s1_ref[...], s2_ref[...]

        def rot(t):
            cols = []
            for c in range(t.shape[1] // LANES):
                tc = t[:, c * LANES:(c + 1) * LANES]
                cols.append(tc * cos + pltpu.roll(tc, LANES - ROPE_PAIRS, 1) * s1
                            + pltpu.roll(tc, ROPE_PAIRS, 1) * s2)
            return cols[0] if len(cols) == 1 else jnp.concatenate(cols, axis=1)

        qn, kn = rot(qn), rot(kn)
    q_ref[...] = (qn * (HEAD_DIM ** -0.5 * LOG2_E)).astype(BF16)
    k_ref[...] = kn


def _cumsum_rows(x, reverse):
    n, w = x.shape
    row = lax.broadcasted_iota(jnp.int32, (SUBLANES, w), 0)
    groups = []
    for g in range(n // SUBLANES):
        grp = x[g * SUBLANES:(g + 1) * SUBLANES, :]
        s = 1
        while s < SUBLANES:
            if reverse:
                grp = grp + jnp.where(row < SUBLANES - s, pltpu.roll(grp, SUBLANES - s, 0), 0.0)
            else:
                grp = grp + jnp.where(row >= s, pltpu.roll(grp, s, 0), 0.0)
            s *= 2
        groups.append(grp)
    order = range(len(groups) - 1, -1, -1) if reverse else range(len(groups))
    carry = None
    for g in order:
        if carry is not None:
            groups[g] = groups[g] + carry
        carry = groups[g][0:1, :] if reverse else groups[g][SUBLANES - 1:SUBLANES, :]
    return jnp.concatenate(groups, axis=0)


def _hgrn_kernel(zh_ref, s0_ref, lbl_ref, ng_ref, o_ref, sfin_ref, acc_ref, qe_ref, decay_ref, kv_ref, *,
                 layer, seq):
    C, W = HGRN_CHUNK, A_WIDTH
    n_chunks = seq // C
    n_seq = zh_ref.shape[0] // seq

    lg = lbl_ref[...]
    e = jnp.exp(lg - jnp.max(lg, axis=0, keepdims=True))
    soft = e / jnp.sum(e, axis=0, keepdims=True)
    lb = jnp.zeros((1, W), F32)
    for j in range(1, layer + 1):
        lb = lb + soft[j:j + 1, :]

    lane_head = lax.broadcasted_iota(jnp.int32, (1, W), 1) // A_DK
    head_masks = [lane_head == h for h in range(A_HEADS)]
    t_idx = lax.broadcasted_iota(jnp.int32, (C, A_HEADS * C), 0)
    s_idx = lax.broadcasted_iota(jnp.int32, (C, A_HEADS * C), 1) % C
    pair_level = t_idx ^ s_idx
    key_not_after = s_idx <= t_idx
    key_not_before = s_idx >= t_idx
    bd_mask = (lax.broadcasted_iota(jnp.int32, (W, W), 0) // A_DK
               == lax.broadcasted_iota(jnp.int32, (W, W), 1) // A_DK)
    def stack_heads(t_bf):
        return jnp.concatenate([jnp.where(m, t_bf, jnp.zeros_like(t_bf)) for m in head_masks], axis=0)

    def chunk_rows(n):
        return pl.ds(pl.multiple_of(n * C, C), C)

    def gates(rows, reverse):
        zf = zh_ref[rows, (3 if reverse else 2) * W:(4 if reverse else 3) * W]
        f = lb + (1.0 - lb) * _sigmoid(zf)
        return 1.0 - f, _cumsum_rows(jnp.log(jnp.maximum(f, F_MIN)), reverse) * LOG2_E

    def intra_body(n, carry):
        rows = chunk_rows(n)
        q = zh_ref[rows, 0:W]
        v_bf = zh_ref[rows, W:2 * W].astype(BF16)
        k_f, a_f = gates(rows, False)
        k_b, a_b = gates(rows, True)

        scores = jnp.zeros((C, A_HEADS * C), F32)
        h = C // 2
        while h >= HGRN_BASE:
            split = lambda t: t.reshape(C // (2 * h), 2, h, W)
            q4, af4, ab4, kf4, kb4 = split(q), split(a_f), split(a_b), split(k_f), split(k_b)
            end_first = af4[:, 0, h - 1:h, :]
            start_second = ab4[:, 1, 0:1, :]
            q_fwd = q4[:, 1] * jnp.exp2(af4[:, 1] - end_first)
            k_fwd = kf4[:, 0] * jnp.exp2(end_first - af4[:, 0])
            q_bwd = q4[:, 0] * jnp.exp2(ab4[:, 0] - start_second)
            k_bwd = kb4[:, 1] * jnp.exp2(start_second - ab4[:, 1])
            qt = jnp.stack([q_bwd, q_fwd], axis=1).reshape(C, W).astype(BF16)
            kt = jnp.stack([k_fwd, k_bwd], axis=1).reshape(C, W).astype(BF16)
            scores = jnp.where(pair_level < 2 * h, _dot_nt(qt, stack_heads(kt)), scores)
            h //= 2
        hb = HGRN_BASE
        block = lambda t: t.reshape(C // hb, hb, W)
        base = []
        for a, kk, ref_at, keep in ((a_f, k_f, hb // 2 - 1, key_not_after), (a_b, k_b, hb // 2, key_not_before)):
            a3 = block(a)
            ref_row = a3[:, ref_at:ref_at + 1, :]
            qt = (block(q) * jnp.exp2(a3 - ref_row)).reshape(C, W).astype(BF16)
            kt = (block(kk) * jnp.exp2(ref_row - a3)).reshape(C, W).astype(BF16)
            base.append(jnp.where(keep, _dot_nt(qt, stack_heads(kt)), 0.0))
        scores = jnp.where(pair_level < hb, base[0] + base[1], scores)
        acc_ref[rows, :] = _dot(scores.astype(BF16), stack_heads(v_bf))

        for d, (a, kk, last) in enumerate(((a_f, k_f, C - 1), (a_b, k_b, 0))):
            a_last = a[last:last + 1, :]
            qe_ref[d, rows, :] = (q * jnp.exp2(a)).astype(BF16)
            decay_ref[n, d:d + 1, :] = a_last
            kv_ref[n, d] = jnp.where(bd_mask, _dot_tn(v_bf, (kk * jnp.exp2(a_last - a)).astype(BF16)), 0.0)
        return carry

    lax.fori_loop(0, n_seq * n_chunks, intra_body, 0, unroll=2)

    def state_body(i, carry):
        new = []
        for sq, states in enumerate(carry):
            first = sq * n_chunks
            for d, (st, n) in enumerate(zip(states, (first + i, first + n_chunks - 1 - i))):
                rows = chunk_rows(n)
                acc_ref[rows, :] += _dot_nt(qe_ref[d, rows, :], st.astype(BF16))
                new.append(st * jnp.exp2(decay_ref[n, d:d + 1, :]) + kv_ref[n, d])
        return tuple((new[2 * sq], new[2 * sq + 1]) for sq in range(n_seq))

    entry = [s0_ref[sq if s0_ref.shape[0] > 1 else 0] for sq in range(n_seq)]
    finals = lax.fori_loop(0, n_chunks, state_body, tuple((s[0], s[1]) for s in entry),
                           unroll=min(n_chunks, 4))

    ones_bd = _segment_ones(LANES, A_DK)

    def norm_body(n, carry):
        rows = pl.ds(pl.multiple_of(n * HGRN_NORM_ROWS, HGRN_NORM_ROWS), HGRN_NORM_ROWS)
        tot = acc_ref[rows, :]
        zg = zh_ref[rows, 4 * W:5 * W]
        cols = []
        for c in range(W // LANES):
            lanes = slice(c * LANES, (c + 1) * LANES)
            tc = tot[:, lanes]
            ms = _segment_sum(tc * tc, ones_bd) * (1.0 / A_DK)
            cols.append(tc * lax.rsqrt(ms + 1e-6) * ng_ref[:, lanes])
        on = jnp.concatenate(cols, axis=1)
        o_ref[rows, :] = (on * (zg * _sigmoid(zg))).astype(o_ref.dtype)
        return carry

    lax.fori_loop(0, n_seq * seq // HGRN_NORM_ROWS, norm_body, 0)

    for sq, states in enumerate(finals):
        for d, st in enumerate(states):
            s_kv = st.T
            for h in range(A_HEADS):
                blk = slice(h * A_DK, (h + 1) * A_DK)
                sfin_ref[sq, d, h] = s_kv[blk, blk]


def _hgrn(zh, s0, lb_logits, norm_g, layer, seq, seqs_per_step=1):
    rows = zh.shape[0]
    n_seq = rows // seq
    S, W = seqs_per_step, A_WIDTH
    if s0.shape[0] == 1:
        s0_spec = pl.BlockSpec((1, 2, W, W), lambda b: (0, 0, 0, 0))
    else:
        s0_spec = pl.BlockSpec((S, 2, W, W), lambda b: (b, 0, 0, 0))
    chunks = S * seq // HGRN_CHUNK
    return pl.pallas_call(
        functools.partial(_hgrn_kernel, layer=layer, seq=seq),
        grid=(n_seq // S,),
        in_specs=[
            pl.BlockSpec((S * seq, HG_WIDTH), lambda b: (b, 0)),
            s0_spec,
            _resident((DEPTH, W), lambda b: (0, 0)),
            _resident((None, 1, W), lambda b: (layer, 0, 0)),
        ],
        out_specs=[
            pl.BlockSpec((S * seq, W), lambda b: (b, 0)),
            pl.BlockSpec((S, 2, A_HEADS, A_DK, A_DK), lambda b: (b, 0, 0, 0, 0)),
        ],
        out_shape=[
            jax.ShapeDtypeStruct((rows, W), BF16),
            jax.ShapeDtypeStruct((n_seq, 2, A_HEADS, A_DK, A_DK), F32),
        ],
        scratch_shapes=[pltpu.VMEM((S * seq, W), F32), pltpu.VMEM((2, S * seq, W), BF16),
                        pltpu.VMEM((chunks, 2, W), F32), pltpu.VMEM((chunks, 2, W, W), F32)],
        compiler_params=_params(("parallel",), 48),
        name="hgrn2",
    )(zh, s0, lb_logits, norm_g)


def _conv_kernel(up_ref, uc_ref, un_ref, w_ref, b_ref, g_ref, beta_ref, o_ref, sh_ref, *, tiles_per_seq):
    i = pl.program_id(0)
    T, H, S = CONV_TILE, CONV_HALO, SUBLANES
    P = T + 2 * H
    first = (i % tiles_per_seq) == 0
    last = (i % tiles_per_seq) == tiles_per_seq - 1
    sh_ref[0, 0:H, :] = jnp.where(first, 0.0, up_ref[...])
    sh_ref[0, H:H + T, :] = uc_ref[...]
    sh_ref[0, H + T:P, :] = jnp.where(last, 0.0, un_ref[...])
    for b in range(1, S):
        sh_ref[b, 0:P - S, :] = sh_ref[0, b:b + P - S, :]
    acc = jnp.zeros((T, B_WIDTH), F32)
    for j in range(CONV_K):
        start = H - CONV_PAD + j
        a, b = start // S, start % S
        acc = acc + w_ref[j:j + 1, :] * sh_ref[b, a * S:a * S + T, :]
    y = _layer_norm(acc + b_ref[...], g_ref[...], beta_ref[...])
    o_ref[...] = (y * _sigmoid(y)).astype(o_ref.dtype)


def _conv(u, conv_w, conv_b, ln_g, ln_b, layer, seq):
    rows = u.shape[0]
    T, H = CONV_TILE, CONV_HALO
    tiles_per_seq = seq // T
    ratio = T // H
    n_halo_blocks = rows // H
    return pl.pallas_call(
        functools.partial(_conv_kernel, tiles_per_seq=tiles_per_seq),
        grid=(rows // T,),
        in_specs=[
            pl.BlockSpec((H, B_WIDTH), lambda i: (jnp.maximum(i * ratio - 1, 0), 0)),
            pl.BlockSpec((T, B_WIDTH), lambda i: (i, 0)),
            pl.BlockSpec((H, B_WIDTH), lambda i: (jnp.minimum((i + 1) * ratio, n_halo_blocks - 1), 0)),
            _resident((None, CONV_K, B_WIDTH), lambda i: (layer, 0, 0)),
            _resident((None, 1, B_WIDTH), lambda i: (layer, 0, 0)),
            _resident((None, 1, B_WIDTH), lambda i: (layer, 0, 0)),
            _resident((None, 1, B_WIDTH), lambda i: (layer, 0, 0)),
        ],
        out_specs=pl.BlockSpec((T, B_WIDTH), lambda i: (i, 0)),
        out_shape=jax.ShapeDtypeStruct((rows, B_WIDTH), BF16),
        scratch_shapes=[pltpu.VMEM((SUBLANES, T + 2 * H, B_WIDTH), F32)],
        compiler_params=_params(("parallel",), 16),
        name="conv_module",
    )(u, u, u, conv_w, conv_b, ln_g, ln_b)


def _attn_kernel(*refs, n_seq, single_q_tile, with_past):
    if with_past:
        q_ref, k_ref, v_ref, kp_ref, vp_ref, o_ref, kd_ref, va_ref = refs
        sources = ((kp_ref, vp_ref), (k_ref, v_ref))
    else:
        q_ref, k_ref, v_ref, o_ref, kd_ref, va_ref = refs
        sources = ((k_ref, v_ref),)
    tq = q_ref.shape[0] // n_seq
    seq_k = kd_ref.shape[1] // n_seq
    kb = min(seq_k, ATTN_KEY_BLOCK)
    group = C_HEADS // C_KV_HEADS
    half = lax.broadcasted_iota(jnp.int32, (1, LANES), 1) // HEAD_DIM

    def prepare():
        start = 0
        for ks_ref, vs_ref in sources:
            rows = slice(start, start + ks_ref.shape[0])
            start += ks_ref.shape[0]
            k = ks_ref[...]
            k_swapped = pltpu.roll(k, HEAD_DIM, 1)
            v = vs_ref[...]
            v_swapped = pltpu.roll(v, HEAD_DIM, 1)
            for g in range(C_KV_HEADS):
                kd_ref[g, rows, :] = jnp.where(half == g, k, k_swapped).astype(BF16)
                va_ref[g, rows, 0:LANES] = jnp.where(half == g, v, v_swapped).astype(BF16)
                va_ref[g, rows, LANES:2 * LANES] = jnp.ones(v.shape, BF16)

    if single_q_tile:
        prepare()
    else:
        pl.when(pl.program_id(1) == 0)(prepare)

    n_blocks = seq_k // kb
    chains = [(sq, g) for sq in range(n_seq) for g in range(C_KV_HEADS)]
    units = [(c, b) for b in range(n_blocks) for c in range(len(chains))]
    q_stacked, run_max, acc = {}, {}, {}
    scores, probs, rescale = {}, {}, {}

    def stage_scores(u):
        c, b = units[u]
        sq, g = chains[c]
        if c not in q_stacked:
            heads = []
            for j in range(group):
                hq = g * group + j
                col = q_ref[sq * tq:(sq + 1) * tq, (hq // 2) * LANES:(hq // 2 + 1) * LANES]
                heads.append(jnp.where(half == hq % 2, col, jnp.zeros_like(col)))
            q_stacked[c] = jnp.concatenate(heads, axis=0)
        keys = slice(sq * seq_k + b * kb, sq * seq_k + (b + 1) * kb)
        scores[u] = _dot_nt(q_stacked[c], kd_ref[g, keys, :])

    def stage_exp(u):
        c, b = units[u]
        s = scores.pop(u)
        m_blk = jnp.max(s, axis=-1, keepdims=True)
        m_new = m_blk if b == 0 else jnp.maximum(run_max[c], m_blk)
        probs[u] = jnp.exp2(s - m_new).astype(BF16)
        if b > 0:
            rescale[u] = jnp.exp2(run_max[c] - m_new)
        run_max[c] = m_new

    def stage_values(u):
        c, b = units[u]
        sq, g = chains[c]
        keys = slice(sq * seq_k + b * kb, sq * seq_k + (b + 1) * kb)
        pv = _dot(probs.pop(u), va_ref[g, keys, :])
        acc[c] = pv if b == 0 else rescale.pop(u) * acc[c] + pv
        if b == n_blocks - 1:
            total = acc.pop(c)
            o = total[:, 0:LANES] / total[:, LANES:2 * LANES]
            for j in range(group):
                hq = g * group + j
                o_ref[sq * tq:(sq + 1) * tq, hq * HEAD_DIM:(hq + 1) * HEAD_DIM] = (
                    o[j * tq:(j + 1) * tq, (hq % 2) * HEAD_DIM:(hq % 2 + 1) * HEAD_DIM].astype(o_ref.dtype))

    for t in range(len(units) + 2):
        if t < len(units):
            stage_scores(t)
        if 0 <= t - 1 < len(units):
            stage_exp(t - 1)
        if 0 <= t - 2 < len(units):
            stage_values(t - 2)


def _attention(q, k, v, seq, tq, seqs_per_step=1, past=None):
    n_seq = q.shape[0] // seq
    q_tiles = seq // tq
    assert seqs_per_step == 1 or (q_tiles == 1 and past is None)
    q_rows, k_rows = seqs_per_step * tq, seqs_per_step * seq
    in_specs = [
        pl.BlockSpec((q_rows, C_WIDTH), lambda b, i: (b * q_tiles + i, 0)),
        pl.BlockSpec((k_rows, KV_WIDTH), lambda b, i: (b, 0)),
        pl.BlockSpec((k_rows, KV_WIDTH), lambda b, i: (b, 0)),
    ]
    args = [q, k, v]
    keys_per_step = k_rows
    if past is not None:
        n_past = past[0].shape[1]
        in_specs += [pl.BlockSpec((None, n_past, KV_WIDTH), lambda b, i: (b, 0, 0))] * 2
        args += list(past)
        keys_per_step += n_past
    return pl.pallas_call(
        functools.partial(_attn_kernel, n_seq=seqs_per_step, single_q_tile=q_tiles == 1, with_past=past is not None),
        grid=(n_seq // seqs_per_step, q_tiles),
        in_specs=in_specs,
        out_specs=pl.BlockSpec((q_rows, C_WIDTH), lambda b, i: (b * q_tiles + i, 0)),
        out_shape=jax.ShapeDtypeStruct(q.shape, BF16),
        scratch_shapes=[pltpu.VMEM((C_KV_HEADS, keys_per_step, LANES), BF16),
                        pltpu.VMEM((C_KV_HEADS, keys_per_step, 2 * LANES), BF16)],
        compiler_params=_params(("parallel", "arbitrary"), 48),
        name="attention",
    )(*args)


def _rope_tables(seq):
    pos = np.arange(seq)
    row_id = (pos // GRID_W).astype(np.float32)
    col_id = (pos % GRID_W).astype(np.float32)
    inv = (np.float32(ROPE_THETA) ** (-np.arange(ROPE_PAIRS, dtype=np.float32) / ROPE_PAIRS)).astype(np.float32)
    lane = np.arange(LANES) % HEAD_DIM
    use_col = (lane // (2 * ROPE_PAIRS)) == 1
    first = (lane % (2 * ROPE_PAIRS)) < ROPE_PAIRS
    freq = inv[lane % ROPE_PAIRS]
    ang = (np.where(use_col[None, :], col_id[:, None], row_id[:, None]) * freq[None, :]).astype(np.float32)
    cos, sin = np.cos(ang).astype(np.float32), np.sin(ang).astype(np.float32)
    zero = np.float32(0.0)
    return tuple(jnp.asarray(t) for t in (cos, np.where(first[None, :], -sin, zero), np.where(first[None, :], zero, sin)))


def _state_to_block_diag(s):
    eye = jnp.eye(A_HEADS, dtype=s.dtype)
    st = jnp.swapaxes(s, -1, -2)
    bd = st[..., :, :, None, :] * eye[:, None, :, None]
    return bd.reshape(s.shape[:-3] + (A_WIDTH, A_WIDTH))


def kernel(x_prompt, x_sample, cache_k, cache_v, state_hgrn, c, c_ctx, w_mod, b_mod, ln_g, ln_b, ffn1_w_in, ffn1_w_out, ffn2_w_in, ffn2_w_out, w_in, w_out, hgrn_lb_logits, hgrn_norm_g, conv_w, conv_b, conv_ln_g, conv_ln_b, q_norm_g, k_norm_g):
    batch, seq, _ = x_prompt.shape
    dec_batch, dec_seq, _ = x_sample.shape
    past = cache_k.shape[2]
    assert (batch * seq) % ROW_TILE == 0 and dec_seq % ROW_TILE == 0
    assert dec_batch + 1 <= SUBLANES
    assert KV_WIDTH == LANES and C_HEADS % (2 * C_KV_HEADS) == 0

    bf = lambda w: w.astype(BF16)
    ffn1_w_in, ffn1_w_out, w_in = map(bf, (ffn1_w_in, ffn1_w_out, w_in))

    conds = jnp.zeros((SUBLANES, D_MODEL), F32).at[0].set(c_ctx).at[1:1 + dec_batch].set(c)
    mod = _modulation(conds, w_mod, b_mod).reshape(DEPTH, SUBLANES, N_MOD, D_MODEL)

    ln_g3 = ln_g.reshape(DEPTH * 3, 1, D_MODEL)
    ln_b3 = ln_b.reshape(DEPTH * 3, 1, D_MODEL)
    qg = jnp.tile(q_norm_g, (1, C_HEADS)).reshape(DEPTH, 1, C_WIDTH)
    kg = jnp.tile(k_norm_g, (1, C_KV_HEADS)).reshape(DEPTH, 1, KV_WIDTH)
    norm_g = hgrn_norm_g.reshape(DEPTH, 1, A_WIDTH)
    conv_b3 = conv_b.reshape(DEPTH, 1, B_WIDTH)
    conv_g3 = conv_ln_g.reshape(DEPTH, 1, B_WIDTH)
    conv_beta3 = conv_ln_b.reshape(DEPTH, 1, B_WIDTH)
    tables = _rope_tables(dec_seq)
    zero_state = jnp.zeros((1, 2, A_WIDTH, A_WIDTH), F32)
    lat_state = _state_to_block_diag(state_hgrn)

    ctx_cond = lambda row: 0
    lat_cond = lambda row: 1 + row // dec_seq

    xp = x_prompt.reshape(batch * seq, D_MODEL)
    xs = x_sample.reshape(dec_batch * dec_seq, D_MODEL)
    ks, vs, ss = [], [], []
    for l in range(DEPTH):
        mod_l = mod[l]

        def layer_fn(x, cond, sq, rope_tables, s0, k_past, v_past):
            x, zh, u, q, k, v = _ffn(x, mod_l, cond, ffn1_w_in, ffn1_w_out, ln_g3, ln_b3, l, 0,
                                     mixer_in=(w_in, qg, kg, rope_tables, sq))
            o_a, s_fin = _hgrn(zh, s0, hgrn_lb_logits, norm_g, l, sq,
                               seqs_per_step=HGRN_SHORT_SEQS_PER_STEP if k_past is None else 1)
            o_b = _conv(u, conv_w, conv_b3, conv_g3, conv_beta3, l, sq)
            if k_past is None:
                o_c = _attention(q, k, v, sq, sq, seqs_per_step=ATTN_SHORT_SEQS_PER_STEP)
            else:
                o_c = _attention(q, k, v, sq, ATTN_Q_TILE_LONG, past=(k_past, v_past))
            x, = _ffn(x, mod_l, cond, ffn2_w_in, ffn2_w_out, ln_g3, ln_b3, l, 2, mixer_out=(o_a, o_b, o_c, w_out))
            return x, k, v, s_fin

        xp, k_l, v_l, s_l = layer_fn(xp, ctx_cond, seq, None, zero_state, None, None)
        ks.append(k_l.reshape(batch, seq, C_KV_HEADS, HEAD_DIM))
        vs.append(v_l.reshape(batch, seq, C_KV_HEADS, HEAD_DIM))
        ss.append(s_l)
        xs, _, _, _ = layer_fn(xs, lat_cond, dec_seq, tables, lat_state[:, l],
                               cache_k[:, l].reshape(dec_batch, past, KV_WIDTH),
                               cache_v[:, l].reshape(dec_batch, past, KV_WIDTH))

    return (xp.reshape(batch, seq, D_MODEL), xs.reshape(dec_batch, dec_seq, D_MODEL),
            jnp.stack(ks, axis=1), jnp.stack(vs, axis=1), jnp.stack(ss, axis=1))
```

```python
import functools

import numpy as np
import jax
import jax.numpy as jnp
from jax import lax
from jax.experimental import pallas as pl
from jax.experimental.pallas import tpu as pltpu

F32 = jnp.float32
BF16 = jnp.bfloat16

D_MODEL = 1024
DEPTH = 2
GRID_W = 64
HEAD_DIM = 64
A_HEADS = 4
A_DK = 64
A_WIDTH = 256
B_WIDTH = 256
CONV_K = 31
CONV_PAD = 15
C_HEADS = 8
C_KV_HEADS = 2
C_WIDTH = 512
KV_WIDTH = C_KV_HEADS * HEAD_DIM
D_FF = 2816
ROPE_THETA = 10000.0
ROPE_PAIRS = 16
N_MOD = 9
ALPHA = (2 * DEPTH) ** 0.25
F_MIN = 1e-6
LOG2_E = 1.4426950408889634
IN_WIDTH = 5 * A_WIDTH + 2 * B_WIDTH + C_WIDTH + 2 * KV_WIDTH
HG_WIDTH = 5 * A_WIDTH

LANES = 128
SUBLANES = 8
VMEM_BYTES_V7X = 64 * 1024 * 1024
VMEM_RESERVE_BYTES = 4 * 1024 * 1024

ROW_TILE = 512
FFN_CHUNK = 256
HGRN_CHUNK = 128
HGRN_BASE = 8
HGRN_NORM_ROWS = 256
HGRN_SEQS_PER_STEP = 1
CONV_TILE = 256
CONV_HALO = 16
ATTN_Q_TILE_LONG = 512
ATTN_KEY_BLOCK = 512
ATTN_SHORT_SEQS_PER_STEP = 8


def _params(semantics, vmem_mb):
    return pltpu.CompilerParams(dimension_semantics=semantics,
                                vmem_limit_bytes=min(vmem_mb << 20, VMEM_BYTES_V7X - VMEM_RESERVE_BYTES))


def _resident(block_shape, index_map):
    return pl.BlockSpec(block_shape, index_map, pipeline_mode=pl.Buffered(1))


def _sigmoid(x):
    return 1.0 / (1.0 + jnp.exp(-x))


def _layer_norm(y, g, b, eps=1e-5):
    mu = jnp.mean(y, axis=-1, keepdims=True)
    d = y - mu
    var = jnp.mean(d * d, axis=-1, keepdims=True)
    return d * lax.rsqrt(var + eps) * g + b


def _dot(a, b):
    return jnp.dot(a, b, preferred_element_type=F32)


def _dot_nt(a, b):
    return lax.dot_general(a, b, (((1,), (1,)), ((), ())), preferred_element_type=F32)


def _dot_tn(a, b):
    return lax.dot_general(a, b, (((0,), (0,)), ((), ())), preferred_element_type=F32)


def _segment_ones(width, seg):
    r = lax.broadcasted_iota(jnp.int32, (width, width), 0) // seg
    c = lax.broadcasted_iota(jnp.int32, (width, width), 1) // seg
    return (r == c).astype(BF16)


def _segment_sum(x, ones_bd):
    hi = x.astype(BF16)
    lo = (x - hi.astype(F32)).astype(BF16)
    return _dot(hi, ones_bd) + _dot(lo, ones_bd)


def _head_rms_norm(x, gain, eps=1e-6):
    ones_bd = _segment_ones(LANES, HEAD_DIM)
    cols = []
    for c in range(x.shape[1] // LANES):
        xc = x[:, c * LANES:(c + 1) * LANES]
        ms = _segment_sum(xc * xc, ones_bd) * (1.0 / HEAD_DIM)
        cols.append(xc * lax.rsqrt(ms + eps) * gain[:, c * LANES:(c + 1) * LANES])
    return cols[0] if len(cols) == 1 else jnp.concatenate(cols, axis=1)


def _mod_kernel(c_ref, w_ref, b_ref, o_ref):
    c = c_ref[...]
    s = (c * _sigmoid(c)).astype(BF16)
    o_ref[...] = _dot(s, w_ref[...].astype(BF16)) + b_ref[...]


def _modulation(conds, w_mod, b_mod):
    tn = D_MODEL
    return pl.pallas_call(
        _mod_kernel,
        grid=(DEPTH, N_MOD * D_MODEL // tn),
        in_specs=[
            pl.BlockSpec((SUBLANES, D_MODEL), lambda l, j: (0, 0)),
            pl.BlockSpec((None, D_MODEL, tn), lambda l, j: (l, 0, j)),
            pl.BlockSpec((None, 1, tn), lambda l, j: (l, 0, j)),
        ],
        out_specs=pl.BlockSpec((None, SUBLANES, tn), lambda l, j: (l, 0, j)),
        out_shape=jax.ShapeDtypeStruct((DEPTH, SUBLANES, N_MOD * D_MODEL), F32),
        compiler_params=_params(("arbitrary", "arbitrary"), 32),
        name="modulation",
    )(conds, w_mod, b_mod.reshape(DEPTH, 1, N_MOD * D_MODEL))


def _ffn_kernel(*refs, sub, with_mixer_out, mixer_in_rope):
    refs = list(refs)
    x_ref = refs.pop(0)
    if with_mixer_out:
        oa_ref, ob_ref, oc_ref = refs[:3]
        del refs[:3]
    mod_ref = refs.pop(0)
    if with_mixer_out:
        wmix_ref, gmix_ref, bmix_ref = refs[:3]
        del refs[:3]
    wg_ref, wu_ref, wo_ref, g_ref, b_ref = refs[:5]
    del refs[:5]
    if mixer_in_rope is not None:
        n_in = 6 if mixer_in_rope else 3
        mixer_in_refs, refs = refs[:n_in], refs[n_in:]
    o_ref = refs.pop(0)
    mixer_in_outs = refs

    if with_mixer_out:
        y = _dot(oa_ref[...], wmix_ref[0:A_WIDTH, :].astype(BF16))
        y = y + _dot(ob_ref[...], wmix_ref[A_WIDTH:A_WIDTH + B_WIDTH, :].astype(BF16))
        y = y + _dot(oc_ref[...], wmix_ref[A_WIDTH + B_WIDTH:, :].astype(BF16))
        x = _layer_norm(ALPHA * x_ref[...] + mod_ref[5:6, :] * y, gmix_ref[...], bmix_ref[...])
    else:
        x = x_ref[...]
    shift = mod_ref[3 * sub:3 * sub + 1, :]
    scale = mod_ref[3 * sub + 1:3 * sub + 2, :]
    gate = mod_ref[3 * sub + 2:3 * sub + 3, :]
    h = (x * (1.0 + scale) + shift).astype(BF16)
    acc = jnp.zeros(x.shape, F32)
    for j in range(D_FF // FFN_CHUNK):
        cols = slice(j * FFN_CHUNK, (j + 1) * FFN_CHUNK)
        gt = _dot(h, wg_ref[:, cols].astype(BF16))
        up = _dot(h, wu_ref[:, cols].astype(BF16))
        act = (gt * _sigmoid(gt) * up).astype(BF16)
        acc = acc + _dot(act, wo_ref[cols, :].astype(BF16))
    y = ALPHA * x + 0.5 * gate * acc
    x_new = _layer_norm(y, g_ref[...], b_ref[...])
    o_ref[...] = x_new
    if mixer_in_rope is not None:
        _mixer_in(x_new, mod_ref, *mixer_in_refs, *mixer_in_outs)


def _ffn(x, mod_l, cond_of_row, w_in, w_out, ln_g, ln_b, layer, sub, mixer_out=None, mixer_in=None):
    rows = x.shape[0]
    tile = ROW_TILE
    row_spec = lambda width: pl.BlockSpec((tile, width), lambda i: (i, 0))
    cond_of_block = lambda i: cond_of_row(i * tile)
    ln_spec = lambda idx: _resident((None, 1, D_MODEL), lambda i: (layer * 3 + idx, 0, 0))
    in_specs, args = [row_spec(D_MODEL)], [x]
    if mixer_out is not None:
        o_a, o_b, o_c, w_mix = mixer_out
        in_specs += [row_spec(A_WIDTH), row_spec(B_WIDTH), row_spec(C_WIDTH)]
        args += [o_a, o_b, o_c]
    in_specs.append(pl.BlockSpec((None, N_MOD, D_MODEL), lambda i: (cond_of_block(i), 0, 0)))
    args.append(mod_l)
    if mixer_out is not None:
        in_specs += [_resident((None, D_MODEL, D_MODEL), lambda i: (layer, 0, 0)), ln_spec(1), ln_spec(1)]
        args += [w_mix, ln_g, ln_b]
    in_specs += [
        _resident((None, D_MODEL, D_FF), lambda i: (layer, 0, 0)),
        _resident((None, D_MODEL, D_FF), lambda i: (layer, 0, 1)),
        _resident((None, D_FF, D_MODEL), lambda i: (layer, 0, 0)),
        ln_spec(sub), ln_spec(sub),
    ]
    args += [w_in, w_in, w_out, ln_g, ln_b]
    out_specs = [row_spec(D_MODEL)]
    out_shape = [jax.ShapeDtypeStruct((rows, D_MODEL), F32)]
    rope = None
    if mixer_in is not None:
        w_mix_in, qg, kg, rope_tables, seq = mixer_in
        rope = rope_tables is not None
        in_specs += [
            _resident((None, D_MODEL, IN_WIDTH), lambda i: (layer, 0, 0)),
            _resident((None, 1, C_WIDTH), lambda i: (layer, 0, 0)),
            _resident((None, 1, KV_WIDTH), lambda i: (layer, 0, 0)),
        ]
        args += [w_mix_in, qg, kg]
        if rope:
            per_seq = seq // tile
            in_specs += [pl.BlockSpec((tile, LANES), lambda i: (i % per_seq, 0))] * 3
            args += list(rope_tables)
        widths = ((HG_WIDTH, F32), (B_WIDTH, F32), (C_WIDTH, BF16), (KV_WIDTH, F32), (KV_WIDTH, F32))
        out_specs += [row_spec(w) for w, _ in widths]
        out_shape += [jax.ShapeDtypeStruct((rows, w), dt) for w, dt in widths]
    return pl.pallas_call(
        functools.partial(_ffn_kernel, sub=sub, with_mixer_out=mixer_out is not None, mixer_in_rope=rope),
        grid=(rows // tile,),
        in_specs=in_specs,
        out_specs=out_specs,
        out_shape=out_shape,
        compiler_params=_params(("parallel",), 56),
        name=f"ffn{sub // 2 + 1}",
    )(*args)


def _mixer_in(x, mod_ref, w_ref, qg_ref, kg_ref, *refs):
    rope = len(refs) == 8
    if rope:
        cos_ref, s1_ref, s2_ref = refs[:3]
    zh_ref, u_ref, q_ref, k_ref, v_ref = refs[-5:]
    h = (x * (1.0 + mod_ref[4:5, :]) + mod_ref[3:4, :]).astype(BF16)

    zh_ref[...] = _dot(h, w_ref[:, 0:HG_WIDTH])
    o = HG_WIDTH
    glu_a = _dot(h, w_ref[:, o:o + B_WIDTH])
    glu_b = _dot(h, w_ref[:, o + B_WIDTH:o + 2 * B_WIDTH])
    u_ref[...] = glu_a * _sigmoid(glu_b)
    o += 2 * B_WIDTH
    cq = _dot(h, w_ref[:, o:o + C_WIDTH])
    ck = _dot(h, w_ref[:, o + C_WIDTH:o + C_WIDTH + KV_WIDTH])
    v_ref[...] = _dot(h, w_ref[:, o + C_WIDTH + KV_WIDTH:o + C_WIDTH + 2 * KV_WIDTH])

    qn = _head_rms_norm(cq, qg_ref[...])
    kn = _head_rms_norm(ck, kg_ref[...])
    if rope:
        cos, s1, s2 = cos_ref[...], s1_ref[...], s2_ref[...]

        def rot(t):
            cols = []
            for c in range(t.shape[1] // LANES):
                tc = t[:, c * LANES:(c + 1) * LANES]
                cols.append(tc * cos + pltpu.roll(tc, LANES - ROPE_PAIRS, 1) * s1
                            + pltpu.roll(tc, ROPE_PAIRS, 1) * s2)
            return cols[0] if len(cols) == 1 else jnp.concatenate(cols, axis=1)

        qn, kn = rot(qn), rot(kn)
    q_ref[...] = (qn * (HEAD_DIM ** -0.5 * LOG2_E)).astype(BF16)
    k_ref[...] = kn


def _cumsum_rows(x, reverse):
    n, w = x.shape
    row = lax.broadcasted_iota(jnp.int32, (SUBLANES, w), 0)
    groups = []
    for g in range(n // SUBLANES):
        grp = x[g * SUBLANES:(g + 1) * SUBLANES, :]
        s = 1
        while s < SUBLANES:
            if reverse:
                grp = grp + jnp.where(row < SUBLANES - s, pltpu.roll(grp, SUBLANES - s, 0), 0.0)
            else:
                grp = grp + jnp.where(row >= s, pltpu.roll(grp, s, 0), 0.0)
            s *= 2
        groups.append(grp)
    order = range(len(groups) - 1, -1, -1) if reverse else range(len(groups))
    carry = None
    for g in order:
        if carry is not None:
            groups[g] = groups[g] + carry
        carry = groups[g][0:1, :] if reverse else groups[g][SUBLANES - 1:SUBLANES, :]
    return jnp.concatenate(groups, axis=0)


def _hgrn_kernel(zh_ref, s0_ref, lbl_ref, ng_ref, o_ref, sfin_ref, acc_ref, qe_ref, decay_ref, kv_ref, *,
                 layer, seq):
    C, W = HGRN_CHUNK, A_WIDTH
    n_chunks = seq // C
    n_seq = zh_ref.shape[0] // seq

    lg = lbl_ref[...]
    e = jnp.exp(lg - jnp.max(lg, axis=0, keepdims=True))
    soft = e / jnp.sum(e, axis=0, keepdims=True)
    lb = jnp.zeros((1, W), F32)
    for j in range(1, layer + 1):
        lb = lb + soft[j:j + 1, :]

    lane_head = lax.broadcasted_iota(jnp.int32, (1, W), 1) // A_DK
    head_masks = [lane_head == h for h in range(A_HEADS)]
    t_idx = lax.broadcasted_iota(jnp.int32, (C, A_HEADS * C), 0)
    s_idx = lax.broadcasted_iota(jnp.int32, (C, A_HEADS * C), 1) % C
    pair_level = t_idx ^ s_idx
    key_not_after = s_idx <= t_idx
    key_not_before = s_idx >= t_idx
    bd_mask = (lax.broadcasted_iota(jnp.int32, (W, W), 0) // A_DK
               == lax.broadcasted_iota(jnp.int32, (W, W), 1) // A_DK)
    def stack_heads(t_bf):
        return jnp.concatenate([jnp.where(m, t_bf, jnp.zeros_like(t_bf)) for m in head_masks], axis=0)

    def chunk_rows(n):
        return pl.ds(pl.multiple_of(n * C, C), C)

    def gates(rows, reverse):
        zf = zh_ref[rows, (3 if reverse else 2) * W:(4 if reverse else 3) * W]
        f = lb + (1.0 - lb) * _sigmoid(zf)
        return 1.0 - f, _cumsum_rows(jnp.log(jnp.maximum(f, F_MIN)), reverse) * LOG2_E

    def intra_body(n, carry):
        rows = chunk_rows(n)
        q = zh_ref[rows, 0:W]
        v_bf = zh_ref[rows, W:2 * W].astype(BF16)
        k_f, a_f = gates(rows, False)
        k_b, a_b = gates(rows, True)

        scores = jnp.zeros((C, A_HEADS * C), F32)
        h = C // 2
        while h >= HGRN_BASE:
            split = lambda t: t.reshape(C // (2 * h), 2, h, W)
            q4, af4, ab4, kf4, kb4 = split(q), split(a_f), split(a_b), split(k_f), split(k_b)
            end_first = af4[:, 0, h - 1:h, :]
            start_second = ab4[:, 1, 0:1, :]
            q_fwd = q4[:, 1] * jnp.exp2(af4[:, 1] - end_first)
            k_fwd = kf4[:, 0] * jnp.exp2(end_first - af4[:, 0])
            q_bwd = q4[:, 0] * jnp.exp2(ab4[:, 0] - start_second)
            k_bwd = kb4[:, 1] * jnp.exp2(start_second - ab4[:, 1])
            qt = jnp.stack([q_bwd, q_fwd], axis=1).reshape(C, W).astype(BF16)
            kt = jnp.stack([k_fwd, k_bwd], axis=1).reshape(C, W).astype(BF16)
            scores = jnp.where(pair_level < 2 * h, _dot_nt(qt, stack_heads(kt)), scores)
            h //= 2
        hb = HGRN_BASE
        block = lambda t: t.reshape(C // hb, hb, W)
        base = []
        for a, kk, ref_at, keep in ((a_f, k_f, hb // 2 - 1, key_not_after), (a_b, k_b, hb // 2, key_not_before)):
            a3 = block(a)
            ref_row = a3[:, ref_at:ref_at + 1, :]
            qt = (block(q) * jnp.exp2(a3 - ref_row)).reshape(C, W).astype(BF16)
            kt = (block(kk) * jnp.exp2(ref_row - a3)).reshape(C, W).astype(BF16)
            base.append(jnp.where(keep, _dot_nt(qt, stack_heads(kt)), 0.0))
        scores = jnp.where(pair_level < hb, base[0] + base[1], scores)
        acc_ref[rows, :] = _dot(scores.astype(BF16), stack_heads(v_bf))

        for d, (a, kk, last) in enumerate(((a_f, k_f, C - 1), (a_b, k_b, 0))):
            a_last = a[last:last + 1, :]
            qe_ref[d, rows, :] = (q * jnp.exp2(a)).astype(BF16)
            decay_ref[n, d:d + 1, :] = a_last
            kv_ref[n, d] = jnp.where(bd_mask, _dot_tn(v_bf, (kk * jnp.exp2(a_last - a)).astype(BF16)), 0.0)
        return carry

    lax.fori_loop(0, n_seq * n_chunks, intra_body, 0, unroll=2)

    def state_body(i, carry):
        new = []
        for sq, states in enumerate(carry):
            first = sq * n_chunks
            for d, (st, n) in enumerate(zip(states, (first + i, first + n_chunks - 1 - i))):
                rows = chunk_rows(n)
                acc_ref[rows, :] += _dot_nt(qe_ref[d, rows, :], st.astype(BF16))
                new.append(st * jnp.exp2(decay_ref[n, d:d + 1, :]) + kv_ref[n, d])
        return tuple((new[2 * sq], new[2 * sq + 1]) for sq in range(n_seq))

    entry = [s0_ref[sq if s0_ref.shape[0] > 1 else 0] for sq in range(n_seq)]
    finals = lax.fori_loop(0, n_chunks, state_body, tuple((s[0], s[1]) for s in entry),
                           unroll=min(n_chunks, 4))

    ones_bd = _segment_ones(LANES, A_DK)

    def norm_body(n, carry):
        rows = pl.ds(pl.multiple_of(n * HGRN_NORM_ROWS, HGRN_NORM_ROWS), HGRN_NORM_ROWS)
        tot = acc_ref[rows, :]
        zg = zh_ref[rows, 4 * W:5 * W]
        cols = []
        for c in range(W // LANES):
            lanes = slice(c * LANES, (c + 1) * LANES)
            tc = tot[:, lanes]
            ms = _segment_sum(tc * tc, ones_bd) * (1.0 / A_DK)
            cols.append(tc * lax.rsqrt(ms + 1e-6) * ng_ref[:, lanes])
        on = jnp.concatenate(cols, axis=1)
        o_ref[rows, :] = (on * (zg * _sigmoid(zg))).astype(o_ref.dtype)
        return carry

    lax.fori_loop(0, n_seq * seq // HGRN_NORM_ROWS, norm_body, 0)

    for sq, states in enumerate(finals):
        for d, st in enumerate(states):
            s_kv = st.T
            for h in range(A_HEADS):
                blk = slice(h * A_DK, (h + 1) * A_DK)
                sfin_ref[sq, d, h] = s_kv[blk, blk]


def _hgrn(zh, s0, lb_logits, norm_g, layer, seq, seqs_per_step=1):
    rows = zh.shape[0]
    n_seq = rows // seq
    S, W = seqs_per_step, A_WIDTH
    if s0.shape[0] == 1:
        s0_spec = pl.BlockSpec((1, 2, W, W), lambda b: (0, 0, 0, 0))
    else:
        s0_spec = pl.BlockSpec((S, 2, W, W), lambda b: (b, 0, 0, 0))
    chunks = S * seq // HGRN_CHUNK
    return pl.pallas_call(
        functools.partial(_hgrn_kernel, layer=layer, seq=seq),
        grid=(n_seq // S,),
        in_specs=[
            pl.BlockSpec((S * seq, HG_WIDTH), lambda b: (b, 0)),
            s0_spec,
            _resident((DEPTH, W), lambda b: (0, 0)),
            _resident((None, 1, W), lambda b: (layer, 0, 0)),
        ],
        out_specs=[
            pl.BlockSpec((S * seq, W), lambda b: (b, 0)),
            pl.BlockSpec((S, 2, A_HEADS, A_DK, A_DK), lambda b: (b, 0, 0, 0, 0)),
        ],
        out_shape=[
            jax.ShapeDtypeStruct((rows, W), BF16),
            jax.ShapeDtypeStruct((n_seq, 2, A_HEADS, A_DK, A_DK), F32),
        ],
        scratch_shapes=[pltpu.VMEM((S * seq, W), F32), pltpu.VMEM((2, S * seq, W), BF16),
                        pltpu.VMEM((chunks, 2, W), F32), pltpu.VMEM((chunks, 2, W, W), F32)],
        compiler_params=_params(("parallel",), 48),
        name="hgrn2",
    )(zh, s0, lb_logits, norm_g)


def _conv_kernel(up_ref, uc_ref, un_ref, w_ref, b_ref, g_ref, beta_ref, o_ref, sh_ref, *, tiles_per_seq):
    i = pl.program_id(0)
    T, H, S = CONV_TILE, CONV_HALO, SUBLANES
    P = T + 2 * H
    first = (i % tiles_per_seq) == 0
    last = (i % tiles_per_seq) == tiles_per_seq - 1
    sh_ref[0, 0:H, :] = jnp.where(first, 0.0, up_ref[...])
    sh_ref[0, H:H + T, :] = uc_ref[...]
    sh_ref[0, H + T:P, :] = jnp.where(last, 0.0, un_ref[...])
    for b in range(1, S):
        sh_ref[b, 0:P - S, :] = sh_ref[0, b:b + P - S, :]
    acc = jnp.zeros((T, B_WIDTH), F32)
    for j in range(CONV_K):
        start = H - CONV_PAD + j
        a, b = start // S, start % S
        acc = acc + w_ref[j:j + 1, :] * sh_ref[b, a * S:a * S + T, :]
    y = _layer_norm(acc + b_ref[...], g_ref[...], beta_ref[...])
    o_ref[...] = (y * _sigmoid(y)).astype(o_ref.dtype)


def _conv(u, conv_w, conv_b, ln_g, ln_b, layer, seq):
    rows = u.shape[0]
    T, H = CONV_TILE, CONV_HALO
    tiles_per_seq = seq // T
    ratio = T // H
    n_halo_blocks = rows // H
    return pl.pallas_call(
        functools.partial(_conv_kernel, tiles_per_seq=tiles_per_seq),
        grid=(rows // T,),
        in_specs=[
            pl.BlockSpec((H, B_WIDTH), lambda i: (jnp.maximum(i * ratio - 1, 0), 0)),
            pl.BlockSpec((T, B_WIDTH), lambda i: (i, 0)),
            pl.BlockSpec((H, B_WIDTH), lambda i: (jnp.minimum((i + 1) * ratio, n_halo_blocks - 1), 0)),
            _resident((None, CONV_K, B_WIDTH), lambda i: (layer, 0, 0)),
            _resident((None, 1, B_WIDTH), lambda i: (layer, 0, 0)),
            _resident((None, 1, B_WIDTH), lambda i: (layer, 0, 0)),
            _resident((None, 1, B_WIDTH), lambda i: (layer, 0, 0)),
        ],
        out_specs=pl.BlockSpec((T, B_WIDTH), lambda i: (i, 0)),
        out_shape=jax.ShapeDtypeStruct((rows, B_WIDTH), BF16),
        scratch_shapes=[pltpu.VMEM((SUBLANES, T + 2 * H, B_WIDTH), F32)],
        compiler_params=_params(("parallel",), 16),
        name="conv_module",
    )(u, u, u, conv_w, conv_b, ln_g, ln_b)


def _attn_kernel(*refs, n_seq, single_q_tile, with_past):
    if with_past:
        q_ref, k_ref, v_ref, kp_ref, vp_ref, o_ref, kd_ref, va_ref = refs
        sources = ((kp_ref, vp_ref), (k_ref, v_ref))
    else:
        q_ref, k_ref, v_ref, o_ref, kd_ref, va_ref = refs
        sources = ((k_ref, v_ref),)
    tq = q_ref.shape[0] // n_seq
    seq_k = kd_ref.shape[1] // n_seq
    kb = min(seq_k, ATTN_KEY_BLOCK)
    group = C_HEADS // C_KV_HEADS
    half = lax.broadcasted_iota(jnp.int32, (1, LANES), 1) // HEAD_DIM

    def prepare():
        start = 0
        for ks_ref, vs_ref in sources:
            rows = slice(start, start + ks_ref.shape[0])
            start += ks_ref.shape[0]
            k = ks_ref[...]
            k_swapped = pltpu.roll(k, HEAD_DIM, 1)
            v = vs_ref[...]
            v_swapped = pltpu.roll(v, HEAD_DIM, 1)
            for g in range(C_KV_HEADS):
                kd_ref[g, rows, :] = jnp.where(half == g, k, k_swapped).astype(BF16)
                va_ref[g, rows, 0:LANES] = jnp.where(half == g, v, v_swapped).astype(BF16)
                va_ref[g, rows, LANES:2 * LANES] = jnp.ones(v.shape, BF16)

    if single_q_tile:
        prepare()
    else:
        pl.when(pl.program_id(1) == 0)(prepare)

    n_blocks = seq_k // kb
    chains = [(sq, g) for sq in range(n_seq) for g in range(C_KV_HEADS)]
    units = [(c, b) for b in range(n_blocks) for c in range(len(chains))]
    q_stacked, run_max, acc = {}, {}, {}
    scores, probs, rescale = {}, {}, {}

    def stage_scores(u):
        c, b = units[u]
        sq, g = chains[c]
        if c not in q_stacked:
            heads = []
            for j in range(group):
                hq = g * group + j
                col = q_ref[sq * tq:(sq + 1) * tq, (hq // 2) * LANES:(hq // 2 + 1) * LANES]
                heads.append(jnp.where(half == hq % 2, col, jnp.zeros_like(col)))
            q_stacked[c] = jnp.concatenate(heads, axis=0)
        keys = slice(sq * seq_k + b * kb, sq * seq_k + (b + 1) * kb)
        scores[u] = _dot_nt(q_stacked[c], kd_ref[g, keys, :])

    def stage_exp(u):
        c, b = units[u]
        s = scores.pop(u)
        m_blk = jnp.max(s, axis=-1, keepdims=True)
        m_new = m_blk if b == 0 else jnp.maximum(run_max[c], m_blk)
        probs[u] = jnp.exp2(s - m_new).astype(BF16)
        if b > 0:
            rescale[u] = jnp.exp2(run_max[c] - m_new)
        run_max[c] = m_new

    def stage_values(u):
        c, b = units[u]
        sq, g = chains[c]
        keys = slice(sq * seq_k + b * kb, sq * seq_k + (b + 1) * kb)
        pv = _dot(probs.pop(u), va_ref[g, keys, :])
        acc[c] = pv if b == 0 else rescale.pop(u) * acc[c] + pv
        if b == n_blocks - 1:
            total = acc.pop(c)
            o = total[:, 0:LANES] / total[:, LANES:2 * LANES]
            for j in range(group):
                hq = g * group + j
                o_ref[sq * tq:(sq + 1) * tq, hq * HEAD_DIM:(hq + 1) * HEAD_DIM] = (
                    o[j * tq:(j + 1) * tq, (hq % 2) * HEAD_DIM:(hq % 2 + 1) * HEAD_DIM].astype(o_ref.dtype))

    for t in range(len(units) + 2):
        if t < len(units):
            stage_scores(t)
        if 0 <= t - 1 < len(units):
            stage_exp(t - 1)
        if 0 <= t - 2 < len(units):
            stage_values(t - 2)


def _attention(q, k, v, seq, tq, seqs_per_step=1, past=None):
    n_seq = q.shape[0] // seq
    q_tiles = seq // tq
    assert seqs_per_step == 1 or (q_tiles == 1 and past is None)
    q_rows, k_rows = seqs_per_step * tq, seqs_per_step * seq
    in_specs = [
        pl.BlockSpec((q_rows, C_WIDTH), lambda b, i: (b * q_tiles + i, 0)),
        pl.BlockSpec((k_rows, KV_WIDTH), lambda b, i: (b, 0)),
        pl.BlockSpec((k_rows, KV_WIDTH), lambda b, i: (b, 0)),
    ]
    args = [q, k, v]
    keys_per_step = k_rows
    if past is not None:
        n_past = past[0].shape[1]
        in_specs += [pl.BlockSpec((None, n_past, KV_WIDTH), lambda b, i: (b, 0, 0))] * 2
        args += list(past)
        keys_per_step += n_past
    return pl.pallas_call(
        functools.partial(_attn_kernel, n_seq=seqs_per_step, single_q_tile=q_tiles == 1, with_past=past is not None),
        grid=(n_seq // seqs_per_step, q_tiles),
        in_specs=in_specs,
        out_specs=pl.BlockSpec((q_rows, C_WIDTH), lambda b, i: (b * q_tiles + i, 0)),
        out_shape=jax.ShapeDtypeStruct(q.shape, BF16),
        scratch_shapes=[pltpu.VMEM((C_KV_HEADS, keys_per_step, LANES), BF16),
                        pltpu.VMEM((C_KV_HEADS, keys_per_step, 2 * LANES), BF16)],
        compiler_params=_params(("parallel", "arbitrary"), 48),
        name="attention",
    )(*args)


def _rope_tables(seq):
    pos = np.arange(seq)
    row_id = (pos // GRID_W).astype(np.float32)
    col_id = (pos % GRID_W).astype(np.float32)
    inv = (np.float32(ROPE_THETA) ** (-np.arange(ROPE_PAIRS, dtype=np.float32) / ROPE_PAIRS)).astype(np.float32)
    lane = np.arange(LANES) % HEAD_DIM
    use_col = (lane // (2 * ROPE_PAIRS)) == 1
    first = (lane % (2 * ROPE_PAIRS)) < ROPE_PAIRS
    freq = inv[lane % ROPE_PAIRS]
    ang = (np.where(use_col[None, :], col_id[:, None], row_id[:, None]) * freq[None, :]).astype(np.float32)
    cos, sin = np.cos(ang).astype(np.float32), np.sin(ang).astype(np.float32)
    zero = np.float32(0.0)
    return tuple(jnp.asarray(t) for t in (cos, np.where(first[None, :], -sin, zero), np.where(first[None, :], zero, sin)))


def _state_to_block_diag(s):
    eye = jnp.eye(A_HEADS, dtype=s.dtype)
    st = jnp.swapaxes(s, -1, -2)
    bd = st[..., :, :, None, :] * eye[:, None, :, None]
    return bd.reshape(s.shape[:-3] + (A_WIDTH, A_WIDTH))


def kernel(x_prompt, x_sample, cache_k, cache_v, state_hgrn, c, c_ctx, w_mod, b_mod, ln_g, ln_b, ffn1_w_in, ffn1_w_out, ffn2_w_in, ffn2_w_out, w_in, w_out, hgrn_lb_logits, hgrn_norm_g, conv_w, conv_b, conv_ln_g, conv_ln_b, q_norm_g, k_norm_g):
    batch, seq, _ = x_prompt.shape
    dec_batch, dec_seq, _ = x_sample.shape
    past = cache_k.shape[2]
    assert (batch * seq) % ROW_TILE == 0 and dec_seq % ROW_TILE == 0
    assert dec_batch + 1 <= SUBLANES
    assert KV_WIDTH == LANES and C_HEADS % (2 * C_KV_HEADS) == 0

    bf = lambda w: w.astype(BF16)
    ffn1_w_in, ffn1_w_out, w_in = map(bf, (ffn1_w_in, ffn1_w_out, w_in))

    conds = jnp.zeros((SUBLANES, D_MODEL), F32).at[0].set(c_ctx).at[1:1 + dec_batch].set(c)
    mod = _modulation(conds, w_mod, b_mod).reshape(DEPTH, SUBLANES, N_MOD, D_MODEL)

    ln_g3 = ln_g.reshape(DEPTH * 3, 1, D_MODEL)
    ln_b3 = ln_b.reshape(DEPTH * 3, 1, D_MODEL)
    qg = jnp.tile(q_norm_g, (1, C_HEADS)).reshape(DEPTH, 1, C_WIDTH)
    kg = jnp.tile(k_norm_g, (1, C_KV_HEADS)).reshape(DEPTH, 1, KV_WIDTH)
    norm_g = hgrn_norm_g.reshape(DEPTH, 1, A_WIDTH)
    conv_b3 = conv_b.reshape(DEPTH, 1, B_WIDTH)
    conv_g3 = conv_ln_g.reshape(DEPTH, 1, B_WIDTH)
    conv_beta3 = conv_ln_b.reshape(DEPTH, 1, B_WIDTH)
    tables = _rope_tables(dec_seq)
    zero_state = jnp.zeros((1, 2, A_WIDTH, A_WIDTH), F32)
    lat_state = _state_to_block_diag(state_hgrn)

    ctx_cond = lambda row: 0
    lat_cond = lambda row: 1 + row // dec_seq

    xp = x_prompt.reshape(batch * seq, D_MODEL)
    xs = x_sample.reshape(dec_batch * dec_seq, D_MODEL)
    ks, vs, ss = [], [], []
    for l in range(DEPTH):
        mod_l = mod[l]

        def layer_fn(x, cond, sq, rope_tables, s0, k_past, v_past):
            x, zh, u, q, k, v = _ffn(x, mod_l, cond, ffn1_w_in, ffn1_w_out, ln_g3, ln_b3, l, 0,
                                     mixer_in=(w_in, qg, kg, rope_tables, sq))
            o_a, s_fin = _hgrn(zh, s0, hgrn_lb_logits, norm_g, l, sq,
                               seqs_per_step=HGRN_SEQS_PER_STEP)
            o_b = _conv(u, conv_w, conv_b3, conv_g3, conv_beta3, l, sq)
            if k_past is None:
                o_c = _attention(q, k, v, sq, sq, seqs_per_step=ATTN_SHORT_SEQS_PER_STEP)
            else:
                o_c = _attention(q, k, v, sq, ATTN_Q_TILE_LONG, past=(k_past, v_past))
            x, = _ffn(x, mod_l, cond, ffn2_w_in, ffn2_w_out, ln_g3, ln_b3, l, 2, mixer_out=(o_a, o_b, o_c, w_out))
            return x, k, v, s_fin

        xp, k_l, v_l, s_l = layer_fn(xp, ctx_cond, seq, None, zero_state, None, None)
        ks.append(k_l.reshape(batch, seq, C_KV_HEADS, HEAD_DIM))
        vs.append(v_l.reshape(batch, seq, C_KV_HEADS, HEAD_DIM))
        ss.append(s_l)
        xs, _, _, _ = layer_fn(xs, lat_cond, dec_seq, tables, lat_state[:, l],
                               cache_k[:, l].reshape(dec_batch, past, KV_WIDTH),
                               cache_v[:, l].reshape(dec_batch, past, KV_WIDTH))

    return (xp.reshape(batch, seq, D_MODEL), xs.reshape(dec_batch, dec_seq, D_MODEL),
            jnp.stack(ks, axis=1), jnp.stack(vs, axis=1), jnp.stack(ss, axis=1))
```

```python
import functools

import numpy as np
import jax
import jax.numpy as jnp
from jax import lax
from jax.experimental import pallas as pl
from jax.experimental.pallas import tpu as pltpu

F32 = jnp.float32
BF16 = jnp.bfloat16

D_MODEL = 1024
DEPTH = 2
GRID_W = 64
HEAD_DIM = 64
A_HEADS = 4
A_DK = 64
A_WIDTH = 256
B_WIDTH = 256
CONV_K = 31
CONV_PAD = 15
C_HEADS = 8
C_KV_HEADS = 2
C_WIDTH = 512
KV_WIDTH = C_KV_HEADS * HEAD_DIM
D_FF = 2816
ROPE_THETA = 10000.0
ROPE_PAIRS = 16
N_MOD = 9
ALPHA = (2 * DEPTH) ** 0.25
F_MIN = 1e-6
LOG2_E = 1.4426950408889634
IN_WIDTH = 5 * A_WIDTH + 2 * B_WIDTH + C_WIDTH + 2 * KV_WIDTH
HG_WIDTH = 5 * A_WIDTH

LANES = 128
SUBLANES = 8
VMEM_BYTES_V7X = 64 * 1024 * 1024
VMEM_RESERVE_BYTES = 4 * 1024 * 1024

ROW_TILE = 512
FFN_CHUNK = 256
HGRN_CHUNK = 128
HGRN_BASE = 8
HGRN_NORM_ROWS = 256
HGRN_SEQS_PER_STEP = 1
CONV_TILE = 256
CONV_HALO = 16
ATTN_Q_TILE_LONG = 256
ATTN_KEY_BLOCK = 512
ATTN_SHORT_SEQS_PER_STEP = 4


def _params(semantics, vmem_mb):
    return pltpu.CompilerParams(dimension_semantics=semantics,
                                vmem_limit_bytes=min(vmem_mb << 20, VMEM_BYTES_V7X - VMEM_RESERVE_BYTES))


def _resident(block_shape, index_map):
    return pl.BlockSpec(block_shape, index_map, pipeline_mode=pl.Buffered(1))


def _sigmoid(x):
    return 1.0 / (1.0 + jnp.exp(-x))


def _layer_norm(y, g, b, eps=1e-5):
    mu = jnp.mean(y, axis=-1, keepdims=True)
    d = y - mu
    var = jnp.mean(d * d, axis=-1, keepdims=True)
    return d * lax.rsqrt(var + eps) * g + b


def _dot(a, b):
    return jnp.dot(a, b, preferred_element_type=F32)


def _dot_nt(a, b):
    return lax.dot_general(a, b, (((1,), (1,)), ((), ())), preferred_element_type=F32)


def _dot_tn(a, b):
    return lax.dot_general(a, b, (((0,), (0,)), ((), ())), preferred_element_type=F32)


def _segment_ones(width, seg):
    r = lax.broadcasted_iota(jnp.int32, (width, width), 0) // seg
    c = lax.broadcasted_iota(jnp.int32, (width, width), 1) // seg
    return (r == c).astype(BF16)


def _segment_sum(x, ones_bd):
    hi = x.astype(BF16)
    lo = (x - hi.astype(F32)).astype(BF16)
    return _dot(hi, ones_bd) + _dot(lo, ones_bd)


def _head_rms_norm(x, gain, eps=1e-6):
    ones_bd = _segment_ones(LANES, HEAD_DIM)
    cols = []
    for c in range(x.shape[1] // LANES):
        xc = x[:, c * LANES:(c + 1) * LANES]
        ms = _segment_sum(xc * xc, ones_bd) * (1.0 / HEAD_DIM)
        cols.append(xc * lax.rsqrt(ms + eps) * gain[:, c * LANES:(c + 1) * LANES])
    return cols[0] if len(cols) == 1 else jnp.concatenate(cols, axis=1)


def _mod_kernel(c_ref, w_ref, b_ref, o_ref):
    c = c_ref[...]
    s = (c * _sigmoid(c)).astype(BF16)
    o_ref[...] = _dot(s, w_ref[...].astype(BF16)) + b_ref[...]


def _modulation(conds, w_mod, b_mod):
    tn = D_MODEL
    return pl.pallas_call(
        _mod_kernel,
        grid=(DEPTH, N_MOD * D_MODEL // tn),
        in_specs=[
            pl.BlockSpec((SUBLANES, D_MODEL), lambda l, j: (0, 0)),
            pl.BlockSpec((None, D_MODEL, tn), lambda l, j: (l, 0, j)),
            pl.BlockSpec((None, 1, tn), lambda l, j: (l, 0, j)),
        ],
        out_specs=pl.BlockSpec((None, SUBLANES, tn), lambda l, j: (l, 0, j)),
        out_shape=jax.ShapeDtypeStruct((DEPTH, SUBLANES, N_MOD * D_MODEL), F32),
        compiler_params=_params(("arbitrary", "arbitrary"), 32),
        name="modulation",
    )(conds, w_mod, b_mod.reshape(DEPTH, 1, N_MOD * D_MODEL))


def _ffn_kernel(*refs, sub, with_mixer_out, mixer_in_rope):
    refs = list(refs)
    x_ref = refs.pop(0)
    if with_mixer_out:
        oa_ref, ob_ref, oc_ref = refs[:3]
        del refs[:3]
    mod_ref = refs.pop(0)
    if with_mixer_out:
        wmix_ref, gmix_ref, bmix_ref = refs[:3]
        del refs[:3]
    wg_ref, wu_ref, wo_ref, g_ref, b_ref = refs[:5]
    del refs[:5]
    if mixer_in_rope is not None:
        n_in = 6 if mixer_in_rope else 3
        mixer_in_refs, refs = refs[:n_in], refs[n_in:]
    o_ref = refs.pop(0)
    mixer_in_outs = refs

    if with_mixer_out:
        y = _dot(oa_ref[...], wmix_ref[0:A_WIDTH, :].astype(BF16))
        y = y + _dot(ob_ref[...], wmix_ref[A_WIDTH:A_WIDTH + B_WIDTH, :].astype(BF16))
        y = y + _dot(oc_ref[...], wmix_ref[A_WIDTH + B_WIDTH:, :].astype(BF16))
        x = _layer_norm(ALPHA * x_ref[...] + mod_ref[5:6, :] * y, gmix_ref[...], bmix_ref[...])
    else:
        x = x_ref[...]
    shift = mod_ref[3 * sub:3 * sub + 1, :]
    scale = mod_ref[3 * sub + 1:3 * sub + 2, :]
    gate = mod_ref[3 * sub + 2:3 * sub + 3, :]
    h = (x * (1.0 + scale) + shift).astype(BF16)
    acc = jnp.zeros(x.shape, F32)
    for j in range(D_FF // FFN_CHUNK):
        cols = slice(j * FFN_CHUNK, (j + 1) * FFN_CHUNK)
        gt = _dot(h, wg_ref[:, cols].astype(BF16))
        up = _dot(h, wu_ref[:, cols].astype(BF16))
        act = (gt * _sigmoid(gt) * up).astype(BF16)
        acc = acc + _dot(act, wo_ref[cols, :].astype(BF16))
    y = ALPHA * x + 0.5 * gate * acc
    x_new = _layer_norm(y, g_ref[...], b_ref[...])
    o_ref[...] = x_new
    if mixer_in_rope is not None:
        _mixer_in(x_new, mod_ref, *mixer_in_refs, *mixer_in_outs)


def _ffn(x, mod_l, cond_of_row, w_in, w_out, ln_g, ln_b, layer, sub, mixer_out=None, mixer_in=None):
    rows = x.shape[0]
    tile = ROW_TILE
    row_spec = lambda width: pl.BlockSpec((tile, width), lambda i: (i, 0))
    cond_of_block = lambda i: cond_of_row(i * tile)
    ln_spec = lambda idx: _resident((None, 1, D_MODEL), lambda i: (layer * 3 + idx, 0, 0))
    in_specs, args = [row_spec(D_MODEL)], [x]
    if mixer_out is not None:
        o_a, o_b, o_c, w_mix = mixer_out
        in_specs += [row_spec(A_WIDTH), row_spec(B_WIDTH), row_spec(C_WIDTH)]
        args += [o_a, o_b, o_c]
    in_specs.append(pl.BlockSpec((None, N_MOD, D_MODEL), lambda i: (cond_of_block(i), 0, 0)))
    args.append(mod_l)
    if mixer_out is not None:
        in_specs += [_resident((None, D_MODEL, D_MODEL), lambda i: (layer, 0, 0)), ln_spec(1), ln_spec(1)]
        args += [w_mix, ln_g, ln_b]
    in_specs += [
        _resident((None, D_MODEL, D_FF), lambda i: (layer, 0, 0)),
        _resident((None, D_MODEL, D_FF), lambda i: (layer, 0, 1)),
        _resident((None, D_FF, D_MODEL), lambda i: (layer, 0, 0)),
        ln_spec(sub), ln_spec(sub),
    ]
    args += [w_in, w_in, w_out, ln_g, ln_b]
    out_specs = [row_spec(D_MODEL)]
    out_shape = [jax.ShapeDtypeStruct((rows, D_MODEL), F32)]
    rope = None
    if mixer_in is not None:
        w_mix_in, qg, kg, rope_tables, seq = mixer_in
        rope = rope_tables is not None
        in_specs += [
            _resident((None, D_MODEL, IN_WIDTH), lambda i: (layer, 0, 0)),
            _resident((None, 1, C_WIDTH), lambda i: (layer, 0, 0)),
            _resident((None, 1, KV_WIDTH), lambda i: (layer, 0, 0)),
        ]
        args += [w_mix_in, qg, kg]
        if rope:
            per_seq = seq // tile
            in_specs += [pl.BlockSpec((tile, LANES), lambda i: (i % per_seq, 0))] * 3
            args += list(rope_tables)
        widths = ((HG_WIDTH, F32), (B_WIDTH, F32), (C_WIDTH, BF16), (KV_WIDTH, F32), (KV_WIDTH, F32))
        out_specs += [row_spec(w) for w, _ in widths]
        out_shape += [jax.ShapeDtypeStruct((rows, w), dt) for w, dt in widths]
    return pl.pallas_call(
        functools.partial(_ffn_kernel, sub=sub, with_mixer_out=mixer_out is not None, mixer_in_rope=rope),
        grid=(rows // tile,),
        in_specs=in_specs,
        out_specs=out_specs,
        out_shape=out_shape,
        compiler_params=_params(("parallel",), 56),
        name=f"ffn{sub // 2 + 1}",
    )(*args)


def _mixer_in(x, mod_ref, w_ref, qg_ref, kg_ref, *refs):
    rope = len(refs) == 8
    if rope:
        cos_ref, s1_ref, s2_ref = refs[:3]
    zh_ref, u_ref, q_ref, k_ref, v_ref = refs[-5:]
    h = (x * (1.0 + mod_ref[4:5, :]) + mod_ref[3:4, :]).astype(BF16)

    zh_ref[...] = _dot(h, w_ref[:, 0:HG_WIDTH])
    o = HG_WIDTH
    glu_a = _dot(h, w_ref[:, o:o + B_WIDTH])
    glu_b = _dot(h, w_ref[:, o + B_WIDTH:o + 2 * B_WIDTH])
    u_ref[...] = glu_a * _sigmoid(glu_b)
    o += 2 * B_WIDTH
    cq = _dot(h, w_ref[:, o:o + C_WIDTH])
    ck = _dot(h, w_ref[:, o + C_WIDTH:o + C_WIDTH + KV_WIDTH])
    v_ref[...] = _dot(h, w_ref[:, o + C_WIDTH + KV_WIDTH:o + C_WIDTH + 2 * KV_WIDTH])

    qn = _head_rms_norm(cq, qg_ref[...])
    kn = _head_rms_norm(ck, kg_ref[...])
    if rope:
        cos, s1, s2 = cos_ref[...], s1_ref[...], s2_ref[...]

        def rot(t):
            cols = []
            for c in range(t.shape[1] // LANES):
                tc = t[:, c * LANES:(c + 1) * LANES]
                cols.append(tc * cos + pltpu.roll(tc, LANES - ROPE_PAIRS, 1) * s1
                            + pltpu.roll(tc, ROPE_PAIRS, 1) * s2)
            return cols[0] if len(cols) == 1 else jnp.concatenate(cols, axis=1)

        qn, kn = rot(qn), rot(kn)
    q_ref[...] = (qn * (HEAD_DIM ** -0.5 * LOG2_E)).astype(BF16)
    k_ref[...] = kn


def _cumsum_rows(x, reverse):
    n, w = x.shape
    row = lax.broadcasted_iota(jnp.int32, (SUBLANES, w), 0)
    groups = []
    for g in range(n // SUBLANES):
        grp = x[g * SUBLANES:(g + 1) * SUBLANES, :]
        s = 1
        while s < SUBLANES:
            if reverse:
                grp = grp + jnp.where(row < SUBLANES - s, pltpu.roll(grp, SUBLANES - s, 0), 0.0)
            else:
                grp = grp + jnp.where(row >= s, pltpu.roll(grp, s, 0), 0.0)
            s *= 2
        groups.append(grp)
    order = range(len(groups) - 1, -1, -1) if reverse else range(len(groups))
    carry = None
    for g in order:
        if carry is not None:
            groups[g] = groups[g] + carry
        carry = groups[g][0:1, :] if reverse else groups[g][SUBLANES - 1:SUBLANES, :]
    return jnp.concatenate(groups, axis=0)


def _hgrn_kernel(zh_ref, s0_ref, lbl_ref, ng_ref, *refs, layer, seq):
    o_ref, sfin_ref, acc_ref, qe_ref, decay_ref, kv_ref = refs[-6:]
    C, W = HGRN_CHUNK, A_WIDTH
    n_chunks = seq // C
    n_seq = zh_ref.shape[0] // seq

    lg = lbl_ref[...]
    e = jnp.exp(lg - jnp.max(lg, axis=0, keepdims=True))
    soft = e / jnp.sum(e, axis=0, keepdims=True)
    lb = jnp.zeros((1, W), F32)
    for j in range(1, layer + 1):
        lb = lb + soft[j:j + 1, :]

    lane_head = lax.broadcasted_iota(jnp.int32, (1, W), 1) // A_DK
    head_masks = [lane_head == h for h in range(A_HEADS)]
    t_idx = lax.broadcasted_iota(jnp.int32, (C, A_HEADS * C), 0)
    s_idx = lax.broadcasted_iota(jnp.int32, (C, A_HEADS * C), 1) % C
    pair_level = t_idx ^ s_idx
    key_not_after = s_idx <= t_idx
    key_not_before = s_idx >= t_idx
    bd_mask = (lax.broadcasted_iota(jnp.int32, (W, W), 0) // A_DK
               == lax.broadcasted_iota(jnp.int32, (W, W), 1) // A_DK)
    def stack_heads(t_bf):
        return jnp.concatenate([jnp.where(m, t_bf, jnp.zeros_like(t_bf)) for m in head_masks], axis=0)

    def chunk_rows(n):
        return pl.ds(pl.multiple_of(n * C, C), C)

    def gates(rows, reverse):
        zf = zh_ref[rows, (3 if reverse else 2) * W:(4 if reverse else 3) * W]
        f = lb + (1.0 - lb) * _sigmoid(zf)
        return 1.0 - f, _cumsum_rows(jnp.log(jnp.maximum(f, F_MIN)), reverse) * LOG2_E

    def intra_body(n, carry):
        rows = chunk_rows(n)
        q = zh_ref[rows, 0:W]
        v_bf = zh_ref[rows, W:2 * W].astype(BF16)
        k_f, a_f = gates(rows, False)
        k_b, a_b = gates(rows, True)

        scores = jnp.zeros((C, A_HEADS * C), F32)
        h = C // 2
        while h >= HGRN_BASE:
            split = lambda t: t.reshape(C // (2 * h), 2, h, W)
            q4, af4, ab4, kf4, kb4 = split(q), split(a_f), split(a_b), split(k_f), split(k_b)
            end_first = af4[:, 0, h - 1:h, :]
            start_second = ab4[:, 1, 0:1, :]
            q_fwd = q4[:, 1] * jnp.exp2(af4[:, 1] - end_first)
            k_fwd = kf4[:, 0] * jnp.exp2(end_first - af4[:, 0])
            q_bwd = q4[:, 0] * jnp.exp2(ab4[:, 0] - start_second)
            k_bwd = kb4[:, 1] * jnp.exp2(start_second - ab4[:, 1])
            qt = jnp.stack([q_bwd, q_fwd], axis=1).reshape(C, W).astype(BF16)
            kt = jnp.stack([k_fwd, k_bwd], axis=1).reshape(C, W).astype(BF16)
            scores = jnp.where(pair_level < 2 * h, _dot_nt(qt, stack_heads(kt)), scores)
            h //= 2
        hb = HGRN_BASE
        block = lambda t: t.reshape(C // hb, hb, W)
        base = []
        for a, kk, ref_at, keep in ((a_f, k_f, hb // 2 - 1, key_not_after), (a_b, k_b, hb // 2, key_not_before)):
            a3 = block(a)
            ref_row = a3[:, ref_at:ref_at + 1, :]
            qt = (block(q) * jnp.exp2(a3 - ref_row)).reshape(C, W).astype(BF16)
            kt = (block(kk) * jnp.exp2(ref_row - a3)).reshape(C, W).astype(BF16)
            base.append(jnp.where(keep, _dot_nt(qt, stack_heads(kt)), 0.0))
        scores = jnp.where(pair_level < hb, base[0] + base[1], scores)
        acc_ref[rows, :] = _dot(scores.astype(BF16), stack_heads(v_bf))

        for d, (a, kk, last) in enumerate(((a_f, k_f, C - 1), (a_b, k_b, 0))):
            a_last = a[last:last + 1, :]
            qe_ref[d, rows, :] = (q * jnp.exp2(a)).astype(BF16)
            decay_ref[n, d:d + 1, :] = a_last
            kv_ref[n, d] = jnp.where(bd_mask, _dot_tn(v_bf, (kk * jnp.exp2(a_last - a)).astype(BF16)), 0.0)
        return carry

    lax.fori_loop(0, n_seq * n_chunks, intra_body, 0, unroll=2)

    def state_body(i, carry):
        new = []
        for sq, states in enumerate(carry):
            first = sq * n_chunks
            for d, (st, n) in enumerate(zip(states, (first + i, first + n_chunks - 1 - i))):
                rows = chunk_rows(n)
                acc_ref[rows, :] += _dot_nt(qe_ref[d, rows, :], st.astype(BF16))
                new.append(st * jnp.exp2(decay_ref[n, d:d + 1, :]) + kv_ref[n, d])
        return tuple((new[2 * sq], new[2 * sq + 1]) for sq in range(n_seq))

    entry = [s0_ref[sq if s0_ref.shape[0] > 1 else 0] for sq in range(n_seq)]
    finals = lax.fori_loop(0, n_chunks, state_body, tuple((s[0], s[1]) for s in entry),
                           unroll=min(n_chunks, 4))

    ones_bd = _segment_ones(LANES, A_DK)

    def norm_body(n, carry):
        rows = pl.ds(pl.multiple_of(n * HGRN_NORM_ROWS, HGRN_NORM_ROWS), HGRN_NORM_ROWS)
        tot = acc_ref[rows, :]
        zg = zh_ref[rows, 4 * W:5 * W]
        cols = []
        for c in range(W // LANES):
            lanes = slice(c * LANES, (c + 1) * LANES)
            tc = tot[:, lanes]
            ms = _segment_sum(tc * tc, ones_bd) * (1.0 / A_DK)
            cols.append(tc * lax.rsqrt(ms + 1e-6) * ng_ref[:, lanes])
        on = jnp.concatenate(cols, axis=1)
        o_ref[rows, :] = (on * (zg * _sigmoid(zg))).astype(o_ref.dtype)
        return carry

    lax.fori_loop(0, n_seq * seq // HGRN_NORM_ROWS, norm_body, 0)

    for sq, states in enumerate(finals):
        for d, st in enumerate(states):
            s_kv = st.T
            for h in range(A_HEADS):
                blk = slice(h * A_DK, (h + 1) * A_DK)
                sfin_ref[sq, d, h] = s_kv[blk, blk]


def _hgrn(zh, s0, lb_logits, norm_g, layer, seq, seqs_per_step=1, states=None):
    rows = zh.shape[0]
    n_seq = rows // seq
    S, W = seqs_per_step, A_WIDTH
    if s0.shape[0] == 1:
        s0_spec = pl.BlockSpec((1, 2, W, W), lambda b: (0, 0, 0, 0))
    else:
        s0_spec = pl.BlockSpec((S, 2, W, W), lambda b: (b, 0, 0, 0))
    chunks = S * seq // HGRN_CHUNK
    in_specs = [
        pl.BlockSpec((S * seq, HG_WIDTH), lambda b: (b, 0)),
        s0_spec,
        _resident((DEPTH, W), lambda b: (0, 0)),
        _resident((None, 1, W), lambda b: (layer, 0, 0)),
    ]
    args = [zh, s0, lb_logits, norm_g]
    aliases = {}
    if states is not None:
        in_specs.append(pl.BlockSpec(memory_space=pl.ANY))
        args.append(states)
        aliases = {len(args) - 1: 1}
    return pl.pallas_call(
        functools.partial(_hgrn_kernel, layer=layer, seq=seq),
        grid=(n_seq // S,),
        in_specs=in_specs,
        out_specs=[
            pl.BlockSpec((S * seq, W), lambda b: (b, 0)),
            pl.BlockSpec((S, None, 2, A_HEADS, A_DK, A_DK), lambda b: (b, layer, 0, 0, 0, 0)),
        ],
        out_shape=[
            jax.ShapeDtypeStruct((rows, W), BF16),
            jax.ShapeDtypeStruct((n_seq, DEPTH, 2, A_HEADS, A_DK, A_DK), F32),
        ],
        input_output_aliases=aliases,
        scratch_shapes=[pltpu.VMEM((S * seq, W), F32), pltpu.VMEM((2, S * seq, W), BF16),
                        pltpu.VMEM((chunks, 2, W), F32), pltpu.VMEM((chunks, 2, W, W), F32)],
        compiler_params=_params(("parallel",), 48),
        name="hgrn2",
    )(*args)


def _conv_kernel(up_ref, uc_ref, un_ref, w_ref, b_ref, g_ref, beta_ref, o_ref, sh_ref, *, tiles_per_seq):
    i = pl.program_id(0)
    T, H, S = CONV_TILE, CONV_HALO, SUBLANES
    P = T + 2 * H
    first = (i % tiles_per_seq) == 0
    last = (i % tiles_per_seq) == tiles_per_seq - 1
    sh_ref[0, 0:H, :] = jnp.where(first, 0.0, up_ref[...])
    sh_ref[0, H:H + T, :] = uc_ref[...]
    sh_ref[0, H + T:P, :] = jnp.where(last, 0.0, un_ref[...])
    for b in range(1, S):
        sh_ref[b, 0:P - S, :] = sh_ref[0, b:b + P - S, :]
    acc = jnp.zeros((T, B_WIDTH), F32)
    for j in range(CONV_K):
        start = H - CONV_PAD + j
        a, b = start // S, start % S
        acc = acc + w_ref[j:j + 1, :] * sh_ref[b, a * S:a * S + T, :]
    y = _layer_norm(acc + b_ref[...], g_ref[...], beta_ref[...])
    o_ref[...] = (y * _sigmoid(y)).astype(o_ref.dtype)


def _conv(u, conv_w, conv_b, ln_g, ln_b, layer, seq):
    rows = u.shape[0]
    T, H = CONV_TILE, CONV_HALO
    tiles_per_seq = seq // T
    ratio = T // H
    n_halo_blocks = rows // H
    return pl.pallas_call(
        functools.partial(_conv_kernel, tiles_per_seq=tiles_per_seq),
        grid=(rows // T,),
        in_specs=[
            pl.BlockSpec((H, B_WIDTH), lambda i: (jnp.maximum(i * ratio - 1, 0), 0)),
            pl.BlockSpec((T, B_WIDTH), lambda i: (i, 0)),
            pl.BlockSpec((H, B_WIDTH), lambda i: (jnp.minimum((i + 1) * ratio, n_halo_blocks - 1), 0)),
            _resident((None, CONV_K, B_WIDTH), lambda i: (layer, 0, 0)),
            _resident((None, 1, B_WIDTH), lambda i: (layer, 0, 0)),
            _resident((None, 1, B_WIDTH), lambda i: (layer, 0, 0)),
            _resident((None, 1, B_WIDTH), lambda i: (layer, 0, 0)),
        ],
        out_specs=pl.BlockSpec((T, B_WIDTH), lambda i: (i, 0)),
        out_shape=jax.ShapeDtypeStruct((rows, B_WIDTH), BF16),
        scratch_shapes=[pltpu.VMEM((SUBLANES, T + 2 * H, B_WIDTH), F32)],
        compiler_params=_params(("parallel",), 16),
        name="conv_module",
    )(u, u, u, conv_w, conv_b, ln_g, ln_b)


def _attn_kernel(*refs, n_seq, single_q_tile, with_past):
    if with_past:
        q_ref, k_ref, v_ref, kp_ref, vp_ref, o_ref, kd_ref, va_ref = refs
        sources = ((kp_ref, vp_ref), (k_ref, v_ref))
    else:
        q_ref, k_ref, v_ref, o_ref, kd_ref, va_ref = refs
        sources = ((k_ref, v_ref),)
    tq = q_ref.shape[0] // n_seq
    seq_k = kd_ref.shape[1] // n_seq
    kb = min(seq_k, ATTN_KEY_BLOCK)
    group = C_HEADS // C_KV_HEADS
    half = lax.broadcasted_iota(jnp.int32, (1, LANES), 1) // HEAD_DIM

    def prepare():
        start = 0
        for ks_ref, vs_ref in sources:
            rows = slice(start, start + ks_ref.shape[0])
            start += ks_ref.shape[0]
            k = ks_ref[...]
            k_swapped = pltpu.roll(k, HEAD_DIM, 1)
            v = vs_ref[...]
            v_swapped = pltpu.roll(v, HEAD_DIM, 1)
            for g in range(C_KV_HEADS):
                kd_ref[g, rows, :] = jnp.where(half == g, k, k_swapped).astype(BF16)
                va_ref[g, rows, 0:LANES] = jnp.where(half == g, v, v_swapped).astype(BF16)
                va_ref[g, rows, LANES:2 * LANES] = jnp.ones(v.shape, BF16)

    if single_q_tile:
        prepare()
    else:
        pl.when(pl.program_id(1) == 0)(prepare)

    n_blocks = seq_k // kb
    chains = [(sq, g) for sq in range(n_seq) for g in range(C_KV_HEADS)]
    units = [(c, b) for b in range(n_blocks) for c in range(len(chains))]
    q_stacked, run_max, acc = {}, {}, {}
    scores, probs, rescale = {}, {}, {}

    def stage_scores(u):
        c, b = units[u]
        sq, g = chains[c]
        if c not in q_stacked:
            heads = []
            for j in range(group):
                hq = g * group + j
                col = q_ref[sq * tq:(sq + 1) * tq, (hq // 2) * LANES:(hq // 2 + 1) * LANES]
                heads.append(jnp.where(half == hq % 2, col, jnp.zeros_like(col)))
            q_stacked[c] = jnp.concatenate(heads, axis=0)
        keys = slice(sq * seq_k + b * kb, sq * seq_k + (b + 1) * kb)
        scores[u] = _dot_nt(q_stacked[c], kd_ref[g, keys, :])

    def stage_exp(u):
        c, b = units[u]
        s = scores.pop(u)
        m_blk = jnp.max(s, axis=-1, keepdims=True)
        m_new = m_blk if b == 0 else jnp.maximum(run_max[c], m_blk)
        probs[u] = jnp.exp2(s - m_new).astype(BF16)
        if b > 0:
            rescale[u] = jnp.exp2(run_max[c] - m_new)
        run_max[c] = m_new

    def stage_values(u):
        c, b = units[u]
        sq, g = chains[c]
        keys = slice(sq * seq_k + b * kb, sq * seq_k + (b + 1) * kb)
        pv = _dot(probs.pop(u), va_ref[g, keys, :])
        acc[c] = pv if b == 0 else rescale.pop(u) * acc[c] + pv
        if b == n_blocks - 1:
            total = acc.pop(c)
            o = total[:, 0:LANES] / total[:, LANES:2 * LANES]
            for j in range(group):
                hq = g * group + j
                o_ref[sq * tq:(sq + 1) * tq, hq * HEAD_DIM:(hq + 1) * HEAD_DIM] = (
                    o[j * tq:(j + 1) * tq, (hq % 2) * HEAD_DIM:(hq % 2 + 1) * HEAD_DIM].astype(o_ref.dtype))

    for t in range(len(units) + 2):
        if t < len(units):
            stage_scores(t)
        if 0 <= t - 1 < len(units):
            stage_exp(t - 1)
        if 0 <= t - 2 < len(units):
            stage_values(t - 2)


def _attention(q, k, v, seq, tq, seqs_per_step=1, past=None):
    n_seq = q.shape[0] // seq
    q_tiles = seq // tq
    assert seqs_per_step == 1 or (q_tiles == 1 and past is None)
    q_rows, k_rows = seqs_per_step * tq, seqs_per_step * seq
    in_specs = [
        pl.BlockSpec((q_rows, C_WIDTH), lambda b, i: (b * q_tiles + i, 0)),
        pl.BlockSpec((k_rows, KV_WIDTH), lambda b, i: (b, 0)),
        pl.BlockSpec((k_rows, KV_WIDTH), lambda b, i: (b, 0)),
    ]
    args = [q, k, v]
    keys_per_step = k_rows
    if past is not None:
        n_past = past[0].shape[1]
        in_specs += [pl.BlockSpec((None, n_past, KV_WIDTH), lambda b, i: (b, 0, 0))] * 2
        args += list(past)
        keys_per_step += n_past
    return pl.pallas_call(
        functools.partial(_attn_kernel, n_seq=seqs_per_step, single_q_tile=q_tiles == 1, with_past=past is not None),
        grid=(n_seq // seqs_per_step, q_tiles),
        in_specs=in_specs,
        out_specs=pl.BlockSpec((q_rows, C_WIDTH), lambda b, i: (b * q_tiles + i, 0)),
        out_shape=jax.ShapeDtypeStruct(q.shape, BF16),
        scratch_shapes=[pltpu.VMEM((C_KV_HEADS, keys_per_step, LANES), BF16),
                        pltpu.VMEM((C_KV_HEADS, keys_per_step, 2 * LANES), BF16)],
        compiler_params=_params(("parallel", "arbitrary"), 48),
        name="attention",
    )(*args)


def _rope_tables(seq):
    pos = np.arange(seq)
    row_id = (pos // GRID_W).astype(np.float32)
    col_id = (pos % GRID_W).astype(np.float32)
    inv = (np.float32(ROPE_THETA) ** (-np.arange(ROPE_PAIRS, dtype=np.float32) / ROPE_PAIRS)).astype(np.float32)
    lane = np.arange(LANES) % HEAD_DIM
    use_col = (lane // (2 * ROPE_PAIRS)) == 1
    first = (lane % (2 * ROPE_PAIRS)) < ROPE_PAIRS
    freq = inv[lane % ROPE_PAIRS]
    ang = (np.where(use_col[None, :], col_id[:, None], row_id[:, None]) * freq[None, :]).astype(np.float32)
    cos, sin = np.cos(ang).astype(np.float32), np.sin(ang).astype(np.float32)
    zero = np.float32(0.0)
    return tuple(jnp.asarray(t) for t in (cos, np.where(first[None, :], -sin, zero), np.where(first[None, :], zero, sin)))


def _state_to_block_diag(s):
    eye = jnp.eye(A_HEADS, dtype=s.dtype)
    st = jnp.swapaxes(s, -1, -2)
    bd = st[..., :, :, None, :] * eye[:, None, :, None]
    return bd.reshape(s.shape[:-3] + (A_WIDTH, A_WIDTH))


def kernel(x_prompt, x_sample, cache_k, cache_v, state_hgrn, c, c_ctx, w_mod, b_mod, ln_g, ln_b, ffn1_w_in, ffn1_w_out, ffn2_w_in, ffn2_w_out, w_in, w_out, hgrn_lb_logits, hgrn_norm_g, conv_w, conv_b, conv_ln_g, conv_ln_b, q_norm_g, k_norm_g):
    batch, seq, _ = x_prompt.shape
    dec_batch, dec_seq, _ = x_sample.shape
    past = cache_k.shape[2]
    assert (batch * seq) % ROW_TILE == 0 and dec_seq % ROW_TILE == 0
    assert dec_batch + 1 <= SUBLANES
    assert KV_WIDTH == LANES and C_HEADS % (2 * C_KV_HEADS) == 0

    bf = lambda w: w.astype(BF16)
    ffn1_w_in, ffn1_w_out, w_in = map(bf, (ffn1_w_in, ffn1_w_out, w_in))

    conds = jnp.zeros((SUBLANES, D_MODEL), F32).at[0].set(c_ctx).at[1:1 + dec_batch].set(c)
    mod = _modulation(conds, w_mod, b_mod).reshape(DEPTH, SUBLANES, N_MOD, D_MODEL)

    ln_g3 = ln_g.reshape(DEPTH * 3, 1, D_MODEL)
    ln_b3 = ln_b.reshape(DEPTH * 3, 1, D_MODEL)
    qg = jnp.tile(q_norm_g, (1, C_HEADS)).reshape(DEPTH, 1, C_WIDTH)
    kg = jnp.tile(k_norm_g, (1, C_KV_HEADS)).reshape(DEPTH, 1, KV_WIDTH)
    norm_g = hgrn_norm_g.reshape(DEPTH, 1, A_WIDTH)
    conv_b3 = conv_b.reshape(DEPTH, 1, B_WIDTH)
    conv_g3 = conv_ln_g.reshape(DEPTH, 1, B_WIDTH)
    conv_beta3 = conv_ln_b.reshape(DEPTH, 1, B_WIDTH)
    tables = _rope_tables(dec_seq)
    zero_state = jnp.zeros((1, 2, A_WIDTH, A_WIDTH), F32)
    lat_state = _state_to_block_diag(state_hgrn)

    ctx_cond = lambda row: 0
    lat_cond = lambda row: 1 + row // dec_seq

    xp = x_prompt.reshape(batch * seq, D_MODEL)
    xs = x_sample.reshape(dec_batch * dec_seq, D_MODEL)
    ks, vs, new_states = [], [], None
    for l in range(DEPTH):
        mod_l = mod[l]

        def layer_fn(x, cond, sq, rope_tables, s0, k_past, v_past, states):
            x, zh, u, q, k, v = _ffn(x, mod_l, cond, ffn1_w_in, ffn1_w_out, ln_g3, ln_b3, l, 0,
                                     mixer_in=(w_in, qg, kg, rope_tables, sq))
            o_a, s_fin = _hgrn(zh, s0, hgrn_lb_logits, norm_g, l, sq,
                               seqs_per_step=HGRN_SEQS_PER_STEP, states=states)
            o_b = _conv(u, conv_w, conv_b3, conv_g3, conv_beta3, l, sq)
            if k_past is None:
                o_c = _attention(q, k, v, sq, sq, seqs_per_step=ATTN_SHORT_SEQS_PER_STEP)
            else:
                o_c = _attention(q, k, v, sq, ATTN_Q_TILE_LONG, past=(k_past, v_past))
            x, = _ffn(x, mod_l, cond, ffn2_w_in, ffn2_w_out, ln_g3, ln_b3, l, 2, mixer_out=(o_a, o_b, o_c, w_out))
            return x, k, v, s_fin

        xp, k_l, v_l, new_states = layer_fn(xp, ctx_cond, seq, None, zero_state, None, None, new_states)
        ks.append(k_l.reshape(batch, seq, C_KV_HEADS, HEAD_DIM))
        vs.append(v_l.reshape(batch, seq, C_KV_HEADS, HEAD_DIM))
        xs, _, _, _ = layer_fn(xs, lat_cond, dec_seq, tables, lat_state[:, l],
                               cache_k[:, l].reshape(dec_batch, past, KV_WIDTH),
                               cache_v[:, l].reshape(dec_batch, past, KV_WIDTH), None)

    return (xp.reshape(batch, seq, D_MODEL), xs.reshape(dec_batch, dec_seq, D_MODEL),
            jnp.stack(ks, axis=1), jnp.stack(vs, axis=1), new_states)
```

```python
import functools

import numpy as np
import jax
import jax.numpy as jnp
from jax import lax
from jax.experimental import pallas as pl
from jax.experimental.pallas import tpu as pltpu

F32 = jnp.float32
BF16 = jnp.bfloat16

D_MODEL = 1024
DEPTH = 2
GRID_W = 64
HEAD_DIM = 64
A_HEADS = 4
A_DK = 64
A_WIDTH = 256
B_WIDTH = 256
CONV_K = 31
CONV_PAD = 15
C_HEADS = 8
C_KV_HEADS = 2
C_WIDTH = 512
KV_WIDTH = C_KV_HEADS * HEAD_DIM
D_FF = 2816
ROPE_THETA = 10000.0
ROPE_PAIRS = 16
N_MOD = 9
ALPHA = (2 * DEPTH) ** 0.25
F_MIN = 1e-6
LOG2_E = 1.4426950408889634
IN_WIDTH = 5 * A_WIDTH + 2 * B_WIDTH + C_WIDTH + 2 * KV_WIDTH
HG_WIDTH = 5 * A_WIDTH

LANES = 128
SUBLANES = 8
VMEM_BYTES_V7X = 64 * 1024 * 1024
VMEM_RESERVE_BYTES = 4 * 1024 * 1024

ROW_TILE = 512
FFN_CHUNK = 256
HGRN_CHUNK = 128
HGRN_BASE = 8
HGRN_NORM_ROWS = 256
HGRN_SEQS_PER_STEP = 1
CONV_TILE = 256
CONV_HALO = 16
ATTN_Q_TILE_LONG = 256
ATTN_KEY_BLOCK = 512
ATTN_SHORT_SEQS_PER_STEP = 4


def _params(semantics, vmem_mb):
    return pltpu.CompilerParams(dimension_semantics=semantics,
                                vmem_limit_bytes=min(vmem_mb << 20, VMEM_BYTES_V7X - VMEM_RESERVE_BYTES))


def _resident(block_shape, index_map):
    return pl.BlockSpec(block_shape, index_map, pipeline_mode=pl.Buffered(1))


def _sigmoid(x):
    return 1.0 / (1.0 + jnp.exp(-x))


def _layer_norm(y, g, b, eps=1e-5):
    mu = jnp.mean(y, axis=-1, keepdims=True)
    d = y - mu
    var = jnp.mean(d * d, axis=-1, keepdims=True)
    return d * lax.rsqrt(var + eps) * g + b


def _dot(a, b):
    return jnp.dot(a, b, preferred_element_type=F32)


def _dot_nt(a, b):
    return lax.dot_general(a, b, (((1,), (1,)), ((), ())), preferred_element_type=F32)


def _dot_tn(a, b):
    return lax.dot_general(a, b, (((0,), (0,)), ((), ())), preferred_element_type=F32)


def _segment_ones(width, seg):
    r = lax.broadcasted_iota(jnp.int32, (width, width), 0) // seg
    c = lax.broadcasted_iota(jnp.int32, (width, width), 1) // seg
    return (r == c).astype(BF16)


def _segment_sum(x, ones_bd):
    hi = x.astype(BF16)
    lo = (x - hi.astype(F32)).astype(BF16)
    return _dot(hi, ones_bd) + _dot(lo, ones_bd)


def _head_rms_norm(x, gain, eps=1e-6):
    ones_bd = _segment_ones(LANES, HEAD_DIM)
    cols = []
    for c in range(x.shape[1] // LANES):
        xc = x[:, c * LANES:(c + 1) * LANES]
        ms = _segment_sum(xc * xc, ones_bd) * (1.0 / HEAD_DIM)
        cols.append(xc * lax.rsqrt(ms + eps) * gain[:, c * LANES:(c + 1) * LANES])
    return cols[0] if len(cols) == 1 else jnp.concatenate(cols, axis=1)


def _mod_kernel(c_ref, w_ref, b_ref, o_ref):
    c = c_ref[...]
    s = (c * _sigmoid(c)).astype(BF16)
    o_ref[...] = _dot(s, w_ref[...].astype(BF16)) + b_ref[...]


def _modulation(conds, w_mod, b_mod):
    tn = D_MODEL
    return pl.pallas_call(
        _mod_kernel,
        grid=(DEPTH, N_MOD * D_MODEL // tn),
        in_specs=[
            pl.BlockSpec((SUBLANES, D_MODEL), lambda l, j: (0, 0)),
            pl.BlockSpec((None, D_MODEL, tn), lambda l, j: (l, 0, j)),
            pl.BlockSpec((None, 1, tn), lambda l, j: (l, 0, j)),
        ],
        out_specs=pl.BlockSpec((None, SUBLANES, tn), lambda l, j: (l, 0, j)),
        out_shape=jax.ShapeDtypeStruct((DEPTH, SUBLANES, N_MOD * D_MODEL), F32),
        compiler_params=_params(("arbitrary", "arbitrary"), 32),
        name="modulation",
    )(conds, w_mod, b_mod.reshape(DEPTH, 1, N_MOD * D_MODEL))


def _ffn_kernel(*refs, sub, with_mixer_out, mixer_in_rope):
    refs = list(refs)
    x_ref = refs.pop(0)
    if with_mixer_out:
        oa_ref, ob_ref, oc_ref = refs[:3]
        del refs[:3]
    mod_ref = refs.pop(0)
    if with_mixer_out:
        wmix_ref, gmix_ref, bmix_ref = refs[:3]
        del refs[:3]
    wg_ref, wu_ref, wo_ref, g_ref, b_ref = refs[:5]
    del refs[:5]
    if mixer_in_rope is not None:
        n_in = 6 if mixer_in_rope else 3
        mixer_in_refs, refs = refs[:n_in], refs[n_in:]
    o_ref = refs.pop(0)
    mixer_in_outs = refs

    if with_mixer_out:
        y = _dot(oa_ref[...], wmix_ref[0:A_WIDTH, :].astype(BF16))
        y = y + _dot(ob_ref[...], wmix_ref[A_WIDTH:A_WIDTH + B_WIDTH, :].astype(BF16))
        y = y + _dot(oc_ref[...], wmix_ref[A_WIDTH + B_WIDTH:, :].astype(BF16))
        x = _layer_norm(ALPHA * x_ref[...] + mod_ref[5:6, :] * y, gmix_ref[...], bmix_ref[...])
    else:
        x = x_ref[...]
    shift = mod_ref[3 * sub:3 * sub + 1, :]
    scale = mod_ref[3 * sub + 1:3 * sub + 2, :]
    gate = mod_ref[3 * sub + 2:3 * sub + 3, :]
    h = (x * (1.0 + scale) + shift).astype(BF16)
    acc = jnp.zeros(x.shape, F32)
    for j in range(D_FF // FFN_CHUNK):
        cols = slice(j * FFN_CHUNK, (j + 1) * FFN_CHUNK)
        gt = _dot(h, wg_ref[:, cols].astype(BF16))
        up = _dot(h, wu_ref[:, cols].astype(BF16))
        act = (gt * _sigmoid(gt) * up).astype(BF16)
        acc = acc + _dot(act, wo_ref[cols, :].astype(BF16))
    y = ALPHA * x + 0.5 * gate * acc
    x_new = _layer_norm(y, g_ref[...], b_ref[...])
    o_ref[...] = x_new
    if mixer_in_rope is not None:
        _mixer_in(x_new, mod_ref, *mixer_in_refs, *mixer_in_outs)


def _ffn(x, mod_l, cond_of_row, w_in, w_out, ln_g, ln_b, layer, sub, mixer_out=None, mixer_in=None):
    rows = x.shape[0]
    tile = ROW_TILE
    row_spec = lambda width: pl.BlockSpec((tile, width), lambda i: (i, 0))
    cond_of_block = lambda i: cond_of_row(i * tile)
    ln_spec = lambda idx: _resident((None, 1, D_MODEL), lambda i: (layer * 3 + idx, 0, 0))
    in_specs, args = [row_spec(D_MODEL)], [x]
    if mixer_out is not None:
        o_a, o_b, o_c, w_mix = mixer_out
        in_specs += [row_spec(A_WIDTH), row_spec(B_WIDTH), row_spec(C_WIDTH)]
        args += [o_a, o_b, o_c]
    in_specs.append(pl.BlockSpec((None, N_MOD, D_MODEL), lambda i: (cond_of_block(i), 0, 0)))
    args.append(mod_l)
    if mixer_out is not None:
        in_specs += [_resident((None, D_MODEL, D_MODEL), lambda i: (layer, 0, 0)), ln_spec(1), ln_spec(1)]
        args += [w_mix, ln_g, ln_b]
    in_specs += [
        _resident((None, D_MODEL, D_FF), lambda i: (layer, 0, 0)),
        _resident((None, D_MODEL, D_FF), lambda i: (layer, 0, 1)),
        _resident((None, D_FF, D_MODEL), lambda i: (layer, 0, 0)),
        ln_spec(sub), ln_spec(sub),
    ]
    args += [w_in, w_in, w_out, ln_g, ln_b]
    out_specs = [row_spec(D_MODEL)]
    out_shape = [jax.ShapeDtypeStruct((rows, D_MODEL), F32)]
    rope = None
    if mixer_in is not None:
        w_mix_in, qg, kg, rope_tables, seq = mixer_in
        rope = rope_tables is not None
        in_specs += [
            _resident((None, D_MODEL, IN_WIDTH), lambda i: (layer, 0, 0)),
            _resident((None, 1, C_WIDTH), lambda i: (layer, 0, 0)),
            _resident((None, 1, KV_WIDTH), lambda i: (layer, 0, 0)),
        ]
        args += [w_mix_in, qg, kg]
        if rope:
            per_seq = seq // tile
            in_specs += [pl.BlockSpec((tile, LANES), lambda i: (i % per_seq, 0))] * 3
            args += list(rope_tables)
        widths = ((HG_WIDTH, F32), (B_WIDTH, F32), (C_WIDTH, BF16), (KV_WIDTH, F32), (KV_WIDTH, F32))
        out_specs += [row_spec(w) for w, _ in widths]
        out_shape += [jax.ShapeDtypeStruct((rows, w), dt) for w, dt in widths]
    return pl.pallas_call(
        functools.partial(_ffn_kernel, sub=sub, with_mixer_out=mixer_out is not None, mixer_in_rope=rope),
        grid=(rows // tile,),
        in_specs=in_specs,
        out_specs=out_specs,
        out_shape=out_shape,
        compiler_params=_params(("parallel",), 56),
        name=f"ffn{sub // 2 + 1}",
    )(*args)


def _mixer_in(x, mod_ref, w_ref, qg_ref, kg_ref, *refs):
    rope = len(refs) == 8
    if rope:
        cos_ref, s1_ref, s2_ref = refs[:3]
    zh_ref, u_ref, q_ref, k_ref, v_ref = refs[-5:]
    h = (x * (1.0 + mod_ref[4:5, :]) + mod_ref[3:4, :]).astype(BF16)

    zh_ref[...] = _dot(h, w_ref[:, 0:HG_WIDTH])
    o = HG_WIDTH
    glu_a = _dot(h, w_ref[:, o:o + B_WIDTH])
    glu_b = _dot(h, w_ref[:, o + B_WIDTH:o + 2 * B_WIDTH])
    u_ref[...] = glu_a * _sigmoid(glu_b)
    o += 2 * B_WIDTH
    cq = _dot(h, w_ref[:, o:o + C_WIDTH])
    ck = _dot(h, w_ref[:, o + C_WIDTH:o + C_WIDTH + KV_WIDTH])
    v_ref[...] = _dot(h, w_ref[:, o + C_WIDTH + KV_WIDTH:o + C_WIDTH + 2 * KV_WIDTH])

    qn = _head_rms_norm(cq, qg_ref[...])
    kn = _head_rms_norm(ck, kg_ref[...])
    if rope:
        cos, s1, s2 = cos_ref[...], s1_ref[...], s2_ref[...]

        def rot(t):
            cols = []
            for c in range(t.shape[1] // LANES):
                tc = t[:, c * LANES:(c + 1) * LANES]
                cols.append(tc * cos + pltpu.roll(tc, LANES - ROPE_PAIRS, 1) * s1
                            + pltpu.roll(tc, ROPE_PAIRS, 1) * s2)
            return cols[0] if len(cols) == 1 else jnp.concatenate(cols, axis=1)

        qn, kn = rot(qn), rot(kn)
    q_ref[...] = (qn * (HEAD_DIM ** -0.5 * LOG2_E)).astype(BF16)
    k_ref[...] = kn


def _cumsum_rows(x, reverse):
    n, w = x.shape
    row = lax.broadcasted_iota(jnp.int32, (SUBLANES, w), 0)
    groups = []
    for g in range(n // SUBLANES):
        grp = x[g * SUBLANES:(g + 1) * SUBLANES, :]
        s = 1
        while s < SUBLANES:
            if reverse:
                grp = grp + jnp.where(row < SUBLANES - s, pltpu.roll(grp, SUBLANES - s, 0), 0.0)
            else:
                grp = grp + jnp.where(row >= s, pltpu.roll(grp, s, 0), 0.0)
            s *= 2
        groups.append(grp)
    order = range(len(groups) - 1, -1, -1) if reverse else range(len(groups))
    carry = None
    for g in order:
        if carry is not None:
            groups[g] = groups[g] + carry
        carry = groups[g][0:1, :] if reverse else groups[g][SUBLANES - 1:SUBLANES, :]
    return jnp.concatenate(groups, axis=0)


def _hgrn_kernel(zh_ref, s0_ref, lbl_ref, ng_ref, *refs, layer, seq):
    o_ref, sfin_ref, acc_ref, qe_ref, decay_ref, kv_ref = refs[-6:]
    C, W = HGRN_CHUNK, A_WIDTH
    n_chunks = seq // C
    n_seq = zh_ref.shape[0] // seq

    lg = lbl_ref[...]
    e = jnp.exp(lg - jnp.max(lg, axis=0, keepdims=True))
    soft = e / jnp.sum(e, axis=0, keepdims=True)
    lb = jnp.zeros((1, W), F32)
    for j in range(1, layer + 1):
        lb = lb + soft[j:j + 1, :]

    lane_head = lax.broadcasted_iota(jnp.int32, (1, W), 1) // A_DK
    head_masks = [lane_head == h for h in range(A_HEADS)]
    t_idx = lax.broadcasted_iota(jnp.int32, (C, A_HEADS * C), 0)
    s_idx = lax.broadcasted_iota(jnp.int32, (C, A_HEADS * C), 1) % C
    pair_level = t_idx ^ s_idx
    key_not_after = s_idx <= t_idx
    key_not_before = s_idx >= t_idx
    bd_mask = (lax.broadcasted_iota(jnp.int32, (W, W), 0) // A_DK
               == lax.broadcasted_iota(jnp.int32, (W, W), 1) // A_DK)
    def stack_heads(t_bf):
        return jnp.concatenate([jnp.where(m, t_bf, jnp.zeros_like(t_bf)) for m in head_masks], axis=0)

    def chunk_rows(n):
        return pl.ds(pl.multiple_of(n * C, C), C)

    def gates(rows, reverse):
        zf = zh_ref[rows, (3 if reverse else 2) * W:(4 if reverse else 3) * W]
        f = lb + (1.0 - lb) * _sigmoid(zf)
        return 1.0 - f, _cumsum_rows(jnp.log(jnp.maximum(f, F_MIN)), reverse) * LOG2_E

    def intra_body(n, carry):
        rows = chunk_rows(n)
        q = zh_ref[rows, 0:W]
        v_bf = zh_ref[rows, W:2 * W].astype(BF16)
        k_f, a_f = gates(rows, False)
        k_b, a_b = gates(rows, True)

        scores = jnp.zeros((C, A_HEADS * C), F32)
        h = C // 2
        while h >= HGRN_BASE:
            split = lambda t: t.reshape(C // (2 * h), 2, h, W)
            q4, af4, ab4, kf4, kb4 = split(q), split(a_f), split(a_b), split(k_f), split(k_b)
            end_first = af4[:, 0, h - 1:h, :]
            start_second = ab4[:, 1, 0:1, :]
            q_fwd = q4[:, 1] * jnp.exp2(af4[:, 1] - end_first)
            k_fwd = kf4[:, 0] * jnp.exp2(end_first - af4[:, 0])
            q_bwd = q4[:, 0] * jnp.exp2(ab4[:, 0] - start_second)
            k_bwd = kb4[:, 1] * jnp.exp2(start_second - ab4[:, 1])
            qt = jnp.stack([q_bwd, q_fwd], axis=1).reshape(C, W).astype(BF16)
            kt = jnp.stack([k_fwd, k_bwd], axis=1).reshape(C, W).astype(BF16)
            scores = jnp.where(pair_level < 2 * h, _dot_nt(qt, stack_heads(kt)), scores)
            h //= 2
        hb = HGRN_BASE
        block = lambda t: t.reshape(C // hb, hb, W)
        base = []
        for a, kk, ref_at, keep in ((a_f, k_f, hb // 2 - 1, key_not_after), (a_b, k_b, hb // 2, key_not_before)):
            a3 = block(a)
            ref_row = a3[:, ref_at:ref_at + 1, :]
            qt = (block(q) * jnp.exp2(a3 - ref_row)).reshape(C, W).astype(BF16)
            kt = (block(kk) * jnp.exp2(ref_row - a3)).reshape(C, W).astype(BF16)
            base.append(jnp.where(keep, _dot_nt(qt, stack_heads(kt)), 0.0))
        scores = jnp.where(pair_level < hb, base[0] + base[1], scores)
        acc_ref[rows, :] = _dot(scores.astype(BF16), stack_heads(v_bf))

        for d, (a, kk, last) in enumerate(((a_f, k_f, C - 1), (a_b, k_b, 0))):
            a_last = a[last:last + 1, :]
            qe_ref[d, rows, :] = (q * jnp.exp2(a)).astype(BF16)
            decay_ref[n, d:d + 1, :] = a_last
            kv_ref[n, d] = jnp.where(bd_mask, _dot_tn(v_bf, (kk * jnp.exp2(a_last - a)).astype(BF16)), 0.0)
        return carry

    lax.fori_loop(0, n_seq * n_chunks, intra_body, 0, unroll=2)

    def state_body(i, carry):
        new = []
        for sq, states in enumerate(carry):
            first = sq * n_chunks
            for d, (st, n) in enumerate(zip(states, (first + i, first + n_chunks - 1 - i))):
                rows = chunk_rows(n)
                acc_ref[rows, :] += _dot_nt(qe_ref[d, rows, :], st.astype(BF16))
                new.append(st * jnp.exp2(decay_ref[n, d:d + 1, :]) + kv_ref[n, d])
        return tuple((new[2 * sq], new[2 * sq + 1]) for sq in range(n_seq))

    entry = [s0_ref[sq if s0_ref.shape[0] > 1 else 0] for sq in range(n_seq)]
    finals = lax.fori_loop(0, n_chunks, state_body, tuple((s[0], s[1]) for s in entry),
                           unroll=min(n_chunks, 4))

    ones_bd = _segment_ones(LANES, A_DK)

    def norm_body(n, carry):
        rows = pl.ds(pl.multiple_of(n * HGRN_NORM_ROWS, HGRN_NORM_ROWS), HGRN_NORM_ROWS)
        tot = acc_ref[rows, :]
        zg = zh_ref[rows, 4 * W:5 * W]
        cols = []
        for c in range(W // LANES):
            lanes = slice(c * LANES, (c + 1) * LANES)
            tc = tot[:, lanes]
            ms = _segment_sum(tc * tc, ones_bd) * (1.0 / A_DK)
            cols.append(tc * lax.rsqrt(ms + 1e-6) * ng_ref[:, lanes])
        on = jnp.concatenate(cols, axis=1)
        o_ref[rows, :] = (on * (zg * _sigmoid(zg))).astype(o_ref.dtype)
        return carry

    lax.fori_loop(0, n_seq * seq // HGRN_NORM_ROWS, norm_body, 0)

    for sq, states in enumerate(finals):
        for d, st in enumerate(states):
            s_kv = st.T
            for h in range(A_HEADS):
                blk = slice(h * A_DK, (h + 1) * A_DK)
                if sfin_ref.ndim == 6:
                    for other in range(DEPTH):
                        sfin_ref[sq, other, d, h] = s_kv[blk, blk] if other == layer else jnp.zeros((A_DK, A_DK), F32)
                else:
                    sfin_ref[sq, d, h] = s_kv[blk, blk]


def _hgrn(zh, s0, lb_logits, norm_g, layer, seq, seqs_per_step=1, states="own"):
    rows = zh.shape[0]
    n_seq = rows // seq
    S, W = seqs_per_step, A_WIDTH
    if s0.shape[0] == 1:
        s0_spec = pl.BlockSpec((1, 2, W, W), lambda b: (0, 0, 0, 0))
    else:
        s0_spec = pl.BlockSpec((S, 2, W, W), lambda b: (b, 0, 0, 0))
    chunks = S * seq // HGRN_CHUNK
    in_specs = [
        pl.BlockSpec((S * seq, HG_WIDTH), lambda b: (b, 0)),
        s0_spec,
        _resident((DEPTH, W), lambda b: (0, 0)),
        _resident((None, 1, W), lambda b: (layer, 0, 0)),
    ]
    args = [zh, s0, lb_logits, norm_g]
    aliases = {}
    one_layer = (2, A_HEADS, A_DK, A_DK)
    if isinstance(states, str) and states == "own":
        state_spec = pl.BlockSpec((S,) + one_layer, lambda b: (b, 0, 0, 0, 0))
        state_shape = jax.ShapeDtypeStruct((n_seq,) + one_layer, F32)
    else:
        state_shape = jax.ShapeDtypeStruct((n_seq, DEPTH) + one_layer, F32)
        if isinstance(states, str):
            state_spec = pl.BlockSpec((S, DEPTH) + one_layer, lambda b: (b, 0, 0, 0, 0, 0))
        else:
            state_spec = pl.BlockSpec((S, None) + one_layer, lambda b: (b, layer, 0, 0, 0, 0))
            in_specs.append(pl.BlockSpec(memory_space=pl.ANY))
            args.append(states)
            aliases = {len(args) - 1: 1}
    return pl.pallas_call(
        functools.partial(_hgrn_kernel, layer=layer, seq=seq),
        grid=(n_seq // S,),
        in_specs=in_specs,
        out_specs=[pl.BlockSpec((S * seq, W), lambda b: (b, 0)), state_spec],
        out_shape=[jax.ShapeDtypeStruct((rows, W), BF16), state_shape],
        input_output_aliases=aliases,
        scratch_shapes=[pltpu.VMEM((S * seq, W), F32), pltpu.VMEM((2, S * seq, W), BF16),
                        pltpu.VMEM((chunks, 2, W), F32), pltpu.VMEM((chunks, 2, W, W), F32)],
        compiler_params=_params(("parallel",), 48),
        name="hgrn2",
    )(*args)


def _conv_kernel(up_ref, uc_ref, un_ref, w_ref, b_ref, g_ref, beta_ref, o_ref, sh_ref, *, tiles_per_seq):
    i = pl.program_id(0)
    T, H, S = CONV_TILE, CONV_HALO, SUBLANES
    P = T + 2 * H
    first = (i % tiles_per_seq) == 0
    last = (i % tiles_per_seq) == tiles_per_seq - 1
    sh_ref[0, 0:H, :] = jnp.where(first, 0.0, up_ref[...])
    sh_ref[0, H:H + T, :] = uc_ref[...]
    sh_ref[0, H + T:P, :] = jnp.where(last, 0.0, un_ref[...])
    for b in range(1, S):
        sh_ref[b, 0:P - S, :] = sh_ref[0, b:b + P - S, :]
    acc = jnp.zeros((T, B_WIDTH), F32)
    for j in range(CONV_K):
        start = H - CONV_PAD + j
        a, b = start // S, start % S
        acc = acc + w_ref[j:j + 1, :] * sh_ref[b, a * S:a * S + T, :]
    y = _layer_norm(acc + b_ref[...], g_ref[...], beta_ref[...])
    o_ref[...] = (y * _sigmoid(y)).astype(o_ref.dtype)


def _conv(u, conv_w, conv_b, ln_g, ln_b, layer, seq):
    rows = u.shape[0]
    T, H = CONV_TILE, CONV_HALO
    tiles_per_seq = seq // T
    ratio = T // H
    n_halo_blocks = rows // H
    return pl.pallas_call(
        functools.partial(_conv_kernel, tiles_per_seq=tiles_per_seq),
        grid=(rows // T,),
        in_specs=[
            pl.BlockSpec((H, B_WIDTH), lambda i: (jnp.maximum(i * ratio - 1, 0), 0)),
            pl.BlockSpec((T, B_WIDTH), lambda i: (i, 0)),
            pl.BlockSpec((H, B_WIDTH), lambda i: (jnp.minimum((i + 1) * ratio, n_halo_blocks - 1), 0)),
            _resident((None, CONV_K, B_WIDTH), lambda i: (layer, 0, 0)),
            _resident((None, 1, B_WIDTH), lambda i: (layer, 0, 0)),
            _resident((None, 1, B_WIDTH), lambda i: (layer, 0, 0)),
            _resident((None, 1, B_WIDTH), lambda i: (layer, 0, 0)),
        ],
        out_specs=pl.BlockSpec((T, B_WIDTH), lambda i: (i, 0)),
        out_shape=jax.ShapeDtypeStruct((rows, B_WIDTH), BF16),
        scratch_shapes=[pltpu.VMEM((SUBLANES, T + 2 * H, B_WIDTH), F32)],
        compiler_params=_params(("parallel",), 16),
        name="conv_module",
    )(u, u, u, conv_w, conv_b, ln_g, ln_b)


def _attn_kernel(*refs, n_seq, single_q_tile, with_past):
    if with_past:
        q_ref, k_ref, v_ref, kp_ref, vp_ref, o_ref, kd_ref, va_ref = refs
        sources = ((kp_ref, vp_ref), (k_ref, v_ref))
    else:
        q_ref, k_ref, v_ref, o_ref, kd_ref, va_ref = refs
        sources = ((k_ref, v_ref),)
    tq = q_ref.shape[0] // n_seq
    seq_k = kd_ref.shape[1] // n_seq
    kb = min(seq_k, ATTN_KEY_BLOCK)
    group = C_HEADS // C_KV_HEADS
    half = lax.broadcasted_iota(jnp.int32, (1, LANES), 1) // HEAD_DIM

    def prepare():
        start = 0
        for ks_ref, vs_ref in sources:
            rows = slice(start, start + ks_ref.shape[0])
            start += ks_ref.shape[0]
            k = ks_ref[...]
            k_swapped = pltpu.roll(k, HEAD_DIM, 1)
            v = vs_ref[...]
            v_swapped = pltpu.roll(v, HEAD_DIM, 1)
            for g in range(C_KV_HEADS):
                kd_ref[g, rows, :] = jnp.where(half == g, k, k_swapped).astype(BF16)
                va_ref[g, rows, 0:LANES] = jnp.where(half == g, v, v_swapped).astype(BF16)
                va_ref[g, rows, LANES:2 * LANES] = jnp.ones(v.shape, BF16)

    if single_q_tile:
        prepare()
    else:
        pl.when(pl.program_id(1) == 0)(prepare)

    n_blocks = seq_k // kb
    chains = [(sq, g) for sq in range(n_seq) for g in range(C_KV_HEADS)]
    units = [(c, b) for b in range(n_blocks) for c in range(len(chains))]
    q_stacked, run_max, acc = {}, {}, {}
    scores, probs, rescale = {}, {}, {}

    def stage_scores(u):
        c, b = units[u]
        sq, g = chains[c]
        if c not in q_stacked:
            heads = []
            for j in range(group):
                hq = g * group + j
                col = q_ref[sq * tq:(sq + 1) * tq, (hq // 2) * LANES:(hq // 2 + 1) * LANES]
                heads.append(jnp.where(half == hq % 2, col, jnp.zeros_like(col)))
            q_stacked[c] = jnp.concatenate(heads, axis=0)
        keys = slice(sq * seq_k + b * kb, sq * seq_k + (b + 1) * kb)
        scores[u] = _dot_nt(q_stacked[c], kd_ref[g, keys, :])

    def stage_exp(u):
        c, b = units[u]
        s = scores.pop(u)
        m_blk = jnp.max(s, axis=-1, keepdims=True)
        m_new = m_blk if b == 0 else jnp.maximum(run_max[c], m_blk)
        probs[u] = jnp.exp2(s - m_new).astype(BF16)
        if b > 0:
            rescale[u] = jnp.exp2(run_max[c] - m_new)
        run_max[c] = m_new

    def stage_values(u):
        c, b = units[u]
        sq, g = chains[c]
        keys = slice(sq * seq_k + b * kb, sq * seq_k + (b + 1) * kb)
        pv = _dot(probs.pop(u), va_ref[g, keys, :])
        acc[c] = pv if b == 0 else rescale.pop(u) * acc[c] + pv
        if b == n_blocks - 1:
            total = acc.pop(c)
            o = total[:, 0:LANES] / total[:, LANES:2 * LANES]
            for j in range(group):
                hq = g * group + j
                o_ref[sq * tq:(sq + 1) * tq, hq * HEAD_DIM:(hq + 1) * HEAD_DIM] = (
                    o[j * tq:(j + 1) * tq, (hq % 2) * HEAD_DIM:(hq % 2 + 1) * HEAD_DIM].astype(o_ref.dtype))

    for t in range(len(units) + 2):
        if t < len(units):
            stage_scores(t)
        if 0 <= t - 1 < len(units):
            stage_exp(t - 1)
        if 0 <= t - 2 < len(units):
            stage_values(t - 2)


def _attention(q, k, v, seq, tq, seqs_per_step=1, past=None):
    n_seq = q.shape[0] // seq
    q_tiles = seq // tq
    assert seqs_per_step == 1 or (q_tiles == 1 and past is None)
    q_rows, k_rows = seqs_per_step * tq, seqs_per_step * seq
    in_specs = [
        pl.BlockSpec((q_rows, C_WIDTH), lambda b, i: (b * q_tiles + i, 0)),
        pl.BlockSpec((k_rows, KV_WIDTH), lambda b, i: (b, 0)),
        pl.BlockSpec((k_rows, KV_WIDTH), lambda b, i: (b, 0)),
    ]
    args = [q, k, v]
    keys_per_step = k_rows
    if past is not None:
        n_past = past[0].shape[1]
        in_specs += [pl.BlockSpec((None, n_past, KV_WIDTH), lambda b, i: (b, 0, 0))] * 2
        args += list(past)
        keys_per_step += n_past
    return pl.pallas_call(
        functools.partial(_attn_kernel, n_seq=seqs_per_step, single_q_tile=q_tiles == 1, with_past=past is not None),
        grid=(n_seq // seqs_per_step, q_tiles),
        in_specs=in_specs,
        out_specs=pl.BlockSpec((q_rows, C_WIDTH), lambda b, i: (b * q_tiles + i, 0)),
        out_shape=jax.ShapeDtypeStruct(q.shape, BF16),
        scratch_shapes=[pltpu.VMEM((C_KV_HEADS, keys_per_step, LANES), BF16),
                        pltpu.VMEM((C_KV_HEADS, keys_per_step, 2 * LANES), BF16)],
        compiler_params=_params(("parallel", "arbitrary"), 48),
        name="attention",
    )(*args)


def _rope_tables(seq):
    pos = np.arange(seq)
    row_id = (pos // GRID_W).astype(np.float32)
    col_id = (pos % GRID_W).astype(np.float32)
    inv = (np.float32(ROPE_THETA) ** (-np.arange(ROPE_PAIRS, dtype=np.float32) / ROPE_PAIRS)).astype(np.float32)
    lane = np.arange(LANES) % HEAD_DIM
    use_col = (lane // (2 * ROPE_PAIRS)) == 1
    first = (lane % (2 * ROPE_PAIRS)) < ROPE_PAIRS
    freq = inv[lane % ROPE_PAIRS]
    ang = (np.where(use_col[None, :], col_id[:, None], row_id[:, None]) * freq[None, :]).astype(np.float32)
    cos, sin = np.cos(ang).astype(np.float32), np.sin(ang).astype(np.float32)
    zero = np.float32(0.0)
    return tuple(jnp.asarray(t) for t in (cos, np.where(first[None, :], -sin, zero), np.where(first[None, :], zero, sin)))


def _state_to_block_diag(s):
    eye = jnp.eye(A_HEADS, dtype=s.dtype)
    st = jnp.swapaxes(s, -1, -2)
    bd = st[..., :, :, None, :] * eye[:, None, :, None]
    return bd.reshape(s.shape[:-3] + (A_WIDTH, A_WIDTH))


def kernel(x_prompt, x_sample, cache_k, cache_v, state_hgrn, c, c_ctx, w_mod, b_mod, ln_g, ln_b, ffn1_w_in, ffn1_w_out, ffn2_w_in, ffn2_w_out, w_in, w_out, hgrn_lb_logits, hgrn_norm_g, conv_w, conv_b, conv_ln_g, conv_ln_b, q_norm_g, k_norm_g):
    batch, seq, _ = x_prompt.shape
    dec_batch, dec_seq, _ = x_sample.shape
    past = cache_k.shape[2]
    assert (batch * seq) % ROW_TILE == 0 and dec_seq % ROW_TILE == 0
    assert dec_batch + 1 <= SUBLANES
    assert KV_WIDTH == LANES and C_HEADS % (2 * C_KV_HEADS) == 0

    bf = lambda w: w.astype(BF16)
    ffn1_w_in, ffn1_w_out, w_in = map(bf, (ffn1_w_in, ffn1_w_out, w_in))

    conds = jnp.zeros((SUBLANES, D_MODEL), F32).at[0].set(c_ctx).at[1:1 + dec_batch].set(c)
    mod = _modulation(conds, w_mod, b_mod).reshape(DEPTH, SUBLANES, N_MOD, D_MODEL)

    ln_g3 = ln_g.reshape(DEPTH * 3, 1, D_MODEL)
    ln_b3 = ln_b.reshape(DEPTH * 3, 1, D_MODEL)
    qg = jnp.tile(q_norm_g, (1, C_HEADS)).reshape(DEPTH, 1, C_WIDTH)
    kg = jnp.tile(k_norm_g, (1, C_KV_HEADS)).reshape(DEPTH, 1, KV_WIDTH)
    norm_g = hgrn_norm_g.reshape(DEPTH, 1, A_WIDTH)
    conv_b3 = conv_b.reshape(DEPTH, 1, B_WIDTH)
    conv_g3 = conv_ln_g.reshape(DEPTH, 1, B_WIDTH)
    conv_beta3 = conv_ln_b.reshape(DEPTH, 1, B_WIDTH)
    tables = _rope_tables(dec_seq)
    zero_state = jnp.zeros((1, 2, A_WIDTH, A_WIDTH), F32)
    lat_state = _state_to_block_diag(state_hgrn)

    ctx_cond = lambda row: 0
    lat_cond = lambda row: 1 + row // dec_seq

    xp = x_prompt.reshape(batch * seq, D_MODEL)
    xs = x_sample.reshape(dec_batch * dec_seq, D_MODEL)
    ks, vs, new_states = [], [], "new"
    for l in range(DEPTH):
        mod_l = mod[l]

        def layer_fn(x, cond, sq, rope_tables, s0, k_past, v_past, states):
            x, zh, u, q, k, v = _ffn(x, mod_l, cond, ffn1_w_in, ffn1_w_out, ln_g3, ln_b3, l, 0,
                                     mixer_in=(w_in, qg, kg, rope_tables, sq))
            o_a, s_fin = _hgrn(zh, s0, hgrn_lb_logits, norm_g, l, sq,
                               seqs_per_step=HGRN_SEQS_PER_STEP, states=states)
            o_b = _conv(u, conv_w, conv_b3, conv_g3, conv_beta3, l, sq)
            if k_past is None:
                o_c = _attention(q, k, v, sq, sq, seqs_per_step=ATTN_SHORT_SEQS_PER_STEP)
            else:
                o_c = _attention(q, k, v, sq, ATTN_Q_TILE_LONG, past=(k_past, v_past))
            x, = _ffn(x, mod_l, cond, ffn2_w_in, ffn2_w_out, ln_g3, ln_b3, l, 2, mixer_out=(o_a, o_b, o_c, w_out))
            return x, k, v, s_fin

        xp, k_l, v_l, new_states = layer_fn(xp, ctx_cond, seq, None, zero_state, None, None, new_states)
        ks.append(k_l.reshape(batch, seq, C_KV_HEADS, HEAD_DIM))
        vs.append(v_l.reshape(batch, seq, C_KV_HEADS, HEAD_DIM))
        xs, _, _, _ = layer_fn(xs, lat_cond, dec_seq, tables, lat_state[:, l],
                               cache_k[:, l].reshape(dec_batch, past, KV_WIDTH),
                               cache_v[:, l].reshape(dec_batch, past, KV_WIDTH), "own")

    return (xp.reshape(batch, seq, D_MODEL), xs.reshape(dec_batch, dec_seq, D_MODEL),
            jnp.stack(ks, axis=1), jnp.stack(vs, axis=1), new_states)
```

```python
import functools

import numpy as np
import jax
import jax.numpy as jnp
from jax import lax
from jax.experimental import pallas as pl
from jax.experimental.pallas import tpu as pltpu

F32 = jnp.float32
BF16 = jnp.bfloat16

D_MODEL = 1024
DEPTH = 2
GRID_W = 64
HEAD_DIM = 64
A_HEADS = 4
A_DK = 64
A_WIDTH = 256
B_WIDTH = 256
CONV_K = 31
CONV_PAD = 15
C_HEADS = 8
C_KV_HEADS = 2
C_WIDTH = 512
KV_WIDTH = C_KV_HEADS * HEAD_DIM
D_FF = 2816
ROPE_THETA = 10000.0
ROPE_PAIRS = 16
N_MOD = 9
ALPHA = (2 * DEPTH) ** 0.25
F_MIN = 1e-6
LOG2_E = 1.4426950408889634
IN_WIDTH = 5 * A_WIDTH + 2 * B_WIDTH + C_WIDTH + 2 * KV_WIDTH
HG_WIDTH = 5 * A_WIDTH

LANES = 128
SUBLANES = 8
VMEM_BYTES_V7X = 64 * 1024 * 1024
VMEM_RESERVE_BYTES = 4 * 1024 * 1024

ROW_TILE = 512
FFN_CHUNK = 256
HGRN_CHUNK = 128
HGRN_BASE = 8
HGRN_NORM_ROWS = 256
HGRN_SEQS_PER_STEP = 1
CONV_TILE = 256
CONV_HALO = 16
ATTN_Q_TILE_LONG = 256
ATTN_KEY_BLOCK = 512
ATTN_SHORT_SEQS_PER_STEP = 4


def _params(semantics, vmem_mb):
    return pltpu.CompilerParams(dimension_semantics=semantics,
                                vmem_limit_bytes=min(vmem_mb << 20, VMEM_BYTES_V7X - VMEM_RESERVE_BYTES))


def _resident(block_shape, index_map):
    return pl.BlockSpec(block_shape, index_map, pipeline_mode=pl.Buffered(1))


def _sigmoid(x):
    return 1.0 / (1.0 + jnp.exp(-x))


def _layer_norm(y, g, b, eps=1e-5):
    mu = jnp.mean(y, axis=-1, keepdims=True)
    d = y - mu
    var = jnp.mean(d * d, axis=-1, keepdims=True)
    return d * lax.rsqrt(var + eps) * g + b


def _dot(a, b):
    return jnp.dot(a, b, preferred_element_type=F32)


def _dot_nt(a, b):
    return lax.dot_general(a, b, (((1,), (1,)), ((), ())), preferred_element_type=F32)


def _dot_tn(a, b):
    return lax.dot_general(a, b, (((0,), (0,)), ((), ())), preferred_element_type=F32)


def _segment_ones(width, seg):
    r = lax.broadcasted_iota(jnp.int32, (width, width), 0) // seg
    c = lax.broadcasted_iota(jnp.int32, (width, width), 1) // seg
    return (r == c).astype(BF16)


def _segment_sum(x, ones_bd):
    hi = x.astype(BF16)
    lo = (x - hi.astype(F32)).astype(BF16)
    return _dot(hi, ones_bd) + _dot(lo, ones_bd)


def _head_rms_norm(x, gain, eps=1e-6):
    ones_bd = _segment_ones(LANES, HEAD_DIM)
    cols = []
    for c in range(x.shape[1] // LANES):
        xc = x[:, c * LANES:(c + 1) * LANES]
        ms = _segment_sum(xc * xc, ones_bd) * (1.0 / HEAD_DIM)
        cols.append(xc * lax.rsqrt(ms + eps) * gain[:, c * LANES:(c + 1) * LANES])
    return cols[0] if len(cols) == 1 else jnp.concatenate(cols, axis=1)


def _mod_kernel(c_ref, w_ref, b_ref, o_ref):
    c = c_ref[...]
    s = (c * _sigmoid(c)).astype(BF16)
    o_ref[...] = _dot(s, w_ref[...].astype(BF16)) + b_ref[...]


def _modulation(conds, w_mod, b_mod):
    tn = D_MODEL
    return pl.pallas_call(
        _mod_kernel,
        grid=(DEPTH, N_MOD * D_MODEL // tn),
        in_specs=[
            pl.BlockSpec((SUBLANES, D_MODEL), lambda l, j: (0, 0)),
            pl.BlockSpec((None, D_MODEL, tn), lambda l, j: (l, 0, j)),
            pl.BlockSpec((None, 1, tn), lambda l, j: (l, 0, j)),
        ],
        out_specs=pl.BlockSpec((None, SUBLANES, tn), lambda l, j: (l, 0, j)),
        out_shape=jax.ShapeDtypeStruct((DEPTH, SUBLANES, N_MOD * D_MODEL), F32),
        compiler_params=_params(("arbitrary", "arbitrary"), 32),
        name="modulation",
    )(conds, w_mod, b_mod.reshape(DEPTH, 1, N_MOD * D_MODEL))


def _ffn_kernel(*refs, sub, with_mixer_out, mixer_in_rope):
    refs = list(refs)
    x_ref = refs.pop(0)
    if with_mixer_out:
        oa_ref, ob_ref, oc_ref = refs[:3]
        del refs[:3]
    mod_ref = refs.pop(0)
    if with_mixer_out:
        wmix_ref, gmix_ref, bmix_ref = refs[:3]
        del refs[:3]
    wg_ref, wu_ref, wo_ref, g_ref, b_ref = refs[:5]
    del refs[:5]
    if mixer_in_rope is not None:
        n_in = 6 if mixer_in_rope else 3
        mixer_in_refs, refs = refs[:n_in], refs[n_in:]
    o_ref = refs.pop(0)
    mixer_in_outs = refs

    if with_mixer_out:
        y = _dot(oa_ref[...], wmix_ref[0:A_WIDTH, :].astype(BF16))
        y = y + _dot(ob_ref[...], wmix_ref[A_WIDTH:A_WIDTH + B_WIDTH, :].astype(BF16))
        y = y + _dot(oc_ref[...], wmix_ref[A_WIDTH + B_WIDTH:, :].astype(BF16))
        x = _layer_norm(ALPHA * x_ref[...] + mod_ref[5:6, :] * y, gmix_ref[...], bmix_ref[...])
    else:
        x = x_ref[...]
    shift = mod_ref[3 * sub:3 * sub + 1, :]
    scale = mod_ref[3 * sub + 1:3 * sub + 2, :]
    gate = mod_ref[3 * sub + 2:3 * sub + 3, :]
    h = (x * (1.0 + scale) + shift).astype(BF16)
    acc = jnp.zeros(x.shape, F32)
    for j in range(D_FF // FFN_CHUNK):
        cols = slice(j * FFN_CHUNK, (j + 1) * FFN_CHUNK)
        gt = _dot(h, wg_ref[:, cols].astype(BF16))
        up = _dot(h, wu_ref[:, cols].astype(BF16))
        act = (gt * _sigmoid(gt) * up).astype(BF16)
        acc = acc + _dot(act, wo_ref[cols, :].astype(BF16))
    y = ALPHA * x + 0.5 * gate * acc
    x_new = _layer_norm(y, g_ref[...], b_ref[...])
    o_ref[...] = x_new
    if mixer_in_rope is not None:
        _mixer_in(x_new, mod_ref, *mixer_in_refs, *mixer_in_outs)


def _ffn(x, mod_l, cond_of_row, w_in, w_out, ln_g, ln_b, layer, sub, mixer_out=None, mixer_in=None):
    rows = x.shape[0]
    tile = ROW_TILE
    row_spec = lambda width: pl.BlockSpec((tile, width), lambda i: (i, 0))
    cond_of_block = lambda i: cond_of_row(i * tile)
    ln_spec = lambda idx: _resident((None, 1, D_MODEL), lambda i: (layer * 3 + idx, 0, 0))
    in_specs, args = [row_spec(D_MODEL)], [x]
    if mixer_out is not None:
        o_a, o_b, o_c, w_mix = mixer_out
        in_specs += [row_spec(A_WIDTH), row_spec(B_WIDTH), row_spec(C_WIDTH)]
        args += [o_a, o_b, o_c]
    in_specs.append(pl.BlockSpec((None, N_MOD, D_MODEL), lambda i: (cond_of_block(i), 0, 0)))
    args.append(mod_l)
    if mixer_out is not None:
        in_specs += [_resident((None, D_MODEL, D_MODEL), lambda i: (layer, 0, 0)), ln_spec(1), ln_spec(1)]
        args += [w_mix, ln_g, ln_b]
    in_specs += [
        _resident((None, D_MODEL, D_FF), lambda i: (layer, 0, 0)),
        _resident((None, D_MODEL, D_FF), lambda i: (layer, 0, 1)),
        _resident((None, D_FF, D_MODEL), lambda i: (layer, 0, 0)),
        ln_spec(sub), ln_spec(sub),
    ]
    args += [w_in, w_in, w_out, ln_g, ln_b]
    out_specs = [row_spec(D_MODEL)]
    out_shape = [jax.ShapeDtypeStruct((rows, D_MODEL), F32)]
    rope = None
    if mixer_in is not None:
        w_mix_in, qg, kg, rope_tables, seq = mixer_in
        rope = rope_tables is not None
        in_specs += [
            _resident((None, D_MODEL, IN_WIDTH), lambda i: (layer, 0, 0)),
            _resident((None, 1, C_WIDTH), lambda i: (layer, 0, 0)),
            _resident((None, 1, KV_WIDTH), lambda i: (layer, 0, 0)),
        ]
        args += [w_mix_in, qg, kg]
        if rope:
            per_seq = seq // tile
            in_specs += [pl.BlockSpec((tile, LANES), lambda i: (i % per_seq, 0))] * 3
            args += list(rope_tables)
        widths = ((HG_WIDTH, F32), (B_WIDTH, F32), (C_WIDTH, BF16), (KV_WIDTH, F32), (KV_WIDTH, F32))
        out_specs += [row_spec(w) for w, _ in widths]
        out_shape += [jax.ShapeDtypeStruct((rows, w), dt) for w, dt in widths]
    return pl.pallas_call(
        functools.partial(_ffn_kernel, sub=sub, with_mixer_out=mixer_out is not None, mixer_in_rope=rope),
        grid=(rows // tile,),
        in_specs=in_specs,
        out_specs=out_specs,
        out_shape=out_shape,
        compiler_params=_params(("parallel",), 56),
        name=f"ffn{sub // 2 + 1}",
    )(*args)


def _mixer_in(x, mod_ref, w_ref, qg_ref, kg_ref, *refs):
    rope = len(refs) == 8
    if rope:
        cos_ref, s1_ref, s2_ref = refs[:3]
    zh_ref, u_ref, q_ref, k_ref, v_ref = refs[-5:]
    h = (x * (1.0 + mod_ref[4:5, :]) + mod_ref[3:4, :]).astype(BF16)

    zh_ref[...] = _dot(h, w_ref[:, 0:HG_WIDTH])
    o = HG_WIDTH
    glu_a = _dot(h, w_ref[:, o:o + B_WIDTH])
    glu_b = _dot(h, w_ref[:, o + B_WIDTH:o + 2 * B_WIDTH])
    u_ref[...] = glu_a * _sigmoid(glu_b)
    o += 2 * B_WIDTH
    cq = _dot(h, w_ref[:, o:o + C_WIDTH])
    ck = _dot(h, w_ref[:, o + C_WIDTH:o + C_WIDTH + KV_WIDTH])
    v_ref[...] = _dot(h, w_ref[:, o + C_WIDTH + KV_WIDTH:o + C_WIDTH + 2 * KV_WIDTH])

    qn = _head_rms_norm(cq, qg_ref[...])
    kn = _head_rms_norm(ck, kg_ref[...])
    if rope:
        cos, s1, s2 = cos_ref[...], s1_ref[...], s2_ref[...]

        def rot(t):
            cols = []
            for c in range(t.shape[1] // LANES):
                tc = t[:, c * LANES:(c + 1) * LANES]
                cols.append(tc * cos + pltpu.roll(tc, LANES - ROPE_PAIRS, 1) * s1
                            + pltpu.roll(tc, ROPE_PAIRS, 1) * s2)
            return cols[0] if len(cols) == 1 else jnp.concatenate(cols, axis=1)

        qn, kn = rot(qn), rot(kn)
    q_ref[...] = (qn * (HEAD_DIM ** -0.5 * LOG2_E)).astype(BF16)
    k_ref[...] = kn


def _cumsum_rows(x, reverse):
    n, w = x.shape
    row = lax.broadcasted_iota(jnp.int32, (SUBLANES, w), 0)
    groups = []
    for g in range(n // SUBLANES):
        grp = x[g * SUBLANES:(g + 1) * SUBLANES, :]
        s = 1
        while s < SUBLANES:
            if reverse:
                grp = grp + jnp.where(row < SUBLANES - s, pltpu.roll(grp, SUBLANES - s, 0), 0.0)
            else:
                grp = grp + jnp.where(row >= s, pltpu.roll(grp, s, 0), 0.0)
            s *= 2
        groups.append(grp)
    order = range(len(groups) - 1, -1, -1) if reverse else range(len(groups))
    carry = None
    for g in order:
        if carry is not None:
            groups[g] = groups[g] + carry
        carry = groups[g][0:1, :] if reverse else groups[g][SUBLANES - 1:SUBLANES, :]
    return jnp.concatenate(groups, axis=0)


def _hgrn_kernel(zh_ref, s0_ref, lbl_ref, ng_ref, *refs, layer, seq):
    o_ref, sfin_ref, acc_ref, qe_ref, decay_ref, kv_ref = refs[-6:]
    C, W = HGRN_CHUNK, A_WIDTH
    n_chunks = seq // C
    n_seq = zh_ref.shape[0] // seq

    lg = lbl_ref[...]
    e = jnp.exp(lg - jnp.max(lg, axis=0, keepdims=True))
    soft = e / jnp.sum(e, axis=0, keepdims=True)
    lb = jnp.zeros((1, W), F32)
    for j in range(1, layer + 1):
        lb = lb + soft[j:j + 1, :]

    lane_head = lax.broadcasted_iota(jnp.int32, (1, W), 1) // A_DK
    head_masks = [lane_head == h for h in range(A_HEADS)]
    t_idx = lax.broadcasted_iota(jnp.int32, (C, A_HEADS * C), 0)
    s_idx = lax.broadcasted_iota(jnp.int32, (C, A_HEADS * C), 1) % C
    pair_level = t_idx ^ s_idx
    key_not_after = s_idx <= t_idx
    key_not_before = s_idx >= t_idx
    bd_mask = (lax.broadcasted_iota(jnp.int32, (W, W), 0) // A_DK
               == lax.broadcasted_iota(jnp.int32, (W, W), 1) // A_DK)
    def stack_heads(t_bf):
        return jnp.concatenate([jnp.where(m, t_bf, jnp.zeros_like(t_bf)) for m in head_masks], axis=0)

    def chunk_rows(n):
        return pl.ds(pl.multiple_of(n * C, C), C)

    def gates(rows, reverse):
        zf = zh_ref[rows, (3 if reverse else 2) * W:(4 if reverse else 3) * W]
        f = lb + (1.0 - lb) * _sigmoid(zf)
        return 1.0 - f, _cumsum_rows(jnp.log(jnp.maximum(f, F_MIN)), reverse) * LOG2_E

    def intra_body(n, carry):
        rows = chunk_rows(n)
        q = zh_ref[rows, 0:W]
        v_bf = zh_ref[rows, W:2 * W].astype(BF16)
        k_f, a_f = gates(rows, False)
        k_b, a_b = gates(rows, True)

        scores = jnp.zeros((C, A_HEADS * C), F32)
        h = C // 2
        while h >= HGRN_BASE:
            split = lambda t: t.reshape(C // (2 * h), 2, h, W)
            q4, af4, ab4, kf4, kb4 = split(q), split(a_f), split(a_b), split(k_f), split(k_b)
            end_first = af4[:, 0, h - 1:h, :]
            start_second = ab4[:, 1, 0:1, :]
            q_fwd = q4[:, 1] * jnp.exp2(af4[:, 1] - end_first)
            k_fwd = kf4[:, 0] * jnp.exp2(end_first - af4[:, 0])
            q_bwd = q4[:, 0] * jnp.exp2(ab4[:, 0] - start_second)
            k_bwd = kb4[:, 1] * jnp.exp2(start_second - ab4[:, 1])
            qt = jnp.stack([q_bwd, q_fwd], axis=1).reshape(C, W).astype(BF16)
            kt = jnp.stack([k_fwd, k_bwd], axis=1).reshape(C, W).astype(BF16)
            scores = jnp.where(pair_level < 2 * h, _dot_nt(qt, stack_heads(kt)), scores)
            h //= 2
        hb = HGRN_BASE
        block = lambda t: t.reshape(C // hb, hb, W)
        base = []
        for a, kk, ref_at, keep in ((a_f, k_f, hb // 2 - 1, key_not_after), (a_b, k_b, hb // 2, key_not_before)):
            a3 = block(a)
            ref_row = a3[:, ref_at:ref_at + 1, :]
            qt = (block(q) * jnp.exp2(a3 - ref_row)).reshape(C, W).astype(BF16)
            kt = (block(kk) * jnp.exp2(ref_row - a3)).reshape(C, W).astype(BF16)
            base.append(jnp.where(keep, _dot_nt(qt, stack_heads(kt)), 0.0))
        scores = jnp.where(pair_level < hb, base[0] + base[1], scores)
        acc_ref[rows, :] = _dot(scores.astype(BF16), stack_heads(v_bf))

        for d, (a, kk, last) in enumerate(((a_f, k_f, C - 1), (a_b, k_b, 0))):
            a_last = a[last:last + 1, :]
            qe_ref[d, rows, :] = (q * jnp.exp2(a)).astype(BF16)
            decay_ref[n, d:d + 1, :] = a_last
            kv_ref[n, d] = jnp.where(bd_mask, _dot_tn(v_bf, (kk * jnp.exp2(a_last - a)).astype(BF16)), 0.0)
        return carry

    lax.fori_loop(0, n_seq * n_chunks, intra_body, 0, unroll=min(n_seq * n_chunks, 4))

    def state_body(i, carry):
        new = []
        for sq, states in enumerate(carry):
            first = sq * n_chunks
            for d, (st, n) in enumerate(zip(states, (first + i, first + n_chunks - 1 - i))):
                rows = chunk_rows(n)
                acc_ref[rows, :] += _dot_nt(qe_ref[d, rows, :], st.astype(BF16))
                new.append(st * jnp.exp2(decay_ref[n, d:d + 1, :]) + kv_ref[n, d])
        return tuple((new[2 * sq], new[2 * sq + 1]) for sq in range(n_seq))

    entry = [s0_ref[sq if s0_ref.shape[0] > 1 else 0] for sq in range(n_seq)]
    finals = lax.fori_loop(0, n_chunks, state_body, tuple((s[0], s[1]) for s in entry),
                           unroll=min(n_chunks, 4))

    ones_bd = _segment_ones(LANES, A_DK)

    def norm_body(n, carry):
        rows = pl.ds(pl.multiple_of(n * HGRN_NORM_ROWS, HGRN_NORM_ROWS), HGRN_NORM_ROWS)
        tot = acc_ref[rows, :]
        zg = zh_ref[rows, 4 * W:5 * W]
        cols = []
        for c in range(W // LANES):
            lanes = slice(c * LANES, (c + 1) * LANES)
            tc = tot[:, lanes]
            ms = _segment_sum(tc * tc, ones_bd) * (1.0 / A_DK)
            cols.append(tc * lax.rsqrt(ms + 1e-6) * ng_ref[:, lanes])
        on = jnp.concatenate(cols, axis=1)
        o_ref[rows, :] = (on * (zg * _sigmoid(zg))).astype(o_ref.dtype)
        return carry

    lax.fori_loop(0, n_seq * seq // HGRN_NORM_ROWS, norm_body, 0)

    for sq, states in enumerate(finals):
        for d, st in enumerate(states):
            s_kv = st.T
            for h in range(A_HEADS):
                blk = slice(h * A_DK, (h + 1) * A_DK)
                if sfin_ref.ndim == 6:
                    for other in range(DEPTH):
                        sfin_ref[sq, other, d, h] = s_kv[blk, blk] if other == layer else jnp.zeros((A_DK, A_DK), F32)
                else:
                    sfin_ref[sq, d, h] = s_kv[blk, blk]


def _hgrn(zh, s0, lb_logits, norm_g, layer, seq, seqs_per_step=1, states="own"):
    rows = zh.shape[0]
    n_seq = rows // seq
    S, W = seqs_per_step, A_WIDTH
    if s0.shape[0] == 1:
        s0_spec = pl.BlockSpec((1, 2, W, W), lambda b: (0, 0, 0, 0))
    else:
        s0_spec = pl.BlockSpec((S, 2, W, W), lambda b: (b, 0, 0, 0))
    chunks = S * seq // HGRN_CHUNK
    in_specs = [
        pl.BlockSpec((S * seq, HG_WIDTH), lambda b: (b, 0)),
        s0_spec,
        _resident((DEPTH, W), lambda b: (0, 0)),
        _resident((None, 1, W), lambda b: (layer, 0, 0)),
    ]
    args = [zh, s0, lb_logits, norm_g]
    aliases = {}
    one_layer = (2, A_HEADS, A_DK, A_DK)
    if isinstance(states, str) and states == "own":
        state_spec = pl.BlockSpec((S,) + one_layer, lambda b: (b, 0, 0, 0, 0))
        state_shape = jax.ShapeDtypeStruct((n_seq,) + one_layer, F32)
    else:
        state_shape = jax.ShapeDtypeStruct((n_seq, DEPTH) + one_layer, F32)
        if isinstance(states, str):
            state_spec = pl.BlockSpec((S, DEPTH) + one_layer, lambda b: (b, 0, 0, 0, 0, 0))
        else:
            state_spec = pl.BlockSpec((S, None) + one_layer, lambda b: (b, layer, 0, 0, 0, 0))
            in_specs.append(pl.BlockSpec(memory_space=pl.ANY))
            args.append(states)
            aliases = {len(args) - 1: 1}
    return pl.pallas_call(
        functools.partial(_hgrn_kernel, layer=layer, seq=seq),
        grid=(n_seq // S,),
        in_specs=in_specs,
        out_specs=[pl.BlockSpec((S * seq, W), lambda b: (b, 0)), state_spec],
        out_shape=[jax.ShapeDtypeStruct((rows, W), BF16), state_shape],
        input_output_aliases=aliases,
        scratch_shapes=[pltpu.VMEM((S * seq, W), F32), pltpu.VMEM((2, S * seq, W), BF16),
                        pltpu.VMEM((chunks, 2, W), F32), pltpu.VMEM((chunks, 2, W, W), F32)],
        compiler_params=_params(("parallel",), 48),
        name="hgrn2",
    )(*args)


def _conv_kernel(up_ref, uc_ref, un_ref, w_ref, b_ref, g_ref, beta_ref, o_ref, sh_ref, *, tiles_per_seq):
    i = pl.program_id(0)
    T, H, S = CONV_TILE, CONV_HALO, SUBLANES
    P = T + 2 * H
    first = (i % tiles_per_seq) == 0
    last = (i % tiles_per_seq) == tiles_per_seq - 1
    sh_ref[0, 0:H, :] = jnp.where(first, 0.0, up_ref[...])
    sh_ref[0, H:H + T, :] = uc_ref[...]
    sh_ref[0, H + T:P, :] = jnp.where(last, 0.0, un_ref[...])
    for b in range(1, S):
        sh_ref[b, 0:P - S, :] = sh_ref[0, b:b + P - S, :]
    acc = jnp.zeros((T, B_WIDTH), F32)
    for j in range(CONV_K):
        start = H - CONV_PAD + j
        a, b = start // S, start % S
        acc = acc + w_ref[j:j + 1, :] * sh_ref[b, a * S:a * S + T, :]
    y = _layer_norm(acc + b_ref[...], g_ref[...], beta_ref[...])
    o_ref[...] = (y * _sigmoid(y)).astype(o_ref.dtype)


def _conv(u, conv_w, conv_b, ln_g, ln_b, layer, seq):
    rows = u.shape[0]
    T, H = CONV_TILE, CONV_HALO
    tiles_per_seq = seq // T
    ratio = T // H
    n_halo_blocks = rows // H
    return pl.pallas_call(
        functools.partial(_conv_kernel, tiles_per_seq=tiles_per_seq),
        grid=(rows // T,),
        in_specs=[
            pl.BlockSpec((H, B_WIDTH), lambda i: (jnp.maximum(i * ratio - 1, 0), 0)),
            pl.BlockSpec((T, B_WIDTH), lambda i: (i, 0)),
            pl.BlockSpec((H, B_WIDTH), lambda i: (jnp.minimum((i + 1) * ratio, n_halo_blocks - 1), 0)),
            _resident((None, CONV_K, B_WIDTH), lambda i: (layer, 0, 0)),
            _resident((None, 1, B_WIDTH), lambda i: (layer, 0, 0)),
            _resident((None, 1, B_WIDTH), lambda i: (layer, 0, 0)),
            _resident((None, 1, B_WIDTH), lambda i: (layer, 0, 0)),
        ],
        out_specs=pl.BlockSpec((T, B_WIDTH), lambda i: (i, 0)),
        out_shape=jax.ShapeDtypeStruct((rows, B_WIDTH), BF16),
        scratch_shapes=[pltpu.VMEM((SUBLANES, T + 2 * H, B_WIDTH), F32)],
        compiler_params=_params(("parallel",), 16),
        name="conv_module",
    )(u, u, u, conv_w, conv_b, ln_g, ln_b)


def _attn_kernel(*refs, n_seq, single_q_tile, with_past):
    if with_past:
        q_ref, k_ref, v_ref, kp_ref, vp_ref, o_ref, kd_ref, va_ref = refs
        sources = ((kp_ref, vp_ref), (k_ref, v_ref))
    else:
        q_ref, k_ref, v_ref, o_ref, kd_ref, va_ref = refs
        sources = ((k_ref, v_ref),)
    tq = q_ref.shape[0] // n_seq
    seq_k = kd_ref.shape[1] // n_seq
    kb = min(seq_k, ATTN_KEY_BLOCK)
    group = C_HEADS // C_KV_HEADS
    half = lax.broadcasted_iota(jnp.int32, (1, LANES), 1) // HEAD_DIM

    def prepare():
        start = 0
        for ks_ref, vs_ref in sources:
            rows = slice(start, start + ks_ref.shape[0])
            start += ks_ref.shape[0]
            k = ks_ref[...]
            k_swapped = pltpu.roll(k, HEAD_DIM, 1)
            v = vs_ref[...]
            v_swapped = pltpu.roll(v, HEAD_DIM, 1)
            for g in range(C_KV_HEADS):
                kd_ref[g, rows, :] = jnp.where(half == g, k, k_swapped).astype(BF16)
                va_ref[g, rows, 0:LANES] = jnp.where(half == g, v, v_swapped).astype(BF16)
                va_ref[g, rows, LANES:2 * LANES] = jnp.ones(v.shape, BF16)

    if single_q_tile:
        prepare()
    else:
        pl.when(pl.program_id(1) == 0)(prepare)

    n_blocks = seq_k // kb
    chains = [(sq, g) for sq in range(n_seq) for g in range(C_KV_HEADS)]
    units = [(c, b) for b in range(n_blocks) for c in range(len(chains))]
    q_stacked, run_max, acc = {}, {}, {}
    scores, probs, rescale = {}, {}, {}

    def stage_scores(u):
        c, b = units[u]
        sq, g = chains[c]
        if c not in q_stacked:
            heads = []
            for j in range(group):
                hq = g * group + j
                col = q_ref[sq * tq:(sq + 1) * tq, (hq // 2) * LANES:(hq // 2 + 1) * LANES]
                heads.append(jnp.where(half == hq % 2, col, jnp.zeros_like(col)))
            q_stacked[c] = jnp.concatenate(heads, axis=0)
        keys = slice(sq * seq_k + b * kb, sq * seq_k + (b + 1) * kb)
        scores[u] = _dot_nt(q_stacked[c], kd_ref[g, keys, :])

    def stage_exp(u):
        c, b = units[u]
        s = scores.pop(u)
        m_blk = jnp.max(s, axis=-1, keepdims=True)
        m_new = m_blk if b == 0 else jnp.maximum(run_max[c], m_blk)
        probs[u] = jnp.exp2(s - m_new).astype(BF16)
        if b > 0:
            rescale[u] = jnp.exp2(run_max[c] - m_new)
        run_max[c] = m_new

    def stage_values(u):
        c, b = units[u]
        sq, g = chains[c]
        keys = slice(sq * seq_k + b * kb, sq * seq_k + (b + 1) * kb)
        pv = _dot(probs.pop(u), va_ref[g, keys, :])
        acc[c] = pv if b == 0 else rescale.pop(u) * acc[c] + pv
        if b == n_blocks - 1:
            total = acc.pop(c)
            o = total[:, 0:LANES] / total[:, LANES:2 * LANES]
            for j in range(group):
                hq = g * group + j
                o_ref[sq * tq:(sq + 1) * tq, hq * HEAD_DIM:(hq + 1) * HEAD_DIM] = (
                    o[j * tq:(j + 1) * tq, (hq % 2) * HEAD_DIM:(hq % 2 + 1) * HEAD_DIM].astype(o_ref.dtype))

    for t in range(len(units) + 2):
        if t < len(units):
            stage_scores(t)
        if 0 <= t - 1 < len(units):
            stage_exp(t - 1)
        if 0 <= t - 2 < len(units):
            stage_values(t - 2)


def _attention(q, k, v, seq, tq, seqs_per_step=1, past=None):
    n_seq = q.shape[0] // seq
    q_tiles = seq // tq
    assert seqs_per_step == 1 or (q_tiles == 1 and past is None)
    q_rows, k_rows = seqs_per_step * tq, seqs_per_step * seq
    in_specs = [
        pl.BlockSpec((q_rows, C_WIDTH), lambda b, i: (b * q_tiles + i, 0)),
        pl.BlockSpec((k_rows, KV_WIDTH), lambda b, i: (b, 0)),
        pl.BlockSpec((k_rows, KV_WIDTH), lambda b, i: (b, 0)),
    ]
    args = [q, k, v]
    keys_per_step = k_rows
    if past is not None:
        n_past = past[0].shape[1]
        in_specs += [pl.BlockSpec((None, n_past, KV_WIDTH), lambda b, i: (b, 0, 0))] * 2
        args += list(past)
        keys_per_step += n_past
    return pl.pallas_call(
        functools.partial(_attn_kernel, n_seq=seqs_per_step, single_q_tile=q_tiles == 1, with_past=past is not None),
        grid=(n_seq // seqs_per_step, q_tiles),
        in_specs=in_specs,
        out_specs=pl.BlockSpec((q_rows, C_WIDTH), lambda b, i: (b * q_tiles + i, 0)),
        out_shape=jax.ShapeDtypeStruct(q.shape, BF16),
        scratch_shapes=[pltpu.VMEM((C_KV_HEADS, keys_per_step, LANES), BF16),
                        pltpu.VMEM((C_KV_HEADS, keys_per_step, 2 * LANES), BF16)],
        compiler_params=_params(("parallel", "arbitrary"), 48),
        name="attention",
    )(*args)


def _rope_tables(seq):
    pos = np.arange(seq)
    row_id = (pos // GRID_W).astype(np.float32)
    col_id = (pos % GRID_W).astype(np.float32)
    inv = (np.float32(ROPE_THETA) ** (-np.arange(ROPE_PAIRS, dtype=np.float32) / ROPE_PAIRS)).astype(np.float32)
    lane = np.arange(LANES) % HEAD_DIM
    use_col = (lane // (2 * ROPE_PAIRS)) == 1
    first = (lane % (2 * ROPE_PAIRS)) < ROPE_PAIRS
    freq = inv[lane % ROPE_PAIRS]
    ang = (np.where(use_col[None, :], col_id[:, None], row_id[:, None]) * freq[None, :]).astype(np.float32)
    cos, sin = np.cos(ang).astype(np.float32), np.sin(ang).astype(np.float32)
    zero = np.float32(0.0)
    return tuple(jnp.asarray(t) for t in (cos, np.where(first[None, :], -sin, zero), np.where(first[None, :], zero, sin)))


def _state_to_block_diag(s):
    eye = jnp.eye(A_HEADS, dtype=s.dtype)
    st = jnp.swapaxes(s, -1, -2)
    bd = st[..., :, :, None, :] * eye[:, None, :, None]
    return bd.reshape(s.shape[:-3] + (A_WIDTH, A_WIDTH))


def kernel(x_prompt, x_sample, cache_k, cache_v, state_hgrn, c, c_ctx, w_mod, b_mod, ln_g, ln_b, ffn1_w_in, ffn1_w_out, ffn2_w_in, ffn2_w_out, w_in, w_out, hgrn_lb_logits, hgrn_norm_g, conv_w, conv_b, conv_ln_g, conv_ln_b, q_norm_g, k_norm_g):
    batch, seq, _ = x_prompt.shape
    dec_batch, dec_seq, _ = x_sample.shape
    past = cache_k.shape[2]
    assert (batch * seq) % ROW_TILE == 0 and dec_seq % ROW_TILE == 0
    assert dec_batch + 1 <= SUBLANES
    assert KV_WIDTH == LANES and C_HEADS % (2 * C_KV_HEADS) == 0

    bf = lambda w: w.astype(BF16)
    ffn1_w_in, ffn1_w_out, w_in = map(bf, (ffn1_w_in, ffn1_w_out, w_in))

    conds = jnp.zeros((SUBLANES, D_MODEL), F32).at[0].set(c_ctx).at[1:1 + dec_batch].set(c)
    mod = _modulation(conds, w_mod, b_mod).reshape(DEPTH, SUBLANES, N_MOD, D_MODEL)

    ln_g3 = ln_g.reshape(DEPTH * 3, 1, D_MODEL)
    ln_b3 = ln_b.reshape(DEPTH * 3, 1, D_MODEL)
    qg = jnp.tile(q_norm_g, (1, C_HEADS)).reshape(DEPTH, 1, C_WIDTH)
    kg = jnp.tile(k_norm_g, (1, C_KV_HEADS)).reshape(DEPTH, 1, KV_WIDTH)
    norm_g = hgrn_norm_g.reshape(DEPTH, 1, A_WIDTH)
    conv_b3 = conv_b.reshape(DEPTH, 1, B_WIDTH)
    conv_g3 = conv_ln_g.reshape(DEPTH, 1, B_WIDTH)
    conv_beta3 = conv_ln_b.reshape(DEPTH, 1, B_WIDTH)
    tables = _rope_tables(dec_seq)
    zero_state = jnp.zeros((1, 2, A_WIDTH, A_WIDTH), F32)
    lat_state = _state_to_block_diag(state_hgrn)

    ctx_cond = lambda row: 0
    lat_cond = lambda row: 1 + row // dec_seq

    xp = x_prompt.reshape(batch * seq, D_MODEL)
    xs = x_sample.reshape(dec_batch * dec_seq, D_MODEL)
    ks, vs, new_states = [], [], "new"
    for l in range(DEPTH):
        mod_l = mod[l]

        def layer_fn(x, cond, sq, rope_tables, s0, k_past, v_past, states):
            x, zh, u, q, k, v = _ffn(x, mod_l, cond, ffn1_w_in, ffn1_w_out, ln_g3, ln_b3, l, 0,
                                     mixer_in=(w_in, qg, kg, rope_tables, sq))
            o_a, s_fin = _hgrn(zh, s0, hgrn_lb_logits, norm_g, l, sq,
                               seqs_per_step=HGRN_SEQS_PER_STEP, states=states)
            o_b = _conv(u, conv_w, conv_b3, conv_g3, conv_beta3, l, sq)
            if k_past is None:
                o_c = _attention(q, k, v, sq, sq, seqs_per_step=ATTN_SHORT_SEQS_PER_STEP)
            else:
                o_c = _attention(q, k, v, sq, ATTN_Q_TILE_LONG, past=(k_past, v_past))
            x, = _ffn(x, mod_l, cond, ffn2_w_in, ffn2_w_out, ln_g3, ln_b3, l, 2, mixer_out=(o_a, o_b, o_c, w_out))
            return x, k, v, s_fin

        xp, k_l, v_l, new_states = layer_fn(xp, ctx_cond, seq, None, zero_state, None, None, new_states)
        ks.append(k_l.reshape(batch, seq, C_KV_HEADS, HEAD_DIM))
        vs.append(v_l.reshape(batch, seq, C_KV_HEADS, HEAD_DIM))
        xs, _, _, _ = layer_fn(xs, lat_cond, dec_seq, tables, lat_state[:, l],
                               cache_k[:, l].reshape(dec_batch, past, KV_WIDTH),
                               cache_v[:, l].reshape(dec_batch, past, KV_WIDTH), "own")

    return (xp.reshape(batch, seq, D_MODEL), xs.reshape(dec_batch, dec_seq, D_MODEL),
            jnp.stack(ks, axis=1), jnp.stack(vs, axis=1), new_states)
```

```python
import functools

import numpy as np
import jax
import jax.numpy as jnp
from jax import lax
from jax.experimental import pallas as pl
from jax.experimental.pallas import tpu as pltpu

F32 = jnp.float32
BF16 = jnp.bfloat16

D_MODEL = 1024
DEPTH = 2
GRID_W = 64
HEAD_DIM = 64
A_HEADS = 4
A_DK = 64
A_WIDTH = 256
B_WIDTH = 256
CONV_K = 31
CONV_PAD = 15
C_HEADS = 8
C_KV_HEADS = 2
C_WIDTH = 512
KV_WIDTH = C_KV_HEADS * HEAD_DIM
D_FF = 2816
ROPE_THETA = 10000.0
ROPE_PAIRS = 16
N_MOD = 9
ALPHA = (2 * DEPTH) ** 0.25
F_MIN = 1e-6
LOG2_E = 1.4426950408889634
IN_WIDTH = 5 * A_WIDTH + 2 * B_WIDTH + C_WIDTH + 2 * KV_WIDTH
HG_WIDTH = 5 * A_WIDTH

LANES = 128
SUBLANES = 8
VMEM_BYTES_V7X = 64 * 1024 * 1024
VMEM_RESERVE_BYTES = 4 * 1024 * 1024

ROW_TILE = 512
FFN_CHUNK = 256
HGRN_CHUNK = 128
HGRN_BASE = 8
HGRN_NORM_ROWS = 256
HGRN_SEQS_PER_STEP = 1
CONV_TILE = 256
CONV_HALO = 16
ATTN_Q_TILE_LONG = 256
ATTN_KEY_BLOCK = 512
ATTN_SHORT_SEQS_PER_STEP = 4


def _params(semantics, vmem_mb):
    return pltpu.CompilerParams(dimension_semantics=semantics,
                                vmem_limit_bytes=min(vmem_mb << 20, VMEM_BYTES_V7X - VMEM_RESERVE_BYTES))


def _resident(block_shape, index_map):
    return pl.BlockSpec(block_shape, index_map, pipeline_mode=pl.Buffered(1))


def _sigmoid(x):
    return 1.0 / (1.0 + jnp.exp(-x))


def _layer_norm(y, g, b, eps=1e-5):
    mu = jnp.mean(y, axis=-1, keepdims=True)
    d = y - mu
    var = jnp.mean(d * d, axis=-1, keepdims=True)
    return d * lax.rsqrt(var + eps) * g + b


def _dot(a, b):
    return jnp.dot(a, b, preferred_element_type=F32)


def _dot_nt(a, b):
    return lax.dot_general(a, b, (((1,), (1,)), ((), ())), preferred_element_type=F32)


def _dot_tn(a, b):
    return lax.dot_general(a, b, (((0,), (0,)), ((), ())), preferred_element_type=F32)


def _segment_ones(width, seg):
    r = lax.broadcasted_iota(jnp.int32, (width, width), 0) // seg
    c = lax.broadcasted_iota(jnp.int32, (width, width), 1) // seg
    return (r == c).astype(BF16)


def _segment_sum(x, ones_bd):
    hi = x.astype(BF16)
    lo = (x - hi.astype(F32)).astype(BF16)
    return _dot(hi, ones_bd) + _dot(lo, ones_bd)


def _head_rms_norm(x, gain, eps=1e-6):
    ones_bd = _segment_ones(LANES, HEAD_DIM)
    cols = []
    for c in range(x.shape[1] // LANES):
        xc = x[:, c * LANES:(c + 1) * LANES]
        ms = _segment_sum(xc * xc, ones_bd) * (1.0 / HEAD_DIM)
        cols.append(xc * lax.rsqrt(ms + eps) * gain[:, c * LANES:(c + 1) * LANES])
    return cols[0] if len(cols) == 1 else jnp.concatenate(cols, axis=1)


def _mod_kernel(c_ref, w_ref, b_ref, o_ref):
    c = c_ref[...]
    s = (c * _sigmoid(c)).astype(BF16)
    o_ref[...] = _dot(s, w_ref[...].astype(BF16)) + b_ref[...]


def _modulation(conds, w_mod, b_mod):
    tn = D_MODEL
    return pl.pallas_call(
        _mod_kernel,
        grid=(DEPTH, N_MOD * D_MODEL // tn),
        in_specs=[
            pl.BlockSpec((SUBLANES, D_MODEL), lambda l, j: (0, 0)),
            pl.BlockSpec((None, D_MODEL, tn), lambda l, j: (l, 0, j)),
            pl.BlockSpec((None, 1, tn), lambda l, j: (l, 0, j)),
        ],
        out_specs=pl.BlockSpec((None, SUBLANES, tn), lambda l, j: (l, 0, j)),
        out_shape=jax.ShapeDtypeStruct((DEPTH, SUBLANES, N_MOD * D_MODEL), F32),
        compiler_params=_params(("arbitrary", "arbitrary"), 32),
        name="modulation",
    )(conds, w_mod, b_mod.reshape(DEPTH, 1, N_MOD * D_MODEL))


def _ffn_kernel(*refs, sub, with_mixer_out, mixer_in_rope):
    refs = list(refs)
    x_ref = refs.pop(0)
    if with_mixer_out:
        oa_ref, ob_ref, oc_ref = refs[:3]
        del refs[:3]
    mod_ref = refs.pop(0)
    if with_mixer_out:
        wmix_ref, gmix_ref, bmix_ref = refs[:3]
        del refs[:3]
    wg_ref, wu_ref, wo_ref, g_ref, b_ref = refs[:5]
    del refs[:5]
    if mixer_in_rope is not None:
        n_in = 6 if mixer_in_rope else 3
        mixer_in_refs, refs = refs[:n_in], refs[n_in:]
    o_ref = refs.pop(0)
    mixer_in_outs = refs

    if with_mixer_out:
        y = _dot(oa_ref[...], wmix_ref[0:A_WIDTH, :].astype(BF16))
        y = y + _dot(ob_ref[...], wmix_ref[A_WIDTH:A_WIDTH + B_WIDTH, :].astype(BF16))
        y = y + _dot(oc_ref[...], wmix_ref[A_WIDTH + B_WIDTH:, :].astype(BF16))
        x = _layer_norm(ALPHA * x_ref[...] + mod_ref[5:6, :] * y, gmix_ref[...], bmix_ref[...])
    else:
        x = x_ref[...]
    shift = mod_ref[3 * sub:3 * sub + 1, :]
    scale = mod_ref[3 * sub + 1:3 * sub + 2, :]
    gate = mod_ref[3 * sub + 2:3 * sub + 3, :]
    h = (x * (1.0 + scale) + shift).astype(BF16)
    acc = jnp.zeros(x.shape, F32)
    for j in range(D_FF // FFN_CHUNK):
        cols = slice(j * FFN_CHUNK, (j + 1) * FFN_CHUNK)
        gt = _dot(h, wg_ref[:, cols].astype(BF16))
        up = _dot(h, wu_ref[:, cols].astype(BF16))
        act = (gt * _sigmoid(gt) * up).astype(BF16)
        acc = acc + _dot(act, wo_ref[cols, :].astype(BF16))
    y = ALPHA * x + 0.5 * gate * acc
    x_new = _layer_norm(y, g_ref[...], b_ref[...])
    o_ref[...] = x_new
    if mixer_in_rope is not None:
        _mixer_in(x_new, mod_ref, *mixer_in_refs, *mixer_in_outs)


def _ffn(x, mod_l, cond_of_row, w_in, w_out, ln_g, ln_b, layer, sub, mixer_out=None, mixer_in=None):
    rows = x.shape[0]
    tile = ROW_TILE
    row_spec = lambda width: pl.BlockSpec((tile, width), lambda i: (i, 0))
    cond_of_block = lambda i: cond_of_row(i * tile)
    ln_spec = lambda idx: _resident((None, 1, D_MODEL), lambda i: (layer * 3 + idx, 0, 0))
    in_specs, args = [row_spec(D_MODEL)], [x]
    if mixer_out is not None:
        o_a, o_b, o_c, w_mix = mixer_out
        in_specs += [row_spec(A_WIDTH), row_spec(B_WIDTH), row_spec(C_WIDTH)]
        args += [o_a, o_b, o_c]
    in_specs.append(pl.BlockSpec((None, N_MOD, D_MODEL), lambda i: (cond_of_block(i), 0, 0)))
    args.append(mod_l)
    if mixer_out is not None:
        in_specs += [_resident((None, D_MODEL, D_MODEL), lambda i: (layer, 0, 0)), ln_spec(1), ln_spec(1)]
        args += [w_mix, ln_g, ln_b]
    in_specs += [
        _resident((None, D_MODEL, D_FF), lambda i: (layer, 0, 0)),
        _resident((None, D_MODEL, D_FF), lambda i: (layer, 0, 1)),
        _resident((None, D_FF, D_MODEL), lambda i: (layer, 0, 0)),
        ln_spec(sub), ln_spec(sub),
    ]
    args += [w_in, w_in, w_out, ln_g, ln_b]
    out_specs = [row_spec(D_MODEL)]
    out_shape = [jax.ShapeDtypeStruct((rows, D_MODEL), F32)]
    rope = None
    if mixer_in is not None:
        w_mix_in, qg, kg, rope_tables, seq = mixer_in
        rope = rope_tables is not None
        in_specs += [
            _resident((None, D_MODEL, IN_WIDTH), lambda i: (layer, 0, 0)),
            _resident((None, 1, C_WIDTH), lambda i: (layer, 0, 0)),
            _resident((None, 1, KV_WIDTH), lambda i: (layer, 0, 0)),
        ]
        args += [w_mix_in, qg, kg]
        if rope:
            per_seq = seq // tile
            in_specs += [pl.BlockSpec((tile, LANES), lambda i: (i % per_seq, 0))] * 3
            args += list(rope_tables)
        widths = ((HG_WIDTH, F32), (B_WIDTH, F32), (C_WIDTH, BF16), (KV_WIDTH, F32), (KV_WIDTH, F32))
        out_specs += [row_spec(w) for w, _ in widths]
        out_shape += [jax.ShapeDtypeStruct((rows, w), dt) for w, dt in widths]
    return pl.pallas_call(
        functools.partial(_ffn_kernel, sub=sub, with_mixer_out=mixer_out is not None, mixer_in_rope=rope),
        grid=(rows // tile,),
        in_specs=in_specs,
        out_specs=out_specs,
        out_shape=out_shape,
        compiler_params=_params(("parallel",), 56),
        name=f"ffn{sub // 2 + 1}",
    )(*args)


def _mixer_in(x, mod_ref, w_ref, qg_ref, kg_ref, *refs):
    rope = len(refs) == 8
    if rope:
        cos_ref, s1_ref, s2_ref = refs[:3]
    zh_ref, u_ref, q_ref, k_ref, v_ref = refs[-5:]
    h = (x * (1.0 + mod_ref[4:5, :]) + mod_ref[3:4, :]).astype(BF16)

    zh_ref[...] = _dot(h, w_ref[:, 0:HG_WIDTH])
    o = HG_WIDTH
    glu_a = _dot(h, w_ref[:, o:o + B_WIDTH])
    glu_b = _dot(h, w_ref[:, o + B_WIDTH:o + 2 * B_WIDTH])
    u_ref[...] = glu_a * _sigmoid(glu_b)
    o += 2 * B_WIDTH
    cq = _dot(h, w_ref[:, o:o + C_WIDTH])
    ck = _dot(h, w_ref[:, o + C_WIDTH:o + C_WIDTH + KV_WIDTH])
    v_ref[...] = _dot(h, w_ref[:, o + C_WIDTH + KV_WIDTH:o + C_WIDTH + 2 * KV_WIDTH])

    qn = _head_rms_norm(cq, qg_ref[...])
    kn = _head_rms_norm(ck, kg_ref[...])
    if rope:
        cos, s1, s2 = cos_ref[...], s1_ref[...], s2_ref[...]

        def rot(t):
            cols = []
            for c in range(t.shape[1] // LANES):
                tc = t[:, c * LANES:(c + 1) * LANES]
                cols.append(tc * cos + pltpu.roll(tc, LANES - ROPE_PAIRS, 1) * s1
                            + pltpu.roll(tc, ROPE_PAIRS, 1) * s2)
            return cols[0] if len(cols) == 1 else jnp.concatenate(cols, axis=1)

        qn, kn = rot(qn), rot(kn)
    q_ref[...] = (qn * (HEAD_DIM ** -0.5 * LOG2_E)).astype(BF16)
    k_ref[...] = kn


def _cumsum_rows(x, reverse):
    n, w = x.shape
    row = lax.broadcasted_iota(jnp.int32, (SUBLANES, w), 0)
    groups = []
    for g in range(n // SUBLANES):
        grp = x[g * SUBLANES:(g + 1) * SUBLANES, :]
        s = 1
        while s < SUBLANES:
            if reverse:
                grp = grp + jnp.where(row < SUBLANES - s, pltpu.roll(grp, SUBLANES - s, 0), 0.0)
            else:
                grp = grp + jnp.where(row >= s, pltpu.roll(grp, s, 0), 0.0)
            s *= 2
        groups.append(grp)
    order = range(len(groups) - 1, -1, -1) if reverse else range(len(groups))
    carry = None
    for g in order:
        if carry is not None:
            groups[g] = groups[g] + carry
        carry = groups[g][0:1, :] if reverse else groups[g][SUBLANES - 1:SUBLANES, :]
    return jnp.concatenate(groups, axis=0)


def _hgrn_kernel(zh_ref, s0_ref, lbl_ref, ng_ref, *refs, layer, seq):
    o_ref, sfin_ref, acc_ref, qe_ref, decay_ref, kv_ref = refs[-6:]
    C, W = HGRN_CHUNK, A_WIDTH
    n_chunks = seq // C
    n_seq = zh_ref.shape[0] // seq

    lg = lbl_ref[...]
    e = jnp.exp(lg - jnp.max(lg, axis=0, keepdims=True))
    soft = e / jnp.sum(e, axis=0, keepdims=True)
    lb = jnp.zeros((1, W), F32)
    for j in range(1, layer + 1):
        lb = lb + soft[j:j + 1, :]

    lane_head = lax.broadcasted_iota(jnp.int32, (1, W), 1) // A_DK
    head_masks = [lane_head == h for h in range(A_HEADS)]
    t_idx = lax.broadcasted_iota(jnp.int32, (C, A_HEADS * C), 0)
    s_idx = lax.broadcasted_iota(jnp.int32, (C, A_HEADS * C), 1) % C
    pair_level = t_idx ^ s_idx
    key_not_after = s_idx <= t_idx
    key_not_before = s_idx >= t_idx
    bd_mask = (lax.broadcasted_iota(jnp.int32, (W, W), 0) // A_DK
               == lax.broadcasted_iota(jnp.int32, (W, W), 1) // A_DK)
    def stack_heads(t_bf):
        return jnp.concatenate([jnp.where(m, t_bf, jnp.zeros_like(t_bf)) for m in head_masks], axis=0)

    def chunk_rows(n):
        return pl.ds(pl.multiple_of(n * C, C), C)

    def gates(rows, reverse):
        zf = zh_ref[rows, (3 if reverse else 2) * W:(4 if reverse else 3) * W]
        f = lb + (1.0 - lb) * _sigmoid(zf)
        return 1.0 - f, _cumsum_rows(jnp.log(jnp.maximum(f, F_MIN)), reverse) * LOG2_E

    def intra_body(n, carry):
        rows = chunk_rows(n)
        q = zh_ref[rows, 0:W]
        v_bf = zh_ref[rows, W:2 * W].astype(BF16)
        k_f, a_f = gates(rows, False)
        k_b, a_b = gates(rows, True)

        scores = jnp.zeros((C, A_HEADS * C), F32)
        h = C // 2
        while h >= HGRN_BASE:
            split = lambda t: t.reshape(C // (2 * h), 2, h, W)
            q4, af4, ab4, kf4, kb4 = split(q), split(a_f), split(a_b), split(k_f), split(k_b)
            end_first = af4[:, 0, h - 1:h, :]
            start_second = ab4[:, 1, 0:1, :]
            q_fwd = q4[:, 1] * jnp.exp2(af4[:, 1] - end_first)
            k_fwd = kf4[:, 0] * jnp.exp2(end_first - af4[:, 0])
            q_bwd = q4[:, 0] * jnp.exp2(ab4[:, 0] - start_second)
            k_bwd = kb4[:, 1] * jnp.exp2(start_second - ab4[:, 1])
            qt = jnp.stack([q_bwd, q_fwd], axis=1).reshape(C, W).astype(BF16)
            kt = jnp.stack([k_fwd, k_bwd], axis=1).reshape(C, W).astype(BF16)
            scores = jnp.where(pair_level < 2 * h, _dot_nt(qt, stack_heads(kt)), scores)
            h //= 2
        hb = HGRN_BASE
        block = lambda t: t.reshape(C // hb, hb, W)
        base = []
        for a, kk, ref_at, keep in ((a_f, k_f, hb // 2 - 1, key_not_after), (a_b, k_b, hb // 2, key_not_before)):
            a3 = block(a)
            ref_row = a3[:, ref_at:ref_at + 1, :]
            qt = (block(q) * jnp.exp2(a3 - ref_row)).reshape(C, W).astype(BF16)
            kt = (block(kk) * jnp.exp2(ref_row - a3)).reshape(C, W).astype(BF16)
            base.append(jnp.where(keep, _dot_nt(qt, stack_heads(kt)), 0.0))
        scores = jnp.where(pair_level < hb, base[0] + base[1], scores)
        acc_ref[rows, :] = _dot(scores.astype(BF16), stack_heads(v_bf))

        for d, (a, kk, last) in enumerate(((a_f, k_f, C - 1), (a_b, k_b, 0))):
            a_last = a[last:last + 1, :]
            qe_ref[d, rows, :] = (q * jnp.exp2(a)).astype(BF16)
            decay_ref[n, d:d + 1, :] = a_last
            kv_ref[n, d] = jnp.where(bd_mask, _dot_tn(v_bf, (kk * jnp.exp2(a_last - a)).astype(BF16)), 0.0)
        return carry

    lax.fori_loop(0, n_seq * n_chunks, intra_body, 0, unroll=min(n_seq * n_chunks, 8))

    def state_body(i, carry):
        new = []
        for sq, states in enumerate(carry):
            first = sq * n_chunks
            for d, (st, n) in enumerate(zip(states, (first + i, first + n_chunks - 1 - i))):
                rows = chunk_rows(n)
                acc_ref[rows, :] += _dot_nt(qe_ref[d, rows, :], st.astype(BF16))
                new.append(st * jnp.exp2(decay_ref[n, d:d + 1, :]) + kv_ref[n, d])
        return tuple((new[2 * sq], new[2 * sq + 1]) for sq in range(n_seq))

    entry = [s0_ref[sq if s0_ref.shape[0] > 1 else 0] for sq in range(n_seq)]
    finals = lax.fori_loop(0, n_chunks, state_body, tuple((s[0], s[1]) for s in entry),
                           unroll=min(n_chunks, 4))

    ones_bd = _segment_ones(LANES, A_DK)

    def norm_body(n, carry):
        rows = pl.ds(pl.multiple_of(n * HGRN_NORM_ROWS, HGRN_NORM_ROWS), HGRN_NORM_ROWS)
        tot = acc_ref[rows, :]
        zg = zh_ref[rows, 4 * W:5 * W]
        cols = []
        for c in range(W // LANES):
            lanes = slice(c * LANES, (c + 1) * LANES)
            tc = tot[:, lanes]
            ms = _segment_sum(tc * tc, ones_bd) * (1.0 / A_DK)
            cols.append(tc * lax.rsqrt(ms + 1e-6) * ng_ref[:, lanes])
        on = jnp.concatenate(cols, axis=1)
        o_ref[rows, :] = (on * (zg * _sigmoid(zg))).astype(o_ref.dtype)
        return carry

    lax.fori_loop(0, n_seq * seq // HGRN_NORM_ROWS, norm_body, 0)

    for sq, states in enumerate(finals):
        for d, st in enumerate(states):
            s_kv = st.T
            for h in range(A_HEADS):
                blk = slice(h * A_DK, (h + 1) * A_DK)
                if sfin_ref.ndim == 6:
                    for other in range(DEPTH):
                        sfin_ref[sq, other, d, h] = s_kv[blk, blk] if other == layer else jnp.zeros((A_DK, A_DK), F32)
                else:
                    sfin_ref[sq, d, h] = s_kv[blk, blk]


def _hgrn(zh, s0, lb_logits, norm_g, layer, seq, seqs_per_step=1, states="own"):
    rows = zh.shape[0]
    n_seq = rows // seq
    S, W = seqs_per_step, A_WIDTH
    if s0.shape[0] == 1:
        s0_spec = pl.BlockSpec((1, 2, W, W), lambda b: (0, 0, 0, 0))
    else:
        s0_spec = pl.BlockSpec((S, 2, W, W), lambda b: (b, 0, 0, 0))
    chunks = S * seq // HGRN_CHUNK
    in_specs = [
        pl.BlockSpec((S * seq, HG_WIDTH), lambda b: (b, 0)),
        s0_spec,
        _resident((DEPTH, W), lambda b: (0, 0)),
        _resident((None, 1, W), lambda b: (layer, 0, 0)),
    ]
    args = [zh, s0, lb_logits, norm_g]
    aliases = {}
    one_layer = (2, A_HEADS, A_DK, A_DK)
    if isinstance(states, str) and states == "own":
        state_spec = pl.BlockSpec((S,) + one_layer, lambda b: (b, 0, 0, 0, 0))
        state_shape = jax.ShapeDtypeStruct((n_seq,) + one_layer, F32)
    else:
        state_shape = jax.ShapeDtypeStruct((n_seq, DEPTH) + one_layer, F32)
        if isinstance(states, str):
            state_spec = pl.BlockSpec((S, DEPTH) + one_layer, lambda b: (b, 0, 0, 0, 0, 0))
        else:
            state_spec = pl.BlockSpec((S, None) + one_layer, lambda b: (b, layer, 0, 0, 0, 0))
            in_specs.append(pl.BlockSpec(memory_space=pl.ANY))
            args.append(states)
            aliases = {len(args) - 1: 1}
    return pl.pallas_call(
        functools.partial(_hgrn_kernel, layer=layer, seq=seq),
        grid=(n_seq // S,),
        in_specs=in_specs,
        out_specs=[pl.BlockSpec((S * seq, W), lambda b: (b, 0)), state_spec],
        out_shape=[jax.ShapeDtypeStruct((rows, W), BF16), state_shape],
        input_output_aliases=aliases,
        scratch_shapes=[pltpu.VMEM((S * seq, W), F32), pltpu.VMEM((2, S * seq, W), BF16),
                        pltpu.VMEM((chunks, 2, W), F32), pltpu.VMEM((chunks, 2, W, W), F32)],
        compiler_params=_params(("parallel",), 48),
        name="hgrn2",
    )(*args)


def _conv_kernel(up_ref, uc_ref, un_ref, w_ref, b_ref, g_ref, beta_ref, o_ref, sh_ref, *, tiles_per_seq):
    i = pl.program_id(0)
    T, H, S = CONV_TILE, CONV_HALO, SUBLANES
    P = T + 2 * H
    first = (i % tiles_per_seq) == 0
    last = (i % tiles_per_seq) == tiles_per_seq - 1
    sh_ref[0, 0:H, :] = jnp.where(first, 0.0, up_ref[...])
    sh_ref[0, H:H + T, :] = uc_ref[...]
    sh_ref[0, H + T:P, :] = jnp.where(last, 0.0, un_ref[...])
    for b in range(1, S):
        sh_ref[b, 0:P - S, :] = sh_ref[0, b:b + P - S, :]
    acc = jnp.zeros((T, B_WIDTH), F32)
    for j in range(CONV_K):
        start = H - CONV_PAD + j
        a, b = start // S, start % S
        acc = acc + w_ref[j:j + 1, :] * sh_ref[b, a * S:a * S + T, :]
    y = _layer_norm(acc + b_ref[...], g_ref[...], beta_ref[...])
    o_ref[...] = (y * _sigmoid(y)).astype(o_ref.dtype)


def _conv(u, conv_w, conv_b, ln_g, ln_b, layer, seq):
    rows = u.shape[0]
    T, H = CONV_TILE, CONV_HALO
    tiles_per_seq = seq // T
    ratio = T // H
    n_halo_blocks = rows // H
    return pl.pallas_call(
        functools.partial(_conv_kernel, tiles_per_seq=tiles_per_seq),
        grid=(rows // T,),
        in_specs=[
            pl.BlockSpec((H, B_WIDTH), lambda i: (jnp.maximum(i * ratio - 1, 0), 0)),
            pl.BlockSpec((T, B_WIDTH), lambda i: (i, 0)),
            pl.BlockSpec((H, B_WIDTH), lambda i: (jnp.minimum((i + 1) * ratio, n_halo_blocks - 1), 0)),
            _resident((None, CONV_K, B_WIDTH), lambda i: (layer, 0, 0)),
            _resident((None, 1, B_WIDTH), lambda i: (layer, 0, 0)),
            _resident((None, 1, B_WIDTH), lambda i: (layer, 0, 0)),
            _resident((None, 1, B_WIDTH), lambda i: (layer, 0, 0)),
        ],
        out_specs=pl.BlockSpec((T, B_WIDTH), lambda i: (i, 0)),
        out_shape=jax.ShapeDtypeStruct((rows, B_WIDTH), BF16),
        scratch_shapes=[pltpu.VMEM((SUBLANES, T + 2 * H, B_WIDTH), F32)],
        compiler_params=_params(("parallel",), 16),
        name="conv_module",
    )(u, u, u, conv_w, conv_b, ln_g, ln_b)


def _attn_kernel(*refs, n_seq, single_q_tile, with_past):
    if with_past:
        q_ref, k_ref, v_ref, kp_ref, vp_ref, o_ref, kd_ref, va_ref = refs
        sources = ((kp_ref, vp_ref), (k_ref, v_ref))
    else:
        q_ref, k_ref, v_ref, o_ref, kd_ref, va_ref = refs
        sources = ((k_ref, v_ref),)
    tq = q_ref.shape[0] // n_seq
    seq_k = kd_ref.shape[1] // n_seq
    kb = min(seq_k, ATTN_KEY_BLOCK)
    group = C_HEADS // C_KV_HEADS
    half = lax.broadcasted_iota(jnp.int32, (1, LANES), 1) // HEAD_DIM

    def prepare():
        start = 0
        for ks_ref, vs_ref in sources:
            rows = slice(start, start + ks_ref.shape[0])
            start += ks_ref.shape[0]
            k = ks_ref[...]
            k_swapped = pltpu.roll(k, HEAD_DIM, 1)
            v = vs_ref[...]
            v_swapped = pltpu.roll(v, HEAD_DIM, 1)
            for g in range(C_KV_HEADS):
                kd_ref[g, rows, :] = jnp.where(half == g, k, k_swapped).astype(BF16)
                va_ref[g, rows, 0:LANES] = jnp.where(half == g, v, v_swapped).astype(BF16)
                va_ref[g, rows, LANES:2 * LANES] = jnp.ones(v.shape, BF16)

    if single_q_tile:
        prepare()
    else:
        pl.when(pl.program_id(1) == 0)(prepare)

    n_blocks = seq_k // kb
    chains = [(sq, g) for sq in range(n_seq) for g in range(C_KV_HEADS)]
    units = [(c, b) for b in range(n_blocks) for c in range(len(chains))]
    q_stacked, run_max, acc = {}, {}, {}
    scores, probs, rescale = {}, {}, {}

    def stage_scores(u):
        c, b = units[u]
        sq, g = chains[c]
        if c not in q_stacked:
            heads = []
            for j in range(group):
                hq = g * group + j
                col = q_ref[sq * tq:(sq + 1) * tq, (hq // 2) * LANES:(hq // 2 + 1) * LANES]
                heads.append(jnp.where(half == hq % 2, col, jnp.zeros_like(col)))
            q_stacked[c] = jnp.concatenate(heads, axis=0)
        keys = slice(sq * seq_k + b * kb, sq * seq_k + (b + 1) * kb)
        scores[u] = _dot_nt(q_stacked[c], kd_ref[g, keys, :])

    def stage_exp(u):
        c, b = units[u]
        s = scores.pop(u)
        m_blk = jnp.max(s, axis=-1, keepdims=True)
        m_new = m_blk if b == 0 else jnp.maximum(run_max[c], m_blk)
        probs[u] = jnp.exp2(s - m_new).astype(BF16)
        if b > 0:
            rescale[u] = jnp.exp2(run_max[c] - m_new)
        run_max[c] = m_new

    def stage_values(u):
        c, b = units[u]
        sq, g = chains[c]
        keys = slice(sq * seq_k + b * kb, sq * seq_k + (b + 1) * kb)
        pv = _dot(probs.pop(u), va_ref[g, keys, :])
        acc[c] = pv if b == 0 else rescale.pop(u) * acc[c] + pv
        if b == n_blocks - 1:
            total = acc.pop(c)
            o = total[:, 0:LANES] / total[:, LANES:2 * LANES]
            for j in range(group):
                hq = g * group + j
                o_ref[sq * tq:(sq + 1) * tq, hq * HEAD_DIM:(hq + 1) * HEAD_DIM] = (
                    o[j * tq:(j + 1) * tq, (hq % 2) * HEAD_DIM:(hq % 2 + 1) * HEAD_DIM].astype(o_ref.dtype))

    for t in range(len(units) + 2):
        if t < len(units):
            stage_scores(t)
        if 0 <= t - 1 < len(units):
            stage_exp(t - 1)
        if 0 <= t - 2 < len(units):
            stage_values(t - 2)


def _attention(q, k, v, seq, tq, seqs_per_step=1, past=None):
    n_seq = q.shape[0] // seq
    q_tiles = seq // tq
    assert seqs_per_step == 1 or (q_tiles == 1 and past is None)
    q_rows, k_rows = seqs_per_step * tq, seqs_per_step * seq
    in_specs = [
        pl.BlockSpec((q_rows, C_WIDTH), lambda b, i: (b * q_tiles + i, 0)),
        pl.BlockSpec((k_rows, KV_WIDTH), lambda b, i: (b, 0)),
        pl.BlockSpec((k_rows, KV_WIDTH), lambda b, i: (b, 0)),
    ]
    args = [q, k, v]
    keys_per_step = k_rows
    if past is not None:
        n_past = past[0].shape[1]
        in_specs += [pl.BlockSpec((None, n_past, KV_WIDTH), lambda b, i: (b, 0, 0))] * 2
        args += list(past)
        keys_per_step += n_past
    return pl.pallas_call(
        functools.partial(_attn_kernel, n_seq=seqs_per_step, single_q_tile=q_tiles == 1, with_past=past is not None),
        grid=(n_seq // seqs_per_step, q_tiles),
        in_specs=in_specs,
        out_specs=pl.BlockSpec((q_rows, C_WIDTH), lambda b, i: (b * q_tiles + i, 0)),
        out_shape=jax.ShapeDtypeStruct(q.shape, BF16),
        scratch_shapes=[pltpu.VMEM((C_KV_HEADS, keys_per_step, LANES), BF16),
                        pltpu.VMEM((C_KV_HEADS, keys_per_step, 2 * LANES), BF16)],
        compiler_params=_params(("parallel", "arbitrary"), 48),
        name="attention",
    )(*args)


def _rope_tables(seq):
    pos = np.arange(seq)
    row_id = (pos // GRID_W).astype(np.float32)
    col_id = (pos % GRID_W).astype(np.float32)
    inv = (np.float32(ROPE_THETA) ** (-np.arange(ROPE_PAIRS, dtype=np.float32) / ROPE_PAIRS)).astype(np.float32)
    lane = np.arange(LANES) % HEAD_DIM
    use_col = (lane // (2 * ROPE_PAIRS)) == 1
    first = (lane % (2 * ROPE_PAIRS)) < ROPE_PAIRS
    freq = inv[lane % ROPE_PAIRS]
    ang = (np.where(use_col[None, :], col_id[:, None], row_id[:, None]) * freq[None, :]).astype(np.float32)
    cos, sin = np.cos(ang).astype(np.float32), np.sin(ang).astype(np.float32)
    zero = np.float32(0.0)
    return tuple(jnp.asarray(t) for t in (cos, np.where(first[None, :], -sin, zero), np.where(first[None, :], zero, sin)))


def _state_to_block_diag(s):
    eye = jnp.eye(A_HEADS, dtype=s.dtype)
    st = jnp.swapaxes(s, -1, -2)
    bd = st[..., :, :, None, :] * eye[:, None, :, None]
    return bd.reshape(s.shape[:-3] + (A_WIDTH, A_WIDTH))


def kernel(x_prompt, x_sample, cache_k, cache_v, state_hgrn, c, c_ctx, w_mod, b_mod, ln_g, ln_b, ffn1_w_in, ffn1_w_out, ffn2_w_in, ffn2_w_out, w_in, w_out, hgrn_lb_logits, hgrn_norm_g, conv_w, conv_b, conv_ln_g, conv_ln_b, q_norm_g, k_norm_g):
    batch, seq, _ = x_prompt.shape
    dec_batch, dec_seq, _ = x_sample.shape
    past = cache_k.shape[2]
    assert (batch * seq) % ROW_TILE == 0 and dec_seq % ROW_TILE == 0
    assert dec_batch + 1 <= SUBLANES
    assert KV_WIDTH == LANES and C_HEADS % (2 * C_KV_HEADS) == 0

    bf = lambda w: w.astype(BF16)
    ffn1_w_in, ffn1_w_out, w_in = map(bf, (ffn1_w_in, ffn1_w_out, w_in))

    conds = jnp.zeros((SUBLANES, D_MODEL), F32).at[0].set(c_ctx).at[1:1 + dec_batch].set(c)
    mod = _modulation(conds, w_mod, b_mod).reshape(DEPTH, SUBLANES, N_MOD, D_MODEL)

    ln_g3 = ln_g.reshape(DEPTH * 3, 1, D_MODEL)
    ln_b3 = ln_b.reshape(DEPTH * 3, 1, D_MODEL)
    qg = jnp.tile(q_norm_g, (1, C_HEADS)).reshape(DEPTH, 1, C_WIDTH)
    kg = jnp.tile(k_norm_g, (1, C_KV_HEADS)).reshape(DEPTH, 1, KV_WIDTH)
    norm_g = hgrn_norm_g.reshape(DEPTH, 1, A_WIDTH)
    conv_b3 = conv_b.reshape(DEPTH, 1, B_WIDTH)
    conv_g3 = conv_ln_g.reshape(DEPTH, 1, B_WIDTH)
    conv_beta3 = conv_ln_b.reshape(DEPTH, 1, B_WIDTH)
    tables = _rope_tables(dec_seq)
    zero_state = jnp.zeros((1, 2, A_WIDTH, A_WIDTH), F32)
    lat_state = _state_to_block_diag(state_hgrn)

    ctx_cond = lambda row: 0
    lat_cond = lambda row: 1 + row // dec_seq

    xp = x_prompt.reshape(batch * seq, D_MODEL)
    xs = x_sample.reshape(dec_batch * dec_seq, D_MODEL)
    ks, vs, new_states = [], [], "new"
    for l in range(DEPTH):
        mod_l = mod[l]

        def layer_fn(x, cond, sq, rope_tables, s0, k_past, v_past, states):
            x, zh, u, q, k, v = _ffn(x, mod_l, cond, ffn1_w_in, ffn1_w_out, ln_g3, ln_b3, l, 0,
                                     mixer_in=(w_in, qg, kg, rope_tables, sq))
            o_a, s_fin = _hgrn(zh, s0, hgrn_lb_logits, norm_g, l, sq,
                               seqs_per_step=HGRN_SEQS_PER_STEP, states=states)
            o_b = _conv(u, conv_w, conv_b3, conv_g3, conv_beta3, l, sq)
            if k_past is None:
                o_c = _attention(q, k, v, sq, sq, seqs_per_step=ATTN_SHORT_SEQS_PER_STEP)
            else:
                o_c = _attention(q, k, v, sq, ATTN_Q_TILE_LONG, past=(k_past, v_past))
            x, = _ffn(x, mod_l, cond, ffn2_w_in, ffn2_w_out, ln_g3, ln_b3, l, 2, mixer_out=(o_a, o_b, o_c, w_out))
            return x, k, v, s_fin

        xp, k_l, v_l, new_states = layer_fn(xp, ctx_cond, seq, None, zero_state, None, None, new_states)
        ks.append(k_l.reshape(batch, seq, C_KV_HEADS, HEAD_DIM))
        vs.append(v_l.reshape(batch, seq, C_KV_HEADS, HEAD_DIM))
        xs, _, _, _ = layer_fn(xs, lat_cond, dec_seq, tables, lat_state[:, l],
                               cache_k[:, l].reshape(dec_batch, past, KV_WIDTH),
                               cache_v[:, l].reshape(dec_batch, past, KV_WIDTH), "own")

    return (xp.reshape(batch, seq, D_MODEL), xs.reshape(dec_batch, dec_seq, D_MODEL),
            jnp.stack(ks, axis=1), jnp.stack(vs, axis=1), new_states)
```
